```python
import math
import jax, jax.numpy as jnp
from jax import lax
import numpy as np

D_MODEL = 1024
BATCH = 2
SEQ = 8192
DEPTH = 4
DEC_BATCH = 2
DEC_SEQ = 16384
PAST_LEN = 128

DIL_GROUPS = ((128, 1), (512, 4), (2048, 16))
N_GROUPS_A = 3
HEADS_PER_GROUP_A = 4
HEAD_DIM_A = 128
WIDTH_A = N_GROUPS_A * HEADS_PER_GROUP_A * HEAD_DIM_A
OUT_A = HEADS_PER_GROUP_A * HEAD_DIM_A
N_HEADS_A = N_GROUPS_A * HEADS_PER_GROUP_A
N_HEADS_B = 8
Q_LORA = 256
KV_LORA = 128
QK_NOPE = 64
QK_ROPE = 32
V_DIM_B = 64
OUT_B = N_HEADS_B * V_DIM_B
ROPE_THETA = 10000.0
Q_BLOCK = 128
N_MEM = 256
N_HEADS_C = 4
HEAD_DIM_C = 128
OUT_C = N_HEADS_C * HEAD_DIM_C
N_BRANCH = 3
BRANCH_WIDTH = 512
IN_SIZES = (WIDTH_A, WIDTH_A, WIDTH_A, Q_LORA, KV_LORA, QK_ROPE, OUT_C, N_BRANCH * D_MODEL)
N_IN = 3 * WIDTH_A + Q_LORA + KV_LORA + QK_ROPE + OUT_C + N_BRANCH * D_MODEL
N_EXPERTS = 64
TOP_K = 8
N_EXPERT_GROUPS = 8
TOPK_GROUPS = 4
D_EXPERT = D_MODEL // 4
ROUTED_SCALE = 2.5
DEEPNORM_ALPHA = (2 * DEPTH) ** 0.25
DEEPNORM_BETA = (8 * DEPTH) ** -0.25
LN_EPS = 1e-5
RMS_EPS = 1e-6

kernel_name = "hybrid_dilated_mla_memory_moe_encoder"


def _layer_norm(x, g, b):
    xf = x.astype(jnp.float32)
    mu = jnp.mean(xf, -1, keepdims=True)
    var = jnp.mean(jnp.square(xf - mu), -1, keepdims=True)
    return ((xf - mu) * lax.rsqrt(var + LN_EPS) * g.astype(jnp.float32) + b.astype(jnp.float32)).astype(x.dtype)


def _rms_norm(x, g):
    xf = x.astype(jnp.float32)
    return (xf * lax.rsqrt(jnp.mean(jnp.square(xf), -1, keepdims=True) + RMS_EPS) * g.astype(jnp.float32)).astype(x.dtype)


def _rope_tables(s):
    inv_freq = 1.0 / (ROPE_THETA ** (jnp.arange(0, QK_ROPE, 2, dtype=jnp.float32) / QK_ROPE))
    ang = jnp.arange(s, dtype=jnp.float32)[:, None] * inv_freq[None, :]
    return jnp.cos(ang), jnp.sin(ang)


def _rope(x, cos, sin):
    xf = x.astype(jnp.float32)
    x1, x2 = xf[..., :QK_ROPE // 2], xf[..., QK_ROPE // 2:]
    return jnp.concatenate([x1 * cos - x2 * sin, x2 * cos + x1 * sin], -1).astype(x.dtype)


def _band_attention(q, k, v, radius, slope_step):
    n, L, h, dh = q.shape
    blk = radius
    nb = -(-L // blk)
    pad = nb * blk - L
    qb = jnp.pad(q, ((0, 0), (0, pad), (0, 0), (0, 0))).reshape(n, nb, blk, h, dh)

    def windows(t):
        tb = jnp.pad(t, ((0, 0), (blk, pad + blk), (0, 0), (0, 0))).reshape(n, nb + 2, blk, h, dh)
        return jnp.concatenate([tb[:, :-2], tb[:, 1:-1], tb[:, 2:]], axis=2)

    kw, vw = windows(k), windows(v)
    rel = jnp.arange(3 * blk)[None, :] - blk - jnp.arange(blk)[:, None]
    key_pos = jnp.arange(nb)[:, None] * blk - blk + jnp.arange(3 * blk)[None, :]
    valid = (jnp.abs(rel) <= radius)[None] & ((key_pos >= 0) & (key_pos < L))[:, None, :]
    logits = jnp.einsum('nbqhd,nbkhd->nbhqk', qb, kw).astype(jnp.float32) * (dh ** -0.5)
    logits = logits - slope_step.astype(jnp.float32)[:, None, None] * jnp.abs(rel).astype(jnp.float32)
    logits = jnp.where(valid[:, None], logits, -jnp.inf)
    m = jnp.max(logits, -1, keepdims=True)
    p = jnp.exp(logits - m)
    den = jnp.sum(p, -1, keepdims=True)
    o = jnp.einsum('nbhqk,nbkhd->nbqhd', (p / den).astype(v.dtype), vw)
    lse = (m + jnp.log(den))[..., 0]
    o = o.reshape(n, nb * blk, h, dh)[:, :L]
    lse = lse.transpose(0, 1, 3, 2).reshape(n, nb * blk, h)[:, :L]
    return o, lse


def _dilated_group(q, k, v, dilation, radius, slopes):
    b, s, h, dh = q.shape
    L = s // dilation

    def split(t):
        return t.reshape(b, L, dilation, h, dh).transpose(0, 2, 1, 3, 4).reshape(b * dilation, L, h, dh)

    o, lse = _band_attention(split(q), split(k), split(v), radius, slopes * dilation)
    o = o.reshape(b, dilation, L, h, dh).transpose(0, 2, 1, 3, 4).reshape(b, s, h, dh)
    lse = lse.reshape(b, dilation, L, h).transpose(0, 2, 1, 3).reshape(b, s, h)
    return o, lse


def _dilated_mixer(qa, ka, va):
    b, s, _ = qa.shape
    shp = (b, s, N_GROUPS_A, HEADS_PER_GROUP_A, HEAD_DIM_A)
    qa, ka, va = qa.reshape(shp), ka.reshape(shp), va.reshape(shp)
    slopes = (2.0 ** (-8.0 * jnp.arange(1, N_HEADS_A + 1, dtype=jnp.float32) / N_HEADS_A)).reshape(N_GROUPS_A, HEADS_PER_GROUP_A)
    outs, lses = [], []
    for g, (window, dilation) in enumerate(DIL_GROUPS):
        o, lse = _dilated_group(qa[:, :, g], ka[:, :, g], va[:, :, g], dilation, window // (2 * dilation), slopes[g])
        outs.append(o)
        lses.append(lse)
    wts = jax.nn.softmax(jnp.stack(lses, 0), axis=0)
    o = jnp.sum(wts[..., None] * jnp.stack(outs, 0).astype(jnp.float32), 0)
    return o.reshape(b, s, OUT_A).astype(qa.dtype)


def _blocked_attention(q, k, v, scale):
    b, s, h, dq = q.shape
    nb = s // Q_BLOCK
    qb = q.reshape(b, nb, Q_BLOCK, h, dq).transpose(1, 0, 2, 3, 4)

    def block(qi):
        logits = jnp.einsum('bqhd,bkhd->bhqk', qi, k).astype(jnp.float32) * scale
        p = jax.nn.softmax(logits, -1).astype(v.dtype)
        return jnp.einsum('bhqk,bkhd->bqhd', p, v)

    o = lax.map(block, qb)
    return o.transpose(1, 0, 2, 3, 4).reshape(b, s, h, v.shape[-1])


def _mla(cq, ckv, k_rope, q_norm_g, w_q_up, kv_norm_g, w_kv_up, cos, sin):
    b, s, _ = cq.shape
    q = (_rms_norm(cq, q_norm_g) @ w_q_up).reshape(b, s, N_HEADS_B, QK_NOPE + QK_ROPE)
    kv = (_rms_norm(ckv, kv_norm_g) @ w_kv_up).reshape(b, s, N_HEADS_B, QK_NOPE + V_DIM_B)
    q = jnp.concatenate([q[..., :QK_NOPE], _rope(q[..., QK_NOPE:], cos[None, :, None], sin[None, :, None])], -1)
    kr = _rope(k_rope, cos[None], sin[None])
    k = jnp.concatenate([kv[..., :QK_NOPE], jnp.broadcast_to(kr[:, :, None, :], (b, s, N_HEADS_B, QK_ROPE))], -1)
    v = kv[..., QK_NOPE:]
    o = _blocked_attention(q, k, v, (QK_NOPE + QK_ROPE) ** -0.5)
    return o.reshape(b, s, OUT_B)


def _memory_attention(qc, mem, w_mem_kv):
    b, s, _ = qc.shape
    q = qc.reshape(b, s, N_HEADS_C, HEAD_DIM_C)
    kv = (mem @ w_mem_kv).reshape(b, mem.shape[1], 2, N_HEADS_C, HEAD_DIM_C)
    k, v = kv[:, :, 0], kv[:, :, 1]
    logits = jnp.einsum('bshd,bmhd->bhsm', q, k).astype(jnp.float32) * (HEAD_DIM_C ** -0.5)
    p = jax.nn.softmax(logits, -1).astype(v.dtype)
    return jnp.einsum('bhsm,bmhd->bshd', p, v).reshape(b, s, OUT_C)


def _moe(xf, w_router, router_bias, w_gate, w_up, w_down, sh_gate, sh_up, sh_down):
    t = xf.shape[0]
    per_group = N_EXPERTS // N_EXPERT_GROUPS
    scores = jax.nn.sigmoid((xf @ w_router).astype(jnp.float32))
    choice = scores + router_bias.astype(jnp.float32)
    group_score = lax.top_k(choice.reshape(t, N_EXPERT_GROUPS, per_group), 2)[0].sum(-1)
    _, top_groups = lax.top_k(group_score, TOPK_GROUPS)
    group_mask = jax.nn.one_hot(top_groups, N_EXPERT_GROUPS, dtype=jnp.float32).sum(1)
    expert_mask = jnp.repeat(group_mask, per_group, axis=1) > 0
    _, top_e = lax.top_k(jnp.where(expert_mask, choice, -jnp.inf), TOP_K)
    w_sel = jnp.take_along_axis(scores, top_e, axis=-1)
    w_sel = w_sel / jnp.sum(w_sel, -1, keepdims=True) * ROUTED_SCALE
    gates = jnp.sum(jax.nn.one_hot(top_e, N_EXPERTS, dtype=jnp.float32) * w_sel[..., None], 1)

    def expert_step(acc, xs):
        wg, wu, wd, g = xs
        h = jax.nn.silu(xf @ wg) * (xf @ wu) * g[:, None]
        return acc + h @ wd, None

    routed, _ = lax.scan(expert_step, jnp.zeros_like(xf), (w_gate, w_up, w_down, gates.T.astype(xf.dtype)))
    shared = (jax.nn.silu(xf @ sh_gate) * (xf @ sh_up)) @ sh_down
    return routed + shared


def _trunk(x, mem, emb_ln_g, emb_ln_b, w_in, q_norm_g, w_q_up, kv_norm_g, w_kv_up, w_mem_kv,
           w_branch, w_out, ln1_g, ln1_b, w_router, router_bias, w_exp_gate, w_exp_up, w_exp_down,
           w_sh_gate, w_sh_up, w_sh_down, ln2_g, ln2_b):
    b, s, d = x.shape
    x = _layer_norm(x, emb_ln_g, emb_ln_b)
    cos, sin = _rope_tables(s)
    cuts = [int(c) for c in np.cumsum(IN_SIZES)[:-1]]
    for l in range(DEPTH):
        p = x @ w_in[l]
        qa, ka, va, cq, ckv, kr, qc, gl = jnp.split(p, cuts, axis=-1)
        o_a = _dilated_mixer(qa, ka, va)
        o_b = _mla(cq, ckv, kr, q_norm_g[l], w_q_up[l], kv_norm_g[l], w_kv_up[l], cos, sin)
        o_c = _memory_attention(qc, mem, w_mem_kv[l])
        branches = jnp.stack([o_a, o_b, o_c], axis=2)
        proj = jnp.einsum('bsgc,gcd->bsgd', branches, w_branch[l])
        gates = jax.nn.sigmoid(gl.reshape(b, s, N_BRANCH, d).astype(jnp.float32)).astype(x.dtype)
        y = jnp.sum(gates * proj, axis=2) @ w_out[l]
        x = _layer_norm(DEEPNORM_ALPHA * x + y, ln1_g[l], ln1_b[l])
        f = _moe(x.reshape(b * s, d), w_router[l], router_bias[l], w_exp_gate[l], w_exp_up[l], w_exp_down[l],
                 w_sh_gate[l], w_sh_up[l], w_sh_down[l]).reshape(b, s, d)
        x = _layer_norm(DEEPNORM_ALPHA * x + f, ln2_g[l], ln2_b[l])
    return x


def setup_inputs(seed: int = 0) -> dict:
    key = jax.random.key(seed)
    ks = jax.random.split(key, 26)
    f32 = jnp.float32

    def nrm(k, shape, scale):
        return jax.random.normal(k, shape, f32) * scale

    L, D, E, F = DEPTH, D_MODEL, N_EXPERTS, D_EXPERT
    return {
        'x_prompt': nrm(ks[0], (BATCH, SEQ, D), 1.0),
        'x_sample': nrm(ks[1], (DEC_BATCH, DEC_SEQ, D), 1.0),
        'mem_prompt': nrm(ks[2], (BATCH, N_MEM, D), 1.0),
        'mem_sample': nrm(ks[3], (DEC_BATCH, N_MEM, D), 1.0),
        'emb_ln_g': 1.0 + nrm(ks[4], (D,), 0.02),
        'emb_ln_b': nrm(ks[5], (D,), 0.02),
        'w_in': nrm(ks[6], (L, D, N_IN), D ** -0.5),
        'q_norm_g': 1.0 + nrm(ks[7], (L, Q_LORA), 0.02),
        'w_q_up': nrm(ks[8], (L, Q_LORA, N_HEADS_B * (QK_NOPE + QK_ROPE)), Q_LORA ** -0.5),
        'kv_norm_g': 1.0 + nrm(ks[9], (L, KV_LORA), 0.02),
        'w_kv_up': nrm(ks[10], (L, KV_LORA, N_HEADS_B * (QK_NOPE + V_DIM_B)), KV_LORA ** -0.5),
        'w_mem_kv': nrm(ks[11], (L, D, 2 * OUT_C), D ** -0.5),
        'w_branch': nrm(ks[12], (L, N_BRANCH, BRANCH_WIDTH, D), BRANCH_WIDTH ** -0.5),
        'w_out': nrm(ks[13], (L, D, D), DEEPNORM_BETA * D ** -0.5),
        'ln1_g': 1.0 + nrm(ks[14], (L, D), 0.02),
        'ln1_b': nrm(ks[15], (L, D), 0.02),
        'w_router': nrm(ks[16], (L, D, E), D ** -0.5),
        'router_bias': nrm(ks[17], (L, E), 0.01),
        'w_exp_gate': nrm(ks[18], (L, E, D, F), D ** -0.5),
        'w_exp_up': nrm(ks[19], (L, E, D, F), D ** -0.5),
        'w_exp_down': nrm(ks[20], (L, E, F, D), DEEPNORM_BETA * F ** -0.5),
        'w_sh_gate': nrm(ks[21], (L, D, F), D ** -0.5),
        'w_sh_up': nrm(ks[22], (L, D, F), D ** -0.5),
        'w_sh_down': nrm(ks[23], (L, F, D), DEEPNORM_BETA * F ** -0.5),
        'ln2_g': 1.0 + nrm(ks[24], (L, D), 0.02),
        'ln2_b': nrm(ks[25], (L, D), 0.02),
    }


def reference(x_prompt, x_sample, mem_prompt, mem_sample, emb_ln_g, emb_ln_b, w_in, q_norm_g, w_q_up,
              kv_norm_g, w_kv_up, w_mem_kv, w_branch, w_out, ln1_g, ln1_b, w_router, router_bias,
              w_exp_gate, w_exp_up, w_exp_down, w_sh_gate, w_sh_up, w_sh_down, ln2_g, ln2_b):
    y_prompt = _trunk(x_prompt, mem_prompt, emb_ln_g, emb_ln_b, w_in, q_norm_g, w_q_up, kv_norm_g, w_kv_up,
                      w_mem_kv, w_branch, w_out, ln1_g, ln1_b, w_router, router_bias, w_exp_gate, w_exp_up,
                      w_exp_down, w_sh_gate, w_sh_up, w_sh_down, ln2_g, ln2_b)
    y_sample = _trunk(x_sample, mem_sample, emb_ln_g, emb_ln_b, w_in, q_norm_g, w_q_up, kv_norm_g, w_kv_up,
                      w_mem_kv, w_branch, w_out, ln1_g, ln1_b, w_router, router_bias, w_exp_gate, w_exp_up,
                      w_exp_down, w_sh_gate, w_sh_up, w_sh_down, ln2_g, ln2_b)
    return (y_prompt, y_sample)
```

```python
import functools
import math

import numpy as np
import jax
import jax.numpy as jnp
from jax import lax
from jax.experimental import pallas as pl
from jax.experimental.pallas import tpu as pltpu

F32 = jnp.float32
BF16 = jnp.bfloat16

D_MODEL = 1024
DEPTH = 4
DIL_GROUPS = ((128, 1), (512, 4), (2048, 16))
N_GROUPS_A = 3
HEADS_A = 4
HEAD_DIM_A = 128
GROUP_WIDTH_A = HEADS_A * HEAD_DIM_A
WIDTH_A = N_GROUPS_A * GROUP_WIDTH_A
RADIUS_A = 64
HEADS_B = 8
Q_LORA = 256
KV_LORA = 128
QK_NOPE = 64
QK_ROPE = 32
V_DIM_B = 64
ROPE_THETA = 10000.0
HEAD_PAD_B = 128
HEADS_C = 4
HEAD_DIM_C = 128
OUT_C = HEADS_C * HEAD_DIM_C
N_BRANCH = 3
BRANCH_WIDTH = 512
N_EXPERTS = 64
TOP_K = 8
N_EXPERT_GROUPS = 8
EXPERTS_PER_GROUP = N_EXPERTS // N_EXPERT_GROUPS
TOPK_GROUPS = 4
D_EXPERT = 256
ROUTED_SCALE = 2.5
DEEPNORM_ALPHA = (2 * DEPTH) ** 0.25
LN_EPS = 1e-5
RMS_EPS = 1e-6

SEG_QKV = 3 * WIDTH_A
SEG_MLA = Q_LORA + KV_LORA + 2 * HEAD_PAD_B
SEG_QC = OUT_C
SEG_GL = N_BRANCH * D_MODEL
N_PROJ = SEG_QKV + SEG_MLA + SEG_QC + SEG_GL

NEG_BIG = -1e30
VMEM_LIMIT = 56 * 2 ** 20

NT_DIMS = (((1,), (1,)), ((), ()))


def _params(*sem):
    return pltpu.CompilerParams(dimension_semantics=sem, vmem_limit_bytes=VMEM_LIMIT)


def _resident(block_shape, index_map):
    return pl.BlockSpec(block_shape, index_map, pipeline_mode=pl.Buffered(1))


def _layer_norm_rows(h, g, b):
    mu = jnp.mean(h, axis=-1, keepdims=True)
    c = h - mu
    var = jnp.mean(c * c, axis=-1, keepdims=True)
    return c * lax.rsqrt(var + LN_EPS) * g + b


def _rms_norm_rows(h, g):
    return h * lax.rsqrt(jnp.mean(h * h, axis=-1, keepdims=True) + RMS_EPS) * g


def _embed_ln_kernel(x_ref, g_ref, b_ref, xf_ref, xb_ref):
    y = _layer_norm_rows(x_ref[...], g_ref[...], b_ref[...])
    xf_ref[...] = y
    xb_ref[...] = y.astype(BF16)


def _embed_ln(x, g, b):
    t, d = x.shape
    tm = 512
    row = lambda i: (i, 0)
    fixed = lambda i: (0, 0)
    return pl.pallas_call(
        _embed_ln_kernel,
        grid=(t // tm,),
        in_specs=[pl.BlockSpec((tm, d), row), pl.BlockSpec((1, d), fixed), pl.BlockSpec((1, d), fixed)],
        out_specs=[pl.BlockSpec((tm, d), row), pl.BlockSpec((tm, d), row)],
        out_shape=[jax.ShapeDtypeStruct((t, d), F32), jax.ShapeDtypeStruct((t, d), BF16)],
        compiler_params=_params("parallel"),
        name="embed_ln",
    )(x, g, b)


PROJ_CHUNK = 512


def _proj_kernel(x_ref, w_ref, qkv_ref, mla_ref, qc_ref, gl_ref):
    xb = x_ref[...]
    col = 0
    for ref, width in ((qkv_ref, SEG_QKV), (mla_ref, SEG_MLA), (qc_ref, SEG_QC), (gl_ref, SEG_GL)):
        for c in range(0, width, PROJ_CHUNK):
            w = min(PROJ_CHUNK, width - c)
            ref[:, c:c + w] = jnp.dot(xb, w_ref[:, col + c:col + c + w],
                                      preferred_element_type=F32).astype(BF16)
        col += width


def _project(xb, w):
    t, d = xb.shape
    tm = 512
    row = lambda i: (i, 0)
    widths = (SEG_QKV, SEG_MLA, SEG_QC, SEG_GL)
    return pl.pallas_call(
        _proj_kernel,
        grid=(t // tm,),
        in_specs=[pl.BlockSpec((tm, d), row), _resident((d, N_PROJ), lambda i: (0, 0))],
        out_specs=[pl.BlockSpec((tm, n), row) for n in widths],
        out_shape=[jax.ShapeDtypeStruct((t, n), BF16) for n in widths],
        compiler_params=_params("parallel"),
        name="in_proj",
    )(xb, w)


BAND_TQ = 512
BAND_QB = 128
BAND_KB = BAND_QB + 2 * RADIUS_A


def _band_kernel(q_ref, kp_ref, km_ref, kn_ref, vp_ref, vm_ref, vn_ref, o_ref, lse_ref, k_scr, v_scr,
                 *, seq_len, slopes):
    i = pl.program_id(1)
    r = RADIUS_A
    k_scr[0:r, :] = kp_ref[...]
    k_scr[r:r + BAND_TQ, :] = km_ref[...]
    k_scr[r + BAND_TQ:, :] = kn_ref[...]
    v_scr[0:r, :] = vp_ref[...]
    v_scr[r:r + BAND_TQ, :] = vm_ref[...]
    v_scr[r + BAND_TQ:, :] = vn_ref[...]

    row = lax.broadcasted_iota(jnp.int32, (BAND_QB, BAND_KB), 0)
    col = lax.broadcasted_iota(jnp.int32, (BAND_QB, BAND_KB), 1)
    rel = col - r - row
    dist = jnp.abs(rel).astype(F32)
    in_band = jnp.abs(rel) <= r
    scale = HEAD_DIM_A ** -0.5

    for qb in range(BAND_TQ // BAND_QB):
        key_pos = i * BAND_TQ + qb * BAND_QB - r + col
        valid = in_band & (key_pos >= 0) & (key_pos < seq_len)
        for h in range(HEADS_A):
            lanes = slice(h * HEAD_DIM_A, (h + 1) * HEAD_DIM_A)
            q = q_ref[qb * BAND_QB:(qb + 1) * BAND_QB, lanes]
            k = k_scr[qb * BAND_QB:qb * BAND_QB + BAND_KB, lanes]
            v = v_scr[qb * BAND_QB:qb * BAND_QB + BAND_KB, lanes]
            s = lax.dot_general(q, k, NT_DIMS, preferred_element_type=F32)
            logits = jnp.where(valid, s * scale - slopes[h] * dist, NEG_BIG)
            m = jnp.max(logits, axis=1, keepdims=True)
            p = jnp.exp(logits - m)
            den = jnp.sum(p, axis=1, keepdims=True)
            o = jnp.dot(p.astype(BF16), v, preferred_element_type=F32) / den
            rows = slice(qb * BAND_QB, (qb + 1) * BAND_QB)
            o_ref[rows, lanes] = o.astype(BF16)
            lse_ref[rows, lanes] = jnp.broadcast_to(m + jnp.log(den), (BAND_QB, HEAD_DIM_A))


def _band_attention(q_src, k_src, v_src, n_seq, seq_len, slopes):
    tq, r = BAND_TQ, RADIUS_A
    assert seq_len % tq == 0 and tq % r == 0
    steps = seq_len // tq
    halo_per_tile = tq // r
    halo_blocks = seq_len // r

    def main_map(cb):
        return lambda n, i: (n * steps + i, cb)

    def prev_map(cb):
        return lambda n, i: (n * halo_blocks + jnp.maximum(i * halo_per_tile - 1, 0), cb)

    def next_map(cb):
        return lambda n, i: (n * halo_blocks + jnp.minimum((i + 1) * halo_per_tile, halo_blocks - 1), cb)

    (qa, qcb), (ka, kcb), (va, vcb) = q_src, k_src, v_src
    w = GROUP_WIDTH_A
    rows = n_seq * seq_len
    return pl.pallas_call(
        functools.partial(_band_kernel, seq_len=seq_len, slopes=slopes),
        grid=(n_seq, steps),
        in_specs=[
            pl.BlockSpec((tq, w), main_map(qcb)),
            pl.BlockSpec((r, w), prev_map(kcb)), pl.BlockSpec((tq, w), main_map(kcb)),
            pl.BlockSpec((r, w), next_map(kcb)),
            pl.BlockSpec((r, w), prev_map(vcb)), pl.BlockSpec((tq, w), main_map(vcb)),
            pl.BlockSpec((r, w), next_map(vcb)),
        ],
        out_specs=[pl.BlockSpec((tq, w), main_map(0)), pl.BlockSpec((tq, w), main_map(0))],
        out_shape=[jax.ShapeDtypeStruct((rows, w), BF16), jax.ShapeDtypeStruct((rows, w), F32)],
        scratch_shapes=[pltpu.VMEM((tq + 2 * r, w), BF16), pltpu.VMEM((tq + 2 * r, w), BF16)],
        compiler_params=_params("parallel", "parallel"),
        name="band_attention",
    )(qa, ka, ka, ka, va, va, va)


def _alibi_slopes():
    n = N_GROUPS_A * HEADS_A
    return [2.0 ** (-8.0 * (i + 1) / n) for i in range(n)]


def _dilated_mixer(qkv, batch, seq):
    slopes = _alibi_slopes()
    t = batch * seq
    outs = []
    for g, (_, d) in enumerate(DIL_GROUPS):
        group_slopes = tuple(float(s * d) for s in slopes[g * HEADS_A:(g + 1) * HEADS_A])
        if d == 1:
            o, lse = _band_attention((qkv, g), (qkv, N_GROUPS_A + g), (qkv, 2 * N_GROUPS_A + g),
                                     batch, seq, group_slopes)
        else:
            length = seq // d

            def split(m):
                c0 = (m * N_GROUPS_A + g) * GROUP_WIDTH_A
                x = qkv[:, c0:c0 + GROUP_WIDTH_A].reshape(batch, length, d, GROUP_WIDTH_A)
                return x.transpose(0, 2, 1, 3).reshape(t, GROUP_WIDTH_A)

            o, lse = _band_attention((split(0), 0), (split(1), 0), (split(2), 0), batch * d, length, group_slopes)

            def merge(x):
                return x.reshape(batch, d, length, GROUP_WIDTH_A).transpose(0, 2, 1, 3).reshape(t, GROUP_WIDTH_A)

            o, lse = merge(o), merge(lse)
        outs.append((o, lse))
    return outs


def _mla_prep_kernel(mla_ref, cos_ref, sin_ref, gq_ref, wqa_ref, wqb_ref, gkv_ref, wk_ref, wv_ref,
                     q_ref, k_ref, v_ref):
    m = mla_ref[...]
    cq = m[:, 0:Q_LORA].astype(F32)
    ckv = m[:, Q_LORA:Q_LORA + KV_LORA].astype(F32)
    kr = m[:, Q_LORA + KV_LORA:Q_LORA + KV_LORA + HEAD_PAD_B].astype(F32)
    kr_rot = m[:, Q_LORA + KV_LORA + HEAD_PAD_B:].astype(F32)
    cos = cos_ref[...]
    sin = sin_ref[...]
    scale = (QK_NOPE + QK_ROPE) ** -0.5

    cqn = _rms_norm_rows(cq, gq_ref[...]).astype(BF16)
    qa = jnp.dot(cqn, wqa_ref[...], preferred_element_type=F32)
    qb = jnp.dot(cqn, wqb_ref[...], preferred_element_type=F32)
    ckvn = _rms_norm_rows(ckv, gkv_ref[...]).astype(BF16)
    kn = jnp.dot(ckvn, wk_ref[...], preferred_element_type=F32)
    vv = jnp.dot(ckvn, wv_ref[...], preferred_element_type=F32)
    k_rope = kr * cos + kr_rot * sin
    lane = lax.broadcasted_iota(jnp.int32, (1, HEAD_PAD_B), 1)
    ones_lane = (lane == V_DIM_B).astype(F32)
    for h in range(HEADS_B):
        lanes = slice(h * HEAD_PAD_B, (h + 1) * HEAD_PAD_B)
        q_ref[:, lanes] = ((qa[:, lanes] * cos + qb[:, lanes] * sin) * scale).astype(BF16)
        k_ref[:, lanes] = (kn[:, lanes] + k_rope).astype(BF16)
        v_ref[:, lanes] = (vv[:, lanes] + ones_lane).astype(BF16)


def _mla_prep(mla, cos, sin, gq, wqa, wqb, gkv, wk, wv, seq):
    t = mla.shape[0]
    tm = 512
    steps_per_seq = seq // tm
    row = lambda i: (i, 0)
    pos = lambda i: (i % steps_per_seq, 0)
    fixed = lambda i: (0, 0)
    wide = HEADS_B * HEAD_PAD_B
    return pl.pallas_call(
        _mla_prep_kernel,
        grid=(t // tm,),
        in_specs=[pl.BlockSpec((tm, SEG_MLA), row),
                  pl.BlockSpec((tm, HEAD_PAD_B), pos), pl.BlockSpec((tm, HEAD_PAD_B), pos),
                  pl.BlockSpec((1, Q_LORA), fixed), pl.BlockSpec((Q_LORA, wide), fixed),
                  pl.BlockSpec((Q_LORA, wide), fixed),
                  pl.BlockSpec((1, KV_LORA), fixed), pl.BlockSpec((KV_LORA, wide), fixed),
                  pl.BlockSpec((KV_LORA, wide), fixed)],
        out_specs=[pl.BlockSpec((tm, wide), row)] * 3,
        out_shape=[jax.ShapeDtypeStruct((t, wide), BF16)] * 3,
        compiler_params=_params("parallel"),
        name="mla_prep",
    )(mla, cos, sin, gq, wqa, wqb, gkv, wk, wv)


MLA_TQ = 1024
MLA_TK = 1024
MLA_HEADS_PER_STEP = 2


def _mla_attn_kernel(q_ref, k_ref, v_ref, o_ref, *, seq):
    outs = []
    for hh in range(MLA_HEADS_PER_STEP):
        lanes = slice(hh * HEAD_PAD_B, (hh + 1) * HEAD_PAD_B)
        q = q_ref[:, lanes]

        def step(kc, carry, lanes=lanes, q=q):
            m, acc = carry
            rows = pl.ds(pl.multiple_of(kc * MLA_TK, MLA_TK), MLA_TK)
            s = lax.dot_general(q, k_ref[rows, lanes], NT_DIMS, preferred_element_type=F32)
            m_new = jnp.maximum(m, jnp.max(s, axis=1, keepdims=True))
            p = jnp.exp(s - m_new)
            acc = jnp.exp(m - m_new) * acc + jnp.dot(p.astype(BF16), v_ref[rows, lanes],
                                                     preferred_element_type=F32)
            return m_new, acc

        m0 = jnp.full((MLA_TQ, 1), NEG_BIG, F32)
        acc0 = jnp.zeros((MLA_TQ, HEAD_PAD_B), F32)
        _, acc = lax.fori_loop(0, seq // MLA_TK, step, (m0, acc0))
        outs.append(acc[:, 0:V_DIM_B] / acc[:, V_DIM_B:V_DIM_B + 1])
    o_ref[...] = jnp.concatenate(outs, axis=1).astype(BF16)


def _mla_attention(q, k, v, batch, seq):
    t = batch * seq
    tq = MLA_TQ
    steps = seq // tq
    pair = MLA_HEADS_PER_STEP * HEAD_PAD_B
    return pl.pallas_call(
        functools.partial(_mla_attn_kernel, seq=seq),
        grid=(batch, HEADS_B // MLA_HEADS_PER_STEP, steps),
        in_specs=[pl.BlockSpec((tq, pair), lambda b, hp, i: (b * steps + i, hp)),
                  _resident((seq, pair), lambda b, hp, i: (b, hp)),
                  _resident((seq, pair), lambda b, hp, i: (b, hp))],
        out_specs=pl.BlockSpec((tq, MLA_HEADS_PER_STEP * V_DIM_B), lambda b, hp, i: (b * steps + i, hp)),
        out_shape=jax.ShapeDtypeStruct((t, HEADS_B * V_DIM_B), BF16),
        compiler_params=_params("parallel", "parallel", "arbitrary"),
        name="mla_attention",
    )(q, k, v)


def _rope_tables(seq):
    inv_freq = 1.0 / (ROPE_THETA ** (jnp.arange(0, QK_ROPE, 2, dtype=F32) / QK_ROPE))
    ang = jnp.arange(seq, dtype=F32)[:, None] * inv_freq[None, :]
    cos, sin = jnp.cos(ang), jnp.sin(ang)
    pad = HEAD_PAD_B - QK_NOPE - QK_ROPE
    cos_t = jnp.concatenate([jnp.ones((seq, QK_NOPE), F32), cos, cos, jnp.zeros((seq, pad), F32)], axis=1)
    sin_t = jnp.concatenate([jnp.zeros((seq, QK_NOPE), F32), sin, sin, jnp.zeros((seq, pad), F32)], axis=1)
    return cos_t, sin_t


def _mem_kv_kernel(mem_ref, w_ref, o_ref):
    o_ref[...] = jnp.dot(mem_ref[...], w_ref[...], preferred_element_type=F32).astype(BF16)


def _mem_kv(mem_b, w):
    rows, d = mem_b.shape
    n = w.shape[1]
    tm = 256
    return pl.pallas_call(
        _mem_kv_kernel,
        grid=(rows // tm,),
        in_specs=[pl.BlockSpec((tm, d), lambda i: (i, 0)), pl.BlockSpec((d, n), lambda i: (0, 0))],
        out_specs=pl.BlockSpec((tm, n), lambda i: (i, 0)),
        out_shape=jax.ShapeDtypeStruct((rows, n), BF16),
        compiler_params=_params("parallel"),
        name="mem_kv",
    )(mem_b, w)


def _mem_attn_kernel(q_ref, kv_ref, o_ref):
    scale = HEAD_DIM_C ** -0.5
    for h in range(HEADS_C):
        lanes = slice(h * HEAD_DIM_C, (h + 1) * HEAD_DIM_C)
        k = kv_ref[:, lanes]
        v = kv_ref[:, OUT_C + h * HEAD_DIM_C:OUT_C + (h + 1) * HEAD_DIM_C]
        s = lax.dot_general(q_ref[:, lanes], k, NT_DIMS, preferred_element_type=F32) * scale
        m = jnp.max(s, axis=1, keepdims=True)
        p = jnp.exp(s - m)
        den = jnp.sum(p, axis=1, keepdims=True)
        o_ref[:, lanes] = (jnp.dot(p.astype(BF16), v, preferred_element_type=F32) / den).astype(BF16)


def _mem_attention(qc, kv, batch, seq, n_mem):
    t = batch * seq
    ts = 1024
    steps = seq // ts
    return pl.pallas_call(
        _mem_attn_kernel,
        grid=(batch, steps),
        in_specs=[pl.BlockSpec((ts, OUT_C), lambda b, i: (b * steps + i, 0)),
                  pl.BlockSpec((n_mem, 2 * OUT_C), lambda b, i: (b, 0))],
        out_specs=pl.BlockSpec((ts, OUT_C), lambda b, i: (b * steps + i, 0)),
        out_shape=jax.ShapeDtypeStruct((t, OUT_C), BF16),
        compiler_params=_params("parallel", "parallel"),
        name="mem_attention",
    )(qc, kv)


def _merge_kernel(oa0_ref, oa1_ref, oa2_ref, l0_ref, l1_ref, l2_ref, ob_ref, oc_ref, gl_ref, x_ref,
                  wb_ref, wo_ref, g_ref, b_ref, xf_ref, xb_ref):
    l0, l1, l2 = l0_ref[...], l1_ref[...], l2_ref[...]
    m = jnp.maximum(jnp.maximum(l0, l1), l2)
    e0, e1, e2 = jnp.exp(l0 - m), jnp.exp(l1 - m), jnp.exp(l2 - m)
    oa = (e0 * oa0_ref[...].astype(F32) + e1 * oa1_ref[...].astype(F32) + e2 * oa2_ref[...].astype(F32))
    oa = (oa / (e0 + e1 + e2)).astype(BF16)
    z = None
    for i, o in enumerate((oa, ob_ref[...], oc_ref[...])):
        gate = jax.nn.sigmoid(gl_ref[:, i * D_MODEL:(i + 1) * D_MODEL].astype(F32))
        term = gate * jnp.dot(o, wb_ref[i], preferred_element_type=F32)
        z = term if z is None else z + term
    y = jnp.dot(z.astype(BF16), wo_ref[...], preferred_element_type=F32)
    out = _layer_norm_rows(DEEPNORM_ALPHA * x_ref[...] + y, g_ref[...], b_ref[...])
    xf_ref[...] = out
    xb_ref[...] = out.astype(BF16)


def _merge(oa, ob, oc, gl, x, wb, wo, g, b):
    t, d = x.shape
    tm = 512
    row = lambda i: (i, 0)
    fixed = lambda i: (0, 0)
    half = pl.BlockSpec((tm, BRANCH_WIDTH), row)
    (oa0, l0), (oa1, l1), (oa2, l2) = oa
    return pl.pallas_call(
        _merge_kernel,
        grid=(t // tm,),
        in_specs=[half] * 8 + [pl.BlockSpec((tm, SEG_GL), row), pl.BlockSpec((tm, d), row),
                               _resident((N_BRANCH, BRANCH_WIDTH, d), lambda i: (0, 0, 0)),
                               _resident((d, d), fixed),
                               pl.BlockSpec((1, d), fixed), pl.BlockSpec((1, d), fixed)],
        out_specs=[pl.BlockSpec((tm, d), row), pl.BlockSpec((tm, d), row)],
        out_shape=[jax.ShapeDtypeStruct((t, d), F32), jax.ShapeDtypeStruct((t, d), BF16)],
        compiler_params=_params("parallel"),
        name="merge_ln1",
    )(oa0, oa1, oa2, l0, l1, l2, ob, oc, gl, x, wb, wo, g, b)


GATE_LANES = 128


def _first_index_of_max(vals, idx, axis, sentinel):
    mx = jnp.max(vals, axis=axis, keepdims=True)
    return jnp.min(jnp.where(vals == mx, idx, sentinel), axis=axis, keepdims=True)


def _router_kernel(x_ref, w_ref, bias_ref, g_ref):
    tm = x_ref.shape[0]
    logits = lax.dot_general(w_ref[...], x_ref[...], NT_DIMS, preferred_element_type=F32)
    scores = jax.nn.sigmoid(logits)
    choice = scores + bias_ref[...]
    neg = -jnp.inf

    c3 = choice.reshape(N_EXPERT_GROUPS, EXPERTS_PER_GROUP, tm)
    e_idx = lax.broadcasted_iota(jnp.int32, c3.shape, 1)
    first = jnp.max(c3, axis=1, keepdims=True)
    first_at = jnp.min(jnp.where(c3 == first, e_idx, EXPERTS_PER_GROUP), axis=1, keepdims=True)
    second = jnp.max(jnp.where(e_idx == first_at, neg, c3), axis=1, keepdims=True)
    group_score = (first + second).reshape(N_EXPERT_GROUPS, tm)

    g_idx = lax.broadcasted_iota(jnp.int32, group_score.shape, 0)
    group_sel = jnp.zeros(group_score.shape, jnp.bool_)
    for _ in range(TOPK_GROUPS):
        at = _first_index_of_max(group_score, g_idx, 0, N_EXPERT_GROUPS)
        hit = g_idx == at
        group_sel = group_sel | hit
        group_score = jnp.where(hit, neg, group_score)

    allowed = jnp.broadcast_to(group_sel.reshape(N_EXPERT_GROUPS, 1, tm), c3.shape).reshape(N_EXPERTS, tm)
    cand = jnp.where(allowed, choice, neg)
    x_idx = lax.broadcasted_iota(jnp.int32, cand.shape, 0)
    chosen = jnp.zeros(cand.shape, jnp.bool_)
    for _ in range(TOP_K):
        at = _first_index_of_max(cand, x_idx, 0, N_EXPERTS)
        hit = x_idx == at
        chosen = chosen | hit
        cand = jnp.where(hit, neg, cand)

    w_sel = jnp.where(chosen, scores, 0.0)
    gates = w_sel / jnp.sum(w_sel, axis=0, keepdims=True) * ROUTED_SCALE
    padded = jnp.concatenate([gates, jnp.zeros((GATE_LANES - N_EXPERTS, tm), F32)], axis=0)
    g_ref[...] = padded.T


def _router(xb, w_t, bias):
    t, d = xb.shape
    tm = 1024
    return pl.pallas_call(
        _router_kernel,
        grid=(t // tm,),
        in_specs=[pl.BlockSpec((tm, d), lambda i: (i, 0)), pl.BlockSpec((N_EXPERTS, d), lambda i: (0, 0)),
                  pl.BlockSpec((N_EXPERTS, 1), lambda i: (0, 0))],
        out_specs=pl.BlockSpec((tm, GATE_LANES), lambda i: (i, 0)),
        out_shape=jax.ShapeDtypeStruct((t, GATE_LANES), F32),
        compiler_params=_params("parallel"),
        name="router",
    )(xb, w_t, bias)


EXPERTS_PER_STEP = 4


def _swiglu(xb, wg, wu):
    return jax.nn.silu(jnp.dot(xb, wg, preferred_element_type=F32)) * jnp.dot(xb, wu, preferred_element_type=F32)


def _moe_kernel(xb_ref, xf_ref, gates_ref, wg_ref, wu_ref, wd_ref, sg_ref, su_ref, sd_ref, g_ref, b_ref,
                of_ref, ob_ref, acc_ref):
    j = pl.program_id(1)
    xb = xb_ref[...]

    @pl.when(j == 0)
    def _():
        h = _swiglu(xb, sg_ref[...], su_ref[...])
        acc_ref[...] = jnp.dot(h.astype(BF16), sd_ref[...], preferred_element_type=F32)

    gates = gates_ref[...]
    lane = lax.broadcasted_iota(jnp.int32, gates.shape, 1)
    for e in range(EXPERTS_PER_STEP):
        gate = jnp.sum(jnp.where(lane == j * EXPERTS_PER_STEP + e, gates, 0.0), axis=1, keepdims=True)
        h = _swiglu(xb, wg_ref[e], wu_ref[e]) * gate
        acc_ref[...] += jnp.dot(h.astype(BF16), wd_ref[e], preferred_element_type=F32)

    @pl.when(j == pl.num_programs(1) - 1)
    def _():
        out = _layer_norm_rows(DEEPNORM_ALPHA * xf_ref[...] + acc_ref[...], g_ref[...], b_ref[...])
        of_ref[...] = out
        ob_ref[...] = out.astype(BF16)


def _moe(xb, xf, gates, wg, wu, wd, sg, su, sd, g, b):
    t, d = xf.shape
    tm = 1024
    f = D_EXPERT
    es = EXPERTS_PER_STEP
    row = lambda i, j: (i, 0)
    fixed = lambda i, j: (0, 0)
    return pl.pallas_call(
        _moe_kernel,
        grid=(t // tm, N_EXPERTS // es),
        in_specs=[pl.BlockSpec((tm, d), row), pl.BlockSpec((tm, d), row), pl.BlockSpec((tm, GATE_LANES), row),
                  pl.BlockSpec((es, d, f), lambda i, j: (j, 0, 0)), pl.BlockSpec((es, d, f), lambda i, j: (j, 0, 0)),
                  pl.BlockSpec((es, f, d), lambda i, j: (j, 0, 0)),
                  pl.BlockSpec((d, f), fixed), pl.BlockSpec((d, f), fixed), pl.BlockSpec((f, d), fixed),
                  pl.BlockSpec((1, d), fixed), pl.BlockSpec((1, d), fixed)],
        out_specs=[pl.BlockSpec((tm, d), row), pl.BlockSpec((tm, d), row)],
        out_shape=[jax.ShapeDtypeStruct((t, d), F32), jax.ShapeDtypeStruct((t, d), BF16)],
        scratch_shapes=[pltpu.VMEM((tm, d), F32)],
        compiler_params=_params("parallel", "arbitrary"),
        name="moe_ln2",
    )(xb, xf, gates, wg, wu, wd, sg, su, sd, g, b)


def _rotate_half_columns(w):
    half = QK_ROPE // 2
    return jnp.concatenate([-w[..., half:], w[..., :half]], axis=-1)


def _prep_in_proj(w_in):
    layers, d, _ = w_in.shape
    cuts = np.cumsum((WIDTH_A, WIDTH_A, WIDTH_A, Q_LORA, KV_LORA, QK_ROPE, OUT_C))
    qa, ka, va, cq, ckv, kr, qc, gl = jnp.split(w_in, [int(c) for c in cuts], axis=-1)
    lead = jnp.zeros((layers, d, QK_NOPE), w_in.dtype)
    tail = jnp.zeros((layers, d, HEAD_PAD_B - QK_NOPE - QK_ROPE), w_in.dtype)
    kr_slot = jnp.concatenate([lead, kr, tail], axis=-1)
    kr_rot_slot = jnp.concatenate([lead, _rotate_half_columns(kr), tail], axis=-1)
    return jnp.concatenate([qa, ka, va, cq, ckv, kr_slot, kr_rot_slot, qc, gl], axis=-1).astype(BF16)


def _prep_mla_weights(w_q_up, w_kv_up):
    layers = w_q_up.shape[0]
    wq = w_q_up.reshape(layers, Q_LORA, HEADS_B, QK_NOPE + QK_ROPE)
    nope, rope = wq[..., :QK_NOPE], wq[..., QK_NOPE:]
    pad = HEAD_PAD_B - QK_NOPE - QK_ROPE
    zq = lambda n: jnp.zeros((layers, Q_LORA, HEADS_B, n), w_q_up.dtype)
    wqa = jnp.concatenate([nope, rope, zq(pad)], axis=-1)
    wqb = jnp.concatenate([zq(QK_NOPE), _rotate_half_columns(rope), zq(pad)], axis=-1)
    wkv = w_kv_up.reshape(layers, KV_LORA, HEADS_B, QK_NOPE + V_DIM_B)
    zk = lambda n: jnp.zeros((layers, KV_LORA, HEADS_B, n), w_kv_up.dtype)
    wk = jnp.concatenate([wkv[..., :QK_NOPE], zk(HEAD_PAD_B - QK_NOPE)], axis=-1)
    wv = jnp.concatenate([wkv[..., QK_NOPE:], zk(HEAD_PAD_B - V_DIM_B)], axis=-1)
    flat = lambda w: w.reshape(layers, w.shape[1], HEADS_B * HEAD_PAD_B).astype(BF16)
    return flat(wqa), flat(wqb), flat(wk), flat(wv)


def _trunk(x, mem, emb_g, emb_b, w, depth):
    batch, seq, d = x.shape
    n_mem = mem.shape[1]
    t = batch * seq
    row2 = lambda v: v.reshape(1, -1)
    xf, xb = _embed_ln(x.reshape(t, d), row2(emb_g), row2(emb_b))
    mem_b = mem.reshape(batch * n_mem, d).astype(BF16)
    cos_t, sin_t = _rope_tables(seq)
    for l in range(depth):
        qkv, mla, qc, gl = _project(xb, w["in_proj"][l])
        oa = _dilated_mixer(qkv, batch, seq)
        q, k, v = _mla_prep(mla, cos_t, sin_t, row2(w["q_norm_g"][l]), w["wqa"][l], w["wqb"][l],
                            row2(w["kv_norm_g"][l]), w["wk"][l], w["wv"][l], seq)
        ob = _mla_attention(q, k, v, batch, seq)
        oc = _mem_attention(qc, _mem_kv(mem_b, w["mem_kv"][l]), batch, seq, n_mem)
        xf, xb = _merge(oa, ob, oc, gl, xf, w["branch"][l], w["out"][l], row2(w["ln1_g"][l]), row2(w["ln1_b"][l]))
        gates = _router(xb, w["router_t"][l], w["router_bias"][l].reshape(N_EXPERTS, 1))
        xf, xb = _moe(xb, xf, gates, w["exp_gate"][l], w["exp_up"][l], w["exp_down"][l],
                      w["sh_gate"][l], w["sh_up"][l], w["sh_down"][l], row2(w["ln2_g"][l]), row2(w["ln2_b"][l]))
    return xf.reshape(batch, seq, d)


def kernel(x_prompt, x_sample, mem_prompt, mem_sample, emb_ln_g, emb_ln_b, w_in, q_norm_g, w_q_up, kv_norm_g,
           w_kv_up, w_mem_kv, w_branch, w_out, ln1_g, ln1_b, w_router, router_bias, w_exp_gate, w_exp_up,
           w_exp_down, w_sh_gate, w_sh_up, w_sh_down, ln2_g, ln2_b):
    wqa, wqb, wk, wv = _prep_mla_weights(w_q_up, w_kv_up)
    w = {
        "in_proj": _prep_in_proj(w_in),
        "q_norm_g": q_norm_g, "kv_norm_g": kv_norm_g, "wqa": wqa, "wqb": wqb, "wk": wk, "wv": wv,
        "mem_kv": w_mem_kv.astype(BF16), "branch": w_branch.astype(BF16), "out": w_out.astype(BF16),
        "ln1_g": ln1_g, "ln1_b": ln1_b,
        "router_t": jnp.swapaxes(w_router, 1, 2).astype(BF16), "router_bias": router_bias,
        "exp_gate": w_exp_gate.astype(BF16), "exp_up": w_exp_up.astype(BF16), "exp_down": w_exp_down.astype(BF16),
        "sh_gate": w_sh_gate.astype(BF16), "sh_up": w_sh_up.astype(BF16), "sh_down": w_sh_down.astype(BF16),
        "ln2_g": ln2_g, "ln2_b": ln2_b,
    }
    depth = w_in.shape[0]
    y_prompt = _trunk(x_prompt, mem_prompt, emb_ln_g, emb_ln_b, w, depth)
    y_sample = _trunk(x_sample, mem_sample, emb_ln_g, emb_ln_b, w, depth)
    return (y_prompt, y_sample)
```

```python
import functools
import math

import numpy as np
import jax
import jax.numpy as jnp
from jax import lax
from jax.experimental import pallas as pl
from jax.experimental.pallas import tpu as pltpu

F32 = jnp.float32
BF16 = jnp.bfloat16

D_MODEL = 1024
DEPTH = 4
DIL_GROUPS = ((128, 1), (512, 4), (2048, 16))
N_GROUPS_A = 3
HEADS_A = 4
HEAD_DIM_A = 128
GROUP_WIDTH_A = HEADS_A * HEAD_DIM_A
WIDTH_A = N_GROUPS_A * GROUP_WIDTH_A
RADIUS_A = 64
HEADS_B = 8
Q_LORA = 256
KV_LORA = 128
QK_NOPE = 64
QK_ROPE = 32
V_DIM_B = 64
ROPE_THETA = 10000.0
HEAD_PAD_B = 128
HEADS_C = 4
HEAD_DIM_C = 128
OUT_C = HEADS_C * HEAD_DIM_C
N_BRANCH = 3
BRANCH_WIDTH = 512
N_EXPERTS = 64
TOP_K = 8
N_EXPERT_GROUPS = 8
EXPERTS_PER_GROUP = N_EXPERTS // N_EXPERT_GROUPS
TOPK_GROUPS = 4
D_EXPERT = 256
ROUTED_SCALE = 2.5
DEEPNORM_ALPHA = (2 * DEPTH) ** 0.25
LN_EPS = 1e-5
RMS_EPS = 1e-6

SEG_QKV = 3 * WIDTH_A
SEG_MLA = Q_LORA + KV_LORA + 2 * HEAD_PAD_B
SEG_QC = OUT_C
SEG_GL = N_BRANCH * D_MODEL
N_PROJ = SEG_QKV + SEG_MLA + SEG_QC + SEG_GL

NEG_BIG = -1e30
VMEM_LIMIT = 56 * 2 ** 20

NT_DIMS = (((1,), (1,)), ((), ()))


def _params(*sem):
    return pltpu.CompilerParams(dimension_semantics=sem, vmem_limit_bytes=VMEM_LIMIT)


def _resident(block_shape, index_map):
    return pl.BlockSpec(block_shape, index_map, pipeline_mode=pl.Buffered(1))


def _layer_norm_rows(h, g, b):
    mu = jnp.mean(h, axis=-1, keepdims=True)
    c = h - mu
    var = jnp.mean(c * c, axis=-1, keepdims=True)
    return c * lax.rsqrt(var + LN_EPS) * g + b


def _rms_norm_rows(h, g):
    return h * lax.rsqrt(jnp.mean(h * h, axis=-1, keepdims=True) + RMS_EPS) * g


def _embed_ln_kernel(x_ref, g_ref, b_ref, xf_ref, xb_ref):
    y = _layer_norm_rows(x_ref[...], g_ref[...], b_ref[...])
    xf_ref[...] = y
    xb_ref[...] = y.astype(BF16)


def _embed_ln(x, g, b):
    t, d = x.shape
    tm = 512
    row = lambda i: (i, 0)
    fixed = lambda i: (0, 0)
    return pl.pallas_call(
        _embed_ln_kernel,
        grid=(t // tm,),
        in_specs=[pl.BlockSpec((tm, d), row), pl.BlockSpec((1, d), fixed), pl.BlockSpec((1, d), fixed)],
        out_specs=[pl.BlockSpec((tm, d), row), pl.BlockSpec((tm, d), row)],
        out_shape=[jax.ShapeDtypeStruct((t, d), F32), jax.ShapeDtypeStruct((t, d), BF16)],
        compiler_params=_params("parallel"),
        name="embed_ln",
    )(x, g, b)


PROJ_CHUNK = 512


def _proj_kernel(x_ref, w_ref, qkv_ref, mla_ref, qc_ref, gl_ref):
    xb = x_ref[...]
    col = 0
    for ref, width in ((qkv_ref, SEG_QKV), (mla_ref, SEG_MLA), (qc_ref, SEG_QC), (gl_ref, SEG_GL)):
        for c in range(0, width, PROJ_CHUNK):
            w = min(PROJ_CHUNK, width - c)
            ref[:, c:c + w] = jnp.dot(xb, w_ref[:, col + c:col + c + w],
                                      preferred_element_type=F32).astype(BF16)
        col += width


def _project(xb, w):
    t, d = xb.shape
    tm = 512
    row = lambda i: (i, 0)
    widths = (SEG_QKV, SEG_MLA, SEG_QC, SEG_GL)
    return pl.pallas_call(
        _proj_kernel,
        grid=(t // tm,),
        in_specs=[pl.BlockSpec((tm, d), row), _resident((d, N_PROJ), lambda i: (0, 0))],
        out_specs=[pl.BlockSpec((tm, n), row) for n in widths],
        out_shape=[jax.ShapeDtypeStruct((t, n), BF16) for n in widths],
        compiler_params=_params("parallel"),
        name="in_proj",
    )(xb, w)


BAND_TQ = 512
BAND_QB = 128
BAND_KB = BAND_QB + 2 * RADIUS_A


def _band_kernel(q_ref, kp_ref, km_ref, kn_ref, vp_ref, vm_ref, vn_ref, o_ref, lse_ref, k_scr, v_scr,
                 *, seq_len, slopes):
    i = pl.program_id(1)
    r = RADIUS_A
    k_scr[0:r, :] = kp_ref[...]
    k_scr[r:r + BAND_TQ, :] = km_ref[...]
    k_scr[r + BAND_TQ:, :] = kn_ref[...]
    v_scr[0:r, :] = vp_ref[...]
    v_scr[r:r + BAND_TQ, :] = vm_ref[...]
    v_scr[r + BAND_TQ:, :] = vn_ref[...]

    row = lax.broadcasted_iota(jnp.int32, (BAND_QB, BAND_KB), 0)
    col = lax.broadcasted_iota(jnp.int32, (BAND_QB, BAND_KB), 1)
    rel = col - r - row
    dist = jnp.abs(rel).astype(F32)
    in_band = jnp.abs(rel) <= r
    scale = HEAD_DIM_A ** -0.5

    for qb in range(BAND_TQ // BAND_QB):
        key_pos = i * BAND_TQ + qb * BAND_QB - r + col
        valid = in_band & (key_pos >= 0) & (key_pos < seq_len)
        for h in range(HEADS_A):
            lanes = slice(h * HEAD_DIM_A, (h + 1) * HEAD_DIM_A)
            q = q_ref[qb * BAND_QB:(qb + 1) * BAND_QB, lanes]
            k = k_scr[qb * BAND_QB:qb * BAND_QB + BAND_KB, lanes]
            v = v_scr[qb * BAND_QB:qb * BAND_QB + BAND_KB, lanes]
            s = lax.dot_general(q, k, NT_DIMS, preferred_element_type=F32)
            logits = jnp.where(valid, s * scale - slopes[h] * dist, NEG_BIG)
            m = jnp.max(logits, axis=1, keepdims=True)
            p = jnp.exp(logits - m)
            den = jnp.sum(p, axis=1, keepdims=True)
            o = jnp.dot(p.astype(BF16), v, preferred_element_type=F32) / den
            rows = slice(qb * BAND_QB, (qb + 1) * BAND_QB)
            o_ref[rows, lanes] = o.astype(BF16)
            lse_ref[rows, lanes] = jnp.broadcast_to(m + jnp.log(den), (BAND_QB, HEAD_DIM_A))


def _band_attention(q_src, k_src, v_src, n_seq, seq_len, slopes):
    tq, r = BAND_TQ, RADIUS_A
    assert seq_len % tq == 0 and tq % r == 0
    steps = seq_len // tq
    halo_per_tile = tq // r
    halo_blocks = seq_len // r

    def main_map(cb):
        return lambda n, i: (n * steps + i, cb)

    def prev_map(cb):
        return lambda n, i: (n * halo_blocks + jnp.maximum(i * halo_per_tile - 1, 0), cb)

    def next_map(cb):
        return lambda n, i: (n * halo_blocks + jnp.minimum((i + 1) * halo_per_tile, halo_blocks - 1), cb)

    (qa, qcb), (ka, kcb), (va, vcb) = q_src, k_src, v_src
    w = GROUP_WIDTH_A
    rows = n_seq * seq_len
    return pl.pallas_call(
        functools.partial(_band_kernel, seq_len=seq_len, slopes=slopes),
        grid=(n_seq, steps),
        in_specs=[
            pl.BlockSpec((tq, w), main_map(qcb)),
            pl.BlockSpec((r, w), prev_map(kcb)), pl.BlockSpec((tq, w), main_map(kcb)),
            pl.BlockSpec((r, w), next_map(kcb)),
            pl.BlockSpec((r, w), prev_map(vcb)), pl.BlockSpec((tq, w), main_map(vcb)),
            pl.BlockSpec((r, w), next_map(vcb)),
        ],
        out_specs=[pl.BlockSpec((tq, w), main_map(0)), pl.BlockSpec((tq, w), main_map(0))],
        out_shape=[jax.ShapeDtypeStruct((rows, w), BF16), jax.ShapeDtypeStruct((rows, w), F32)],
        scratch_shapes=[pltpu.VMEM((tq + 2 * r, w), BF16), pltpu.VMEM((tq + 2 * r, w), BF16)],
        compiler_params=_params("parallel", "parallel"),
        name="band_attention",
    )(qa, ka, ka, ka, va, va, va)


def _alibi_slopes():
    n = N_GROUPS_A * HEADS_A
    return [2.0 ** (-8.0 * (i + 1) / n) for i in range(n)]


def _dilated_mixer(qkv, batch, seq):
    slopes = _alibi_slopes()
    t = batch * seq
    outs = []
    for g, (_, d) in enumerate(DIL_GROUPS):
        group_slopes = tuple(float(s * d) for s in slopes[g * HEADS_A:(g + 1) * HEADS_A])
        if d == 1:
            o, lse = _band_attention((qkv, g), (qkv, N_GROUPS_A + g), (qkv, 2 * N_GROUPS_A + g),
                                     batch, seq, group_slopes)
        else:
            length = seq // d

            def split(m):
                c0 = (m * N_GROUPS_A + g) * GROUP_WIDTH_A
                x = qkv[:, c0:c0 + GROUP_WIDTH_A].reshape(batch, length, d, GROUP_WIDTH_A)
                return x.transpose(0, 2, 1, 3).reshape(t, GROUP_WIDTH_A)

            o, lse = _band_attention((split(0), 0), (split(1), 0), (split(2), 0), batch * d, length, group_slopes)

            def merge(x):
                return x.reshape(batch, d, length, GROUP_WIDTH_A).transpose(0, 2, 1, 3).reshape(t, GROUP_WIDTH_A)

            o, lse = merge(o), merge(lse)
        outs.append((o, lse))
    return outs


V_ROWS_B = 80
MLA_TQ = 1024
MLA_TK = 2048
MLA_KS = 256
MLA_LOOKAHEAD = 2
MLA_HEADS_PER_STEP = 2


def _mla_prep_kernel(mla_ref, cos_ref, sin_ref, cos_t_ref, sin_t_ref, gq_ref, wqa_ref, wqb_ref, gkv_ref,
                     wk_ref, wv_ref, qt_ref, k_ref, vt_ref):
    m = mla_ref[...]
    cq = m[:, 0:Q_LORA].astype(F32)
    ckv = m[:, Q_LORA:Q_LORA + KV_LORA].astype(F32)
    kr = m[:, Q_LORA + KV_LORA:Q_LORA + KV_LORA + HEAD_PAD_B].astype(F32)
    kr_rot = m[:, Q_LORA + KV_LORA + HEAD_PAD_B:].astype(F32)
    scale = (QK_NOPE + QK_ROPE) ** -0.5 * math.log2(math.e)

    cqn = _rms_norm_rows(cq, gq_ref[...]).astype(BF16)
    qa_t = lax.dot_general(wqa_ref[...], cqn, NT_DIMS, preferred_element_type=F32)
    qb_t = lax.dot_general(wqb_ref[...], cqn, NT_DIMS, preferred_element_type=F32)
    ckvn = _rms_norm_rows(ckv, gkv_ref[...]).astype(BF16)
    kn = jnp.dot(ckvn, wk_ref[...], preferred_element_type=F32)
    v_t = lax.dot_general(wv_ref[...], ckvn, NT_DIMS, preferred_element_type=F32)
    k_rope = kr * cos_ref[...] + kr_rot * sin_ref[...]
    cos_t, sin_t = cos_t_ref[...], sin_t_ref[...]
    row = lax.broadcasted_iota(jnp.int32, (V_ROWS_B, 1), 0)
    ones_row = (row == V_DIM_B).astype(F32)
    for h in range(HEADS_B):
        slot = slice(h * HEAD_PAD_B, (h + 1) * HEAD_PAD_B)
        qt_ref[slot, :] = ((qa_t[slot, :] * cos_t + qb_t[slot, :] * sin_t) * scale).astype(BF16)
        k_ref[:, slot] = (kn[:, slot] + k_rope).astype(BF16)
        vt_ref[0, h, 0] = (v_t[h * V_ROWS_B:(h + 1) * V_ROWS_B, :] + ones_row).astype(BF16)


def _mla_prep(mla, tables, gq, wqa_t, wqb_t, gkv, wk, wv_t, batch, seq):
    t = mla.shape[0]
    tm = MLA_TK
    chunks = seq // tm
    cos, sin, cos_t, sin_t = tables
    row = lambda i: (i, 0)
    pos = lambda i: (i % chunks, 0)
    pos_t = lambda i: (0, i % chunks)
    fixed = lambda i: (0, 0)
    wide = HEADS_B * HEAD_PAD_B
    return pl.pallas_call(
        _mla_prep_kernel,
        grid=(t // tm,),
        in_specs=[pl.BlockSpec((tm, SEG_MLA), row),
                  pl.BlockSpec((tm, HEAD_PAD_B), pos), pl.BlockSpec((tm, HEAD_PAD_B), pos),
                  pl.BlockSpec((HEAD_PAD_B, tm), pos_t), pl.BlockSpec((HEAD_PAD_B, tm), pos_t),
                  pl.BlockSpec((1, Q_LORA), fixed), pl.BlockSpec((wide, Q_LORA), fixed),
                  pl.BlockSpec((wide, Q_LORA), fixed),
                  pl.BlockSpec((1, KV_LORA), fixed), pl.BlockSpec((KV_LORA, wide), fixed),
                  pl.BlockSpec((HEADS_B * V_ROWS_B, KV_LORA), fixed)],
        out_specs=[pl.BlockSpec((wide, tm), lambda i: (i // chunks, i % chunks)),
                   pl.BlockSpec((tm, wide), row),
                   pl.BlockSpec((1, HEADS_B, 1, V_ROWS_B, tm), lambda i: (i // chunks, 0, i % chunks, 0, 0))],
        out_shape=[jax.ShapeDtypeStruct((batch * wide, seq), BF16),
                   jax.ShapeDtypeStruct((t, wide), BF16),
                   jax.ShapeDtypeStruct((batch, HEADS_B, chunks, V_ROWS_B, tm), BF16)],
        compiler_params=_params("parallel"),
        name="mla_prep",
    )(mla, cos, sin, cos_t, sin_t, gq, wqa_t, wqb_t, gkv, wk, wv_t)


def _mla_attn_kernel(qt_ref, k_ref, vt_ref, o_ref, *, seq):
    n_sub = MLA_TK // MLA_KS
    units = [(c, hh) for c in range(n_sub) for hh in range(MLA_HEADS_PER_STEP)]

    def step(kc, carry):
        new = list(carry)

        def scores(u):
            c, hh = units[u]
            slot = slice(hh * HEAD_PAD_B, (hh + 1) * HEAD_PAD_B)
            rows = pl.ds(pl.multiple_of(kc * MLA_TK + c * MLA_KS, MLA_KS), MLA_KS)
            return jnp.dot(k_ref[rows, slot], qt_ref[slot, :], preferred_element_type=F32)

        pending = [scores(u) for u in range(min(MLA_LOOKAHEAD, len(units)))]
        for u, (c, hh) in enumerate(units):
            if u + MLA_LOOKAHEAD < len(units):
                pending.append(scores(u + MLA_LOOKAHEAD))
            s = pending.pop(0)
            m, acc = new[hh]
            m_new = jnp.maximum(m, jnp.max(s, axis=0, keepdims=True))
            p = jnp.exp2(s - m_new).astype(BF16)
            v_blk = vt_ref[0, hh, kc, :, c * MLA_KS:(c + 1) * MLA_KS]
            acc = jnp.exp2(m - m_new) * acc + jnp.dot(v_blk, p, preferred_element_type=F32)
            new[hh] = (m_new, acc)
        return tuple(new)

    init = tuple((jnp.full((1, MLA_TQ), NEG_BIG, F32), jnp.zeros((V_ROWS_B, MLA_TQ), F32))
                 for _ in range(MLA_HEADS_PER_STEP))
    final = lax.fori_loop(0, seq // MLA_TK, step, init)
    heads = [acc[0:V_DIM_B, :] / acc[V_DIM_B:V_DIM_B + 1, :] for _, acc in final]
    o_ref[...] = jnp.concatenate(heads, axis=0).T.astype(BF16)


def _mla_attention(q_t, k, v_t, batch, seq):
    t = batch * seq
    tq = MLA_TQ
    steps = seq // tq
    chunks = seq // MLA_TK
    pairs = HEADS_B // MLA_HEADS_PER_STEP
    pair = MLA_HEADS_PER_STEP * HEAD_PAD_B
    return pl.pallas_call(
        functools.partial(_mla_attn_kernel, seq=seq),
        grid=(batch, pairs, steps),
        in_specs=[pl.BlockSpec((pair, tq), lambda b, hp, i: (b * pairs + hp, i)),
                  _resident((seq, pair), lambda b, hp, i: (b, hp)),
                  _resident((1, MLA_HEADS_PER_STEP, chunks, V_ROWS_B, MLA_TK), lambda b, hp, i: (b, hp, 0, 0, 0))],
        out_specs=pl.BlockSpec((tq, MLA_HEADS_PER_STEP * V_DIM_B), lambda b, hp, i: (b * steps + i, hp)),
        out_shape=jax.ShapeDtypeStruct((t, HEADS_B * V_DIM_B), BF16),
        compiler_params=_params("parallel", "parallel", "arbitrary"),
        name="mla_attention",
    )(q_t, k, v_t)


def _rope_tables(seq):
    inv_freq = 1.0 / (ROPE_THETA ** (jnp.arange(0, QK_ROPE, 2, dtype=F32) / QK_ROPE))
    ang = jnp.arange(seq, dtype=F32)[:, None] * inv_freq[None, :]
    cos, sin = jnp.cos(ang), jnp.sin(ang)
    pad = HEAD_PAD_B - QK_NOPE - QK_ROPE
    cos_s = jnp.concatenate([jnp.ones((seq, QK_NOPE), F32), cos, cos, jnp.zeros((seq, pad), F32)], axis=1)
    sin_s = jnp.concatenate([jnp.zeros((seq, QK_NOPE), F32), sin, sin, jnp.zeros((seq, pad), F32)], axis=1)
    return cos_s, sin_s, cos_s.T, sin_s.T


def _mem_kv_kernel(mem_ref, w_ref, o_ref):
    o_ref[...] = jnp.dot(mem_ref[...], w_ref[...], preferred_element_type=F32).astype(BF16)


def _mem_kv(mem_b, w):
    rows, d = mem_b.shape
    n = w.shape[1]
    tm = 256
    return pl.pallas_call(
        _mem_kv_kernel,
        grid=(rows // tm,),
        in_specs=[pl.BlockSpec((tm, d), lambda i: (i, 0)), pl.BlockSpec((d, n), lambda i: (0, 0))],
        out_specs=pl.BlockSpec((tm, n), lambda i: (i, 0)),
        out_shape=jax.ShapeDtypeStruct((rows, n), BF16),
        compiler_params=_params("parallel"),
        name="mem_kv",
    )(mem_b, w)


def _mem_attn_kernel(q_ref, kv_ref, o_ref):
    scale = HEAD_DIM_C ** -0.5
    for h in range(HEADS_C):
        lanes = slice(h * HEAD_DIM_C, (h + 1) * HEAD_DIM_C)
        k = kv_ref[:, lanes]
        v = kv_ref[:, OUT_C + h * HEAD_DIM_C:OUT_C + (h + 1) * HEAD_DIM_C]
        s = lax.dot_general(q_ref[:, lanes], k, NT_DIMS, preferred_element_type=F32) * scale
        m = jnp.max(s, axis=1, keepdims=True)
        p = jnp.exp(s - m)
        den = jnp.sum(p, axis=1, keepdims=True)
        o_ref[:, lanes] = (jnp.dot(p.astype(BF16), v, preferred_element_type=F32) / den).astype(BF16)


def _mem_attention(qc, kv, batch, seq, n_mem):
    t = batch * seq
    ts = 1024
    steps = seq // ts
    return pl.pallas_call(
        _mem_attn_kernel,
        grid=(batch, steps),
        in_specs=[pl.BlockSpec((ts, OUT_C), lambda b, i: (b * steps + i, 0)),
                  pl.BlockSpec((n_mem, 2 * OUT_C), lambda b, i: (b, 0))],
        out_specs=pl.BlockSpec((ts, OUT_C), lambda b, i: (b * steps + i, 0)),
        out_shape=jax.ShapeDtypeStruct((t, OUT_C), BF16),
        compiler_params=_params("parallel", "parallel"),
        name="mem_attention",
    )(qc, kv)


def _merge_kernel(oa0_ref, oa1_ref, oa2_ref, l0_ref, l1_ref, l2_ref, ob_ref, oc_ref, gl_ref, x_ref,
                  wb_ref, wo_ref, g_ref, b_ref, xf_ref, xb_ref):
    l0, l1, l2 = l0_ref[...], l1_ref[...], l2_ref[...]
    m = jnp.maximum(jnp.maximum(l0, l1), l2)
    e0, e1, e2 = jnp.exp(l0 - m), jnp.exp(l1 - m), jnp.exp(l2 - m)
    oa = (e0 * oa0_ref[...].astype(F32) + e1 * oa1_ref[...].astype(F32) + e2 * oa2_ref[...].astype(F32))
    oa = (oa / (e0 + e1 + e2)).astype(BF16)
    z = None
    for i, o in enumerate((oa, ob_ref[...], oc_ref[...])):
        gate = jax.nn.sigmoid(gl_ref[:, i * D_MODEL:(i + 1) * D_MODEL].astype(F32))
        term = gate * jnp.dot(o, wb_ref[i], preferred_element_type=F32)
        z = term if z is None else z + term
    y = jnp.dot(z.astype(BF16), wo_ref[...], preferred_element_type=F32)
    out = _layer_norm_rows(DEEPNORM_ALPHA * x_ref[...] + y, g_ref[...], b_ref[...])
    xf_ref[...] = out
    xb_ref[...] = out.astype(BF16)


def _merge(oa, ob, oc, gl, x, wb, wo, g, b):
    t, d = x.shape
    tm = 512
    row = lambda i: (i, 0)
    fixed = lambda i: (0, 0)
    half = pl.BlockSpec((tm, BRANCH_WIDTH), row)
    (oa0, l0), (oa1, l1), (oa2, l2) = oa
    return pl.pallas_call(
        _merge_kernel,
        grid=(t // tm,),
        in_specs=[half] * 8 + [pl.BlockSpec((tm, SEG_GL), row), pl.BlockSpec((tm, d), row),
                               _resident((N_BRANCH, BRANCH_WIDTH, d), lambda i: (0, 0, 0)),
                               _resident((d, d), fixed),
                               pl.BlockSpec((1, d), fixed), pl.BlockSpec((1, d), fixed)],
        out_specs=[pl.BlockSpec((tm, d), row), pl.BlockSpec((tm, d), row)],
        out_shape=[jax.ShapeDtypeStruct((t, d), F32), jax.ShapeDtypeStruct((t, d), BF16)],
        compiler_params=_params("parallel"),
        name="merge_ln1",
    )(oa0, oa1, oa2, l0, l1, l2, ob, oc, gl, x, wb, wo, g, b)


GATE_LANES = 128


def _first_index_of_max(vals, idx, axis, sentinel):
    mx = jnp.max(vals, axis=axis, keepdims=True)
    return jnp.min(jnp.where(vals == mx, idx, sentinel), axis=axis, keepdims=True)


def _router_kernel(x_ref, w_ref, bias_ref, g_ref):
    tm = x_ref.shape[0]
    logits = lax.dot_general(w_ref[...], x_ref[...], NT_DIMS, preferred_element_type=F32)
    scores = jax.nn.sigmoid(logits)
    choice = scores + bias_ref[...]
    neg = -jnp.inf

    c3 = choice.reshape(N_EXPERT_GROUPS, EXPERTS_PER_GROUP, tm)
    e_idx = lax.broadcasted_iota(jnp.int32, c3.shape, 1)
    first = jnp.max(c3, axis=1, keepdims=True)
    first_at = jnp.min(jnp.where(c3 == first, e_idx, EXPERTS_PER_GROUP), axis=1, keepdims=True)
    second = jnp.max(jnp.where(e_idx == first_at, neg, c3), axis=1, keepdims=True)
    group_score = (first + second).reshape(N_EXPERT_GROUPS, tm)

    g_idx = lax.broadcasted_iota(jnp.int32, group_score.shape, 0)
    group_sel = jnp.zeros(group_score.shape, jnp.bool_)
    for _ in range(TOPK_GROUPS):
        at = _first_index_of_max(group_score, g_idx, 0, N_EXPERT_GROUPS)
        hit = g_idx == at
        group_sel = group_sel | hit
        group_score = jnp.where(hit, neg, group_score)

    allowed = jnp.broadcast_to(group_sel.reshape(N_EXPERT_GROUPS, 1, tm), c3.shape).reshape(N_EXPERTS, tm)
    cand = jnp.where(allowed, choice, neg)
    x_idx = lax.broadcasted_iota(jnp.int32, cand.shape, 0)
    chosen = jnp.zeros(cand.shape, jnp.bool_)
    for _ in range(TOP_K):
        at = _first_index_of_max(cand, x_idx, 0, N_EXPERTS)
        hit = x_idx == at
        chosen = chosen | hit
        cand = jnp.where(hit, neg, cand)

    w_sel = jnp.where(chosen, scores, 0.0)
    gates = w_sel / jnp.sum(w_sel, axis=0, keepdims=True) * ROUTED_SCALE
    padded = jnp.concatenate([gates, jnp.zeros((GATE_LANES - N_EXPERTS, tm), F32)], axis=0)
    g_ref[...] = padded.T


def _router(xb, w_t, bias):
    t, d = xb.shape
    tm = 1024
    return pl.pallas_call(
        _router_kernel,
        grid=(t // tm,),
        in_specs=[pl.BlockSpec((tm, d), lambda i: (i, 0)), pl.BlockSpec((N_EXPERTS, d), lambda i: (0, 0)),
                  pl.BlockSpec((N_EXPERTS, 1), lambda i: (0, 0))],
        out_specs=pl.BlockSpec((tm, GATE_LANES), lambda i: (i, 0)),
        out_shape=jax.ShapeDtypeStruct((t, GATE_LANES), F32),
        compiler_params=_params("parallel"),
        name="router",
    )(xb, w_t, bias)


EXPERTS_PER_STEP = 4


def _swiglu(xb, wg, wu):
    return jax.nn.silu(jnp.dot(xb, wg, preferred_element_type=F32)) * jnp.dot(xb, wu, preferred_element_type=F32)


def _moe_kernel(xb_ref, xf_ref, gates_ref, wg_ref, wu_ref, wd_ref, sg_ref, su_ref, sd_ref, g_ref, b_ref,
                of_ref, ob_ref, acc_ref):
    j = pl.program_id(1)
    xb = xb_ref[...]

    @pl.when(j == 0)
    def _():
        h = _swiglu(xb, sg_ref[...], su_ref[...])
        acc_ref[...] = jnp.dot(h.astype(BF16), sd_ref[...], preferred_element_type=F32)

    gates = gates_ref[...]
    lane = lax.broadcasted_iota(jnp.int32, gates.shape, 1)
    for e in range(EXPERTS_PER_STEP):
        gate = jnp.sum(jnp.where(lane == j * EXPERTS_PER_STEP + e, gates, 0.0), axis=1, keepdims=True)
        h = _swiglu(xb, wg_ref[e], wu_ref[e]) * gate
        acc_ref[...] += jnp.dot(h.astype(BF16), wd_ref[e], preferred_element_type=F32)

    @pl.when(j == pl.num_programs(1) - 1)
    def _():
        out = _layer_norm_rows(DEEPNORM_ALPHA * xf_ref[...] + acc_ref[...], g_ref[...], b_ref[...])
        of_ref[...] = out
        ob_ref[...] = out.astype(BF16)


def _moe(xb, xf, gates, wg, wu, wd, sg, su, sd, g, b):
    t, d = xf.shape
    tm = 1024
    f = D_EXPERT
    es = EXPERTS_PER_STEP
    row = lambda i, j: (i, 0)
    fixed = lambda i, j: (0, 0)
    return pl.pallas_call(
        _moe_kernel,
        grid=(t // tm, N_EXPERTS // es),
        in_specs=[pl.BlockSpec((tm, d), row), pl.BlockSpec((tm, d), row), pl.BlockSpec((tm, GATE_LANES), row),
                  pl.BlockSpec((es, d, f), lambda i, j: (j, 0, 0)), pl.BlockSpec((es, d, f), lambda i, j: (j, 0, 0)),
                  pl.BlockSpec((es, f, d), lambda i, j: (j, 0, 0)),
                  pl.BlockSpec((d, f), fixed), pl.BlockSpec((d, f), fixed), pl.BlockSpec((f, d), fixed),
                  pl.BlockSpec((1, d), fixed), pl.BlockSpec((1, d), fixed)],
        out_specs=[pl.BlockSpec((tm, d), row), pl.BlockSpec((tm, d), row)],
        out_shape=[jax.ShapeDtypeStruct((t, d), F32), jax.ShapeDtypeStruct((t, d), BF16)],
        scratch_shapes=[pltpu.VMEM((tm, d), F32)],
        compiler_params=_params("parallel", "arbitrary"),
        name="moe_ln2",
    )(xb, xf, gates, wg, wu, wd, sg, su, sd, g, b)


def _rotate_half_columns(w):
    half = QK_ROPE // 2
    return jnp.concatenate([-w[..., half:], w[..., :half]], axis=-1)


def _prep_in_proj(w_in):
    layers, d, _ = w_in.shape
    cuts = np.cumsum((WIDTH_A, WIDTH_A, WIDTH_A, Q_LORA, KV_LORA, QK_ROPE, OUT_C))
    qa, ka, va, cq, ckv, kr, qc, gl = jnp.split(w_in, [int(c) for c in cuts], axis=-1)
    lead = jnp.zeros((layers, d, QK_NOPE), w_in.dtype)
    tail = jnp.zeros((layers, d, HEAD_PAD_B - QK_NOPE - QK_ROPE), w_in.dtype)
    kr_slot = jnp.concatenate([lead, kr, tail], axis=-1)
    kr_rot_slot = jnp.concatenate([lead, _rotate_half_columns(kr), tail], axis=-1)
    return jnp.concatenate([qa, ka, va, cq, ckv, kr_slot, kr_rot_slot, qc, gl], axis=-1).astype(BF16)


def _prep_mla_weights(w_q_up, w_kv_up):
    layers = w_q_up.shape[0]
    wq = w_q_up.reshape(layers, Q_LORA, HEADS_B, QK_NOPE + QK_ROPE)
    nope, rope = wq[..., :QK_NOPE], wq[..., QK_NOPE:]
    pad = HEAD_PAD_B - QK_NOPE - QK_ROPE
    zq = lambda n: jnp.zeros((layers, Q_LORA, HEADS_B, n), w_q_up.dtype)
    wqa = jnp.concatenate([nope, rope, zq(pad)], axis=-1)
    wqb = jnp.concatenate([zq(QK_NOPE), _rotate_half_columns(rope), zq(pad)], axis=-1)
    wkv = w_kv_up.reshape(layers, KV_LORA, HEADS_B, QK_NOPE + V_DIM_B)
    zk = lambda n: jnp.zeros((layers, KV_LORA, HEADS_B, n), w_kv_up.dtype)
    wk = jnp.concatenate([wkv[..., :QK_NOPE], zk(HEAD_PAD_B - QK_NOPE)], axis=-1)
    wv = jnp.concatenate([wkv[..., QK_NOPE:], zk(V_ROWS_B - V_DIM_B)], axis=-1)
    flat = lambda w: w.reshape(layers, w.shape[1], -1).astype(BF16)
    flat_t = lambda w: jnp.swapaxes(flat(w), 1, 2)
    return flat_t(wqa), flat_t(wqb), flat(wk), flat_t(wv)


def _trunk(x, mem, emb_g, emb_b, w, depth):
    batch, seq, d = x.shape
    n_mem = mem.shape[1]
    t = batch * seq
    row2 = lambda v: v.reshape(1, -1)
    xf, xb = _embed_ln(x.reshape(t, d), row2(emb_g), row2(emb_b))
    mem_b = mem.reshape(batch * n_mem, d).astype(BF16)
    tables = _rope_tables(seq)
    for l in range(depth):
        qkv, mla, qc, gl = _project(xb, w["in_proj"][l])
        oa = _dilated_mixer(qkv, batch, seq)
        q_t, k, v_t = _mla_prep(mla, tables, row2(w["q_norm_g"][l]), w["wqa"][l], w["wqb"][l],
                                row2(w["kv_norm_g"][l]), w["wk"][l], w["wv"][l], batch, seq)
        ob = _mla_attention(q_t, k, v_t, batch, seq)
        oc = _mem_attention(qc, _mem_kv(mem_b, w["mem_kv"][l]), batch, seq, n_mem)
        xf, xb = _merge(oa, ob, oc, gl, xf, w["branch"][l], w["out"][l], row2(w["ln1_g"][l]), row2(w["ln1_b"][l]))
        gates = _router(xb, w["router_t"][l], w["router_bias"][l].reshape(N_EXPERTS, 1))
        xf, xb = _moe(xb, xf, gates, w["exp_gate"][l], w["exp_up"][l], w["exp_down"][l],
                      w["sh_gate"][l], w["sh_up"][l], w["sh_down"][l], row2(w["ln2_g"][l]), row2(w["ln2_b"][l]))
    return xf.reshape(batch, seq, d)


def kernel(x_prompt, x_sample, mem_prompt, mem_sample, emb_ln_g, emb_ln_b, w_in, q_norm_g, w_q_up, kv_norm_g,
           w_kv_up, w_mem_kv, w_branch, w_out, ln1_g, ln1_b, w_router, router_bias, w_exp_gate, w_exp_up,
           w_exp_down, w_sh_gate, w_sh_up, w_sh_down, ln2_g, ln2_b):
    wqa, wqb, wk, wv = _prep_mla_weights(w_q_up, w_kv_up)
    w = {
        "in_proj": _prep_in_proj(w_in),
        "q_norm_g": q_norm_g, "kv_norm_g": kv_norm_g, "wqa": wqa, "wqb": wqb, "wk": wk, "wv": wv,
        "mem_kv": w_mem_kv.astype(BF16), "branch": w_branch.astype(BF16), "out": w_out.astype(BF16),
        "ln1_g": ln1_g, "ln1_b": ln1_b,
        "router_t": jnp.swapaxes(w_router, 1, 2).astype(BF16), "router_bias": router_bias,
        "exp_gate": w_exp_gate.astype(BF16), "exp_up": w_exp_up.astype(BF16), "exp_down": w_exp_down.astype(BF16),
        "sh_gate": w_sh_gate.astype(BF16), "sh_up": w_sh_up.astype(BF16), "sh_down": w_sh_down.astype(BF16),
        "ln2_g": ln2_g, "ln2_b": ln2_b,
    }
    depth = w_in.shape[0]
    y_prompt = _trunk(x_prompt, mem_prompt, emb_ln_g, emb_ln_b, w, depth)
    y_sample = _trunk(x_sample, mem_sample, emb_ln_g, emb_ln_b, w, depth)
    return (y_prompt, y_sample)
```

```python
import functools
import math

import numpy as np
import jax
import jax.numpy as jnp
from jax import lax
from jax.experimental import pallas as pl
from jax.experimental.pallas import tpu as pltpu

F32 = jnp.float32
BF16 = jnp.bfloat16

D_MODEL = 1024
DEPTH = 4
DIL_GROUPS = ((128, 1), (512, 4), (2048, 16))
N_GROUPS_A = 3
HEADS_A = 4
HEAD_DIM_A = 128
GROUP_WIDTH_A = HEADS_A * HEAD_DIM_A
WIDTH_A = N_GROUPS_A * GROUP_WIDTH_A
RADIUS_A = 64
HEADS_B = 8
Q_LORA = 256
KV_LORA = 128
QK_NOPE = 64
QK_ROPE = 32
V_DIM_B = 64
ROPE_THETA = 10000.0
HEAD_PAD_B = 128
HEADS_C = 4
HEAD_DIM_C = 128
OUT_C = HEADS_C * HEAD_DIM_C
N_BRANCH = 3
BRANCH_WIDTH = 512
N_EXPERTS = 64
TOP_K = 8
N_EXPERT_GROUPS = 8
EXPERTS_PER_GROUP = N_EXPERTS // N_EXPERT_GROUPS
TOPK_GROUPS = 4
D_EXPERT = 256
ROUTED_SCALE = 2.5
DEEPNORM_ALPHA = (2 * DEPTH) ** 0.25
LN_EPS = 1e-5
RMS_EPS = 1e-6

SEG_QKV = 3 * WIDTH_A
SEG_MLA = Q_LORA + KV_LORA + 2 * HEAD_PAD_B
SEG_QC = OUT_C
SEG_GL = N_BRANCH * D_MODEL
N_PROJ = SEG_QKV + SEG_MLA + SEG_QC + SEG_GL

NEG_BIG = -1e30
VMEM_LIMIT = 56 * 2 ** 20

NT_DIMS = (((1,), (1,)), ((), ()))


def _params(*sem):
    return pltpu.CompilerParams(dimension_semantics=sem, vmem_limit_bytes=VMEM_LIMIT)


def _resident(block_shape, index_map):
    return pl.BlockSpec(block_shape, index_map, pipeline_mode=pl.Buffered(1))


def _layer_norm_rows(h, g, b):
    mu = jnp.mean(h, axis=-1, keepdims=True)
    c = h - mu
    var = jnp.mean(c * c, axis=-1, keepdims=True)
    return c * lax.rsqrt(var + LN_EPS) * g + b


def _rms_norm_rows(h, g):
    return h * lax.rsqrt(jnp.mean(h * h, axis=-1, keepdims=True) + RMS_EPS) * g


def _embed_ln_kernel(x_ref, g_ref, b_ref, xf_ref):
    xf_ref[...] = _layer_norm_rows(x_ref[...], g_ref[...], b_ref[...])


def _embed_ln(x, g, b):
    t, d = x.shape
    tm = 512
    row = lambda i: (i, 0)
    fixed = lambda i: (0, 0)
    return pl.pallas_call(
        _embed_ln_kernel,
        grid=(t // tm,),
        in_specs=[pl.BlockSpec((tm, d), row), pl.BlockSpec((1, d), fixed), pl.BlockSpec((1, d), fixed)],
        out_specs=pl.BlockSpec((tm, d), row),
        out_shape=jax.ShapeDtypeStruct((t, d), F32),
        compiler_params=_params("parallel"),
        name="embed_ln",
    )(x, g, b)


PROJ_CHUNK = 512


def _proj_kernel(x_ref, w_ref, qkv_ref, mla_ref, qc_ref, gl_ref):
    xb = x_ref[...].astype(BF16)
    col = 0
    for ref, width in ((qkv_ref, SEG_QKV), (mla_ref, SEG_MLA), (qc_ref, SEG_QC), (gl_ref, SEG_GL)):
        for c in range(0, width, PROJ_CHUNK):
            w = min(PROJ_CHUNK, width - c)
            ref[:, c:c + w] = jnp.dot(xb, w_ref[:, col + c:col + c + w],
                                      preferred_element_type=F32).astype(BF16)
        col += width


def _project(x, w):
    t, d = x.shape
    tm = 512
    row = lambda i: (i, 0)
    widths = (SEG_QKV, SEG_MLA, SEG_QC, SEG_GL)
    return pl.pallas_call(
        _proj_kernel,
        grid=(t // tm,),
        in_specs=[pl.BlockSpec((tm, d), row), _resident((d, N_PROJ), lambda i: (0, 0))],
        out_specs=[pl.BlockSpec((tm, n), row) for n in widths],
        out_shape=[jax.ShapeDtypeStruct((t, n), BF16) for n in widths],
        compiler_params=_params("parallel"),
        name="in_proj",
    )(x, w)


BAND_TQ = 512
BAND_QB = 128
BAND_KB = BAND_QB + 2 * RADIUS_A


def _band_kernel(q_ref, kp_ref, km_ref, kn_ref, vp_ref, vm_ref, vn_ref, o_ref, lse_ref, k_scr, v_scr,
                 *, seq_len, slopes):
    i = pl.program_id(1)
    r = RADIUS_A
    k_scr[0:r, :] = kp_ref[...]
    k_scr[r:r + BAND_TQ, :] = km_ref[...]
    k_scr[r + BAND_TQ:, :] = kn_ref[...]
    v_scr[0:r, :] = vp_ref[...]
    v_scr[r:r + BAND_TQ, :] = vm_ref[...]
    v_scr[r + BAND_TQ:, :] = vn_ref[...]

    row = lax.broadcasted_iota(jnp.int32, (BAND_QB, BAND_KB), 0)
    col = lax.broadcasted_iota(jnp.int32, (BAND_QB, BAND_KB), 1)
    rel = col - r - row
    dist = jnp.abs(rel).astype(F32)
    in_band = jnp.abs(rel) <= r
    scale = HEAD_DIM_A ** -0.5

    for qb in range(BAND_TQ // BAND_QB):
        key_pos = i * BAND_TQ + qb * BAND_QB - r + col
        valid = in_band & (key_pos >= 0) & (key_pos < seq_len)
        for h in range(HEADS_A):
            lanes = slice(h * HEAD_DIM_A, (h + 1) * HEAD_DIM_A)
            q = q_ref[qb * BAND_QB:(qb + 1) * BAND_QB, lanes]
            k = k_scr[qb * BAND_QB:qb * BAND_QB + BAND_KB, lanes]
            v = v_scr[qb * BAND_QB:qb * BAND_QB + BAND_KB, lanes]
            s = lax.dot_general(q, k, NT_DIMS, preferred_element_type=F32)
            logits = jnp.where(valid, s * scale - slopes[h] * dist, NEG_BIG)
            m = jnp.max(logits, axis=1, keepdims=True)
            p = jnp.exp(logits - m)
            den = jnp.sum(p, axis=1, keepdims=True)
            o = jnp.dot(p.astype(BF16), v, preferred_element_type=F32) / den
            rows = slice(qb * BAND_QB, (qb + 1) * BAND_QB)
            o_ref[rows, lanes] = o.astype(BF16)
            lse_ref[rows, lanes] = jnp.broadcast_to(m + jnp.log(den), (BAND_QB, HEAD_DIM_A))


def _band_attention(q_src, k_src, v_src, n_seq, seq_len, slopes):
    tq, r = BAND_TQ, RADIUS_A
    assert seq_len % tq == 0 and tq % r == 0
    steps = seq_len // tq
    halo_per_tile = tq // r
    halo_blocks = seq_len // r

    def main_map(cb):
        return lambda n, i: (n * steps + i, cb)

    def prev_map(cb):
        return lambda n, i: (n * halo_blocks + jnp.maximum(i * halo_per_tile - 1, 0), cb)

    def next_map(cb):
        return lambda n, i: (n * halo_blocks + jnp.minimum((i + 1) * halo_per_tile, halo_blocks - 1), cb)

    (qa, qcb), (ka, kcb), (va, vcb) = q_src, k_src, v_src
    w = GROUP_WIDTH_A
    rows = n_seq * seq_len
    return pl.pallas_call(
        functools.partial(_band_kernel, seq_len=seq_len, slopes=slopes),
        grid=(n_seq, steps),
        in_specs=[
            pl.BlockSpec((tq, w), main_map(qcb)),
            pl.BlockSpec((r, w), prev_map(kcb)), pl.BlockSpec((tq, w), main_map(kcb)),
            pl.BlockSpec((r, w), next_map(kcb)),
            pl.BlockSpec((r, w), prev_map(vcb)), pl.BlockSpec((tq, w), main_map(vcb)),
            pl.BlockSpec((r, w), next_map(vcb)),
        ],
        out_specs=[pl.BlockSpec((tq, w), main_map(0)), pl.BlockSpec((tq, w), main_map(0))],
        out_shape=[jax.ShapeDtypeStruct((rows, w), BF16), jax.ShapeDtypeStruct((rows, w), F32)],
        scratch_shapes=[pltpu.VMEM((tq + 2 * r, w), BF16), pltpu.VMEM((tq + 2 * r, w), BF16)],
        compiler_params=_params("parallel", "parallel"),
        name="band_attention",
    )(qa, ka, ka, ka, va, va, va)


def _alibi_slopes():
    n = N_GROUPS_A * HEADS_A
    return [2.0 ** (-8.0 * (i + 1) / n) for i in range(n)]


def _dilated_mixer(qkv, batch, seq):
    slopes = _alibi_slopes()
    t = batch * seq
    outs = []
    for g, (_, d) in enumerate(DIL_GROUPS):
        group_slopes = tuple(float(s * d) for s in slopes[g * HEADS_A:(g + 1) * HEADS_A])
        if d == 1:
            o, lse = _band_attention((qkv, g), (qkv, N_GROUPS_A + g), (qkv, 2 * N_GROUPS_A + g),
                                     batch, seq, group_slopes)
        else:
            length = seq // d

            def split(m):
                c0 = (m * N_GROUPS_A + g) * GROUP_WIDTH_A
                x = qkv[:, c0:c0 + GROUP_WIDTH_A].reshape(batch, length, d, GROUP_WIDTH_A)
                return x.transpose(0, 2, 1, 3).reshape(t, GROUP_WIDTH_A)

            o, lse = _band_attention((split(0), 0), (split(1), 0), (split(2), 0), batch * d, length, group_slopes)

            def merge(x):
                return x.reshape(batch, d, length, GROUP_WIDTH_A).transpose(0, 2, 1, 3).reshape(t, GROUP_WIDTH_A)

            o, lse = merge(o), merge(lse)
        outs.append((o, lse))
    return outs


V_ROWS_B = 80
MLA_TQ = 1024
MLA_TK = 2048
MLA_KS = 256
MLA_LOOKAHEAD = 2
MLA_HEADS_PER_STEP = 2


def _mla_prep_kernel(mla_ref, cos_ref, sin_ref, cos_t_ref, sin_t_ref, gq_ref, wqa_ref, wqb_ref, gkv_ref,
                     wk_ref, wv_ref, qt_ref, k_ref, vt_ref):
    m = mla_ref[...]
    cq = m[:, 0:Q_LORA].astype(F32)
    ckv = m[:, Q_LORA:Q_LORA + KV_LORA].astype(F32)
    kr = m[:, Q_LORA + KV_LORA:Q_LORA + KV_LORA + HEAD_PAD_B].astype(F32)
    kr_rot = m[:, Q_LORA + KV_LORA + HEAD_PAD_B:].astype(F32)
    scale = (QK_NOPE + QK_ROPE) ** -0.5 * math.log2(math.e)

    cqn = _rms_norm_rows(cq, gq_ref[...]).astype(BF16)
    qa_t = lax.dot_general(wqa_ref[...], cqn, NT_DIMS, preferred_element_type=F32)
    qb_t = lax.dot_general(wqb_ref[...], cqn, NT_DIMS, preferred_element_type=F32)
    ckvn = _rms_norm_rows(ckv, gkv_ref[...]).astype(BF16)
    kn = jnp.dot(ckvn, wk_ref[...], preferred_element_type=F32)
    v_t = lax.dot_general(wv_ref[...], ckvn, NT_DIMS, preferred_element_type=F32)
    k_rope = kr * cos_ref[...] + kr_rot * sin_ref[...]
    cos_t, sin_t = cos_t_ref[...], sin_t_ref[...]
    row = lax.broadcasted_iota(jnp.int32, (V_ROWS_B, 1), 0)
    ones_row = (row == V_DIM_B).astype(F32)
    for h in range(HEADS_B):
        slot = slice(h * HEAD_PAD_B, (h + 1) * HEAD_PAD_B)
        qt_ref[slot, :] = ((qa_t[slot, :] * cos_t + qb_t[slot, :] * sin_t) * scale).astype(BF16)
        k_ref[:, slot] = (kn[:, slot] + k_rope).astype(BF16)
        vt_ref[0, h, 0] = (v_t[h * V_ROWS_B:(h + 1) * V_ROWS_B, :] + ones_row).astype(BF16)


def _mla_prep(mla, tables, gq, wqa_t, wqb_t, gkv, wk, wv_t, batch, seq):
    t = mla.shape[0]
    tm = MLA_TK
    chunks = seq // tm
    cos, sin, cos_t, sin_t = tables
    row = lambda i: (i, 0)
    pos = lambda i: (i % chunks, 0)
    pos_t = lambda i: (0, i % chunks)
    fixed = lambda i: (0, 0)
    wide = HEADS_B * HEAD_PAD_B
    return pl.pallas_call(
        _mla_prep_kernel,
        grid=(t // tm,),
        in_specs=[pl.BlockSpec((tm, SEG_MLA), row),
                  pl.BlockSpec((tm, HEAD_PAD_B), pos), pl.BlockSpec((tm, HEAD_PAD_B), pos),
                  pl.BlockSpec((HEAD_PAD_B, tm), pos_t), pl.BlockSpec((HEAD_PAD_B, tm), pos_t),
                  pl.BlockSpec((1, Q_LORA), fixed), pl.BlockSpec((wide, Q_LORA), fixed),
                  pl.BlockSpec((wide, Q_LORA), fixed),
                  pl.BlockSpec((1, KV_LORA), fixed), pl.BlockSpec((KV_LORA, wide), fixed),
                  pl.BlockSpec((HEADS_B * V_ROWS_B, KV_LORA), fixed)],
        out_specs=[pl.BlockSpec((wide, tm), lambda i: (i // chunks, i % chunks)),
                   pl.BlockSpec((tm, wide), row),
                   pl.BlockSpec((1, HEADS_B, 1, V_ROWS_B, tm), lambda i: (i // chunks, 0, i % chunks, 0, 0))],
        out_shape=[jax.ShapeDtypeStruct((batch * wide, seq), BF16),
                   jax.ShapeDtypeStruct((t, wide), BF16),
                   jax.ShapeDtypeStruct((batch, HEADS_B, chunks, V_ROWS_B, tm), BF16)],
        compiler_params=_params("parallel"),
        name="mla_prep",
    )(mla, cos, sin, cos_t, sin_t, gq, wqa_t, wqb_t, gkv, wk, wv_t)


def _mla_attn_kernel(qt_ref, k_ref, vt_ref, o_ref, *, seq):
    n_sub = MLA_TK // MLA_KS
    units = [(c, hh) for c in range(n_sub) for hh in range(MLA_HEADS_PER_STEP)]

    def step(kc, carry):
        new = list(carry)

        def scores(u):
            c, hh = units[u]
            slot = slice(hh * HEAD_PAD_B, (hh + 1) * HEAD_PAD_B)
            rows = pl.ds(pl.multiple_of(kc * MLA_TK + c * MLA_KS, MLA_KS), MLA_KS)
            return jnp.dot(k_ref[rows, slot], qt_ref[slot, :], preferred_element_type=F32)

        pending = [scores(u) for u in range(min(MLA_LOOKAHEAD, len(units)))]
        for u, (c, hh) in enumerate(units):
            if u + MLA_LOOKAHEAD < len(units):
                pending.append(scores(u + MLA_LOOKAHEAD))
            s = pending.pop(0)
            m, acc = new[hh]
            m_new = jnp.maximum(m, jnp.max(s, axis=0, keepdims=True))
            p = jnp.exp2(s - m_new).astype(BF16)
            v_blk = vt_ref[0, hh, kc, :, c * MLA_KS:(c + 1) * MLA_KS]
            acc = jnp.exp2(m - m_new) * acc + jnp.dot(v_blk, p, preferred_element_type=F32)
            new[hh] = (m_new, acc)
        return tuple(new)

    init = tuple((jnp.full((1, MLA_TQ), NEG_BIG, F32), jnp.zeros((V_ROWS_B, MLA_TQ), F32))
                 for _ in range(MLA_HEADS_PER_STEP))
    final = lax.fori_loop(0, seq // MLA_TK, step, init)
    heads = [acc[0:V_DIM_B, :] / acc[V_DIM_B:V_DIM_B + 1, :] for _, acc in final]
    o_ref[...] = jnp.concatenate(heads, axis=0).T.astype(BF16)


def _mla_attention(q_t, k, v_t, batch, seq):
    t = batch * seq
    tq = MLA_TQ
    steps = seq // tq
    chunks = seq // MLA_TK
    pairs = HEADS_B // MLA_HEADS_PER_STEP
    pair = MLA_HEADS_PER_STEP * HEAD_PAD_B
    return pl.pallas_call(
        functools.partial(_mla_attn_kernel, seq=seq),
        grid=(batch, pairs, steps),
        in_specs=[pl.BlockSpec((pair, tq), lambda b, hp, i: (b * pairs + hp, i)),
                  _resident((seq, pair), lambda b, hp, i: (b, hp)),
                  _resident((1, MLA_HEADS_PER_STEP, chunks, V_ROWS_B, MLA_TK), lambda b, hp, i: (b, hp, 0, 0, 0))],
        out_specs=pl.BlockSpec((tq, MLA_HEADS_PER_STEP * V_DIM_B), lambda b, hp, i: (b * steps + i, hp)),
        out_shape=jax.ShapeDtypeStruct((t, HEADS_B * V_DIM_B), BF16),
        compiler_params=_params("parallel", "parallel", "arbitrary"),
        name="mla_attention",
    )(q_t, k, v_t)


def _rope_tables(seq):
    inv_freq = 1.0 / (ROPE_THETA ** (jnp.arange(0, QK_ROPE, 2, dtype=F32) / QK_ROPE))
    ang = jnp.arange(seq, dtype=F32)[:, None] * inv_freq[None, :]
    cos, sin = jnp.cos(ang), jnp.sin(ang)
    pad = HEAD_PAD_B - QK_NOPE - QK_ROPE
    cos_s = jnp.concatenate([jnp.ones((seq, QK_NOPE), F32), cos, cos, jnp.zeros((seq, pad), F32)], axis=1)
    sin_s = jnp.concatenate([jnp.zeros((seq, QK_NOPE), F32), sin, sin, jnp.zeros((seq, pad), F32)], axis=1)
    return cos_s, sin_s, cos_s.T, sin_s.T


def _mem_kv_kernel(mem_ref, w_ref, o_ref):
    o_ref[...] = jnp.dot(mem_ref[...], w_ref[...], preferred_element_type=F32).astype(BF16)


def _mem_kv(mem_b, w):
    rows, d = mem_b.shape
    n = w.shape[1]
    tm = 256
    return pl.pallas_call(
        _mem_kv_kernel,
        grid=(rows // tm,),
        in_specs=[pl.BlockSpec((tm, d), lambda i: (i, 0)), pl.BlockSpec((d, n), lambda i: (0, 0))],
        out_specs=pl.BlockSpec((tm, n), lambda i: (i, 0)),
        out_shape=jax.ShapeDtypeStruct((rows, n), BF16),
        compiler_params=_params("parallel"),
        name="mem_kv",
    )(mem_b, w)


def _mem_attn_kernel(q_ref, kv_ref, o_ref):
    scale = HEAD_DIM_C ** -0.5
    for h in range(HEADS_C):
        lanes = slice(h * HEAD_DIM_C, (h + 1) * HEAD_DIM_C)
        k = kv_ref[:, lanes]
        v = kv_ref[:, OUT_C + h * HEAD_DIM_C:OUT_C + (h + 1) * HEAD_DIM_C]
        s = lax.dot_general(q_ref[:, lanes], k, NT_DIMS, preferred_element_type=F32) * scale
        m = jnp.max(s, axis=1, keepdims=True)
        p = jnp.exp(s - m)
        den = jnp.sum(p, axis=1, keepdims=True)
        o_ref[:, lanes] = (jnp.dot(p.astype(BF16), v, preferred_element_type=F32) / den).astype(BF16)


def _mem_attention(qc, kv, batch, seq, n_mem):
    t = batch * seq
    ts = 1024
    steps = seq // ts
    return pl.pallas_call(
        _mem_attn_kernel,
        grid=(batch, steps),
        in_specs=[pl.BlockSpec((ts, OUT_C), lambda b, i: (b * steps + i, 0)),
                  pl.BlockSpec((n_mem, 2 * OUT_C), lambda b, i: (b, 0))],
        out_specs=pl.BlockSpec((ts, OUT_C), lambda b, i: (b * steps + i, 0)),
        out_shape=jax.ShapeDtypeStruct((t, OUT_C), BF16),
        compiler_params=_params("parallel", "parallel"),
        name="mem_attention",
    )(qc, kv)


def _merge_kernel(oa0_ref, oa1_ref, oa2_ref, l0_ref, l1_ref, l2_ref, ob_ref, oc_ref, gl_ref, x_ref,
                  wb_ref, wo_ref, g_ref, b_ref, xf_ref):
    l0, l1, l2 = l0_ref[...], l1_ref[...], l2_ref[...]
    m = jnp.maximum(jnp.maximum(l0, l1), l2)
    e0, e1, e2 = jnp.exp(l0 - m), jnp.exp(l1 - m), jnp.exp(l2 - m)
    oa = (e0 * oa0_ref[...].astype(F32) + e1 * oa1_ref[...].astype(F32) + e2 * oa2_ref[...].astype(F32))
    oa = (oa / (e0 + e1 + e2)).astype(BF16)
    z = None
    for i, o in enumerate((oa, ob_ref[...], oc_ref[...])):
        gate = jax.nn.sigmoid(gl_ref[:, i * D_MODEL:(i + 1) * D_MODEL].astype(F32))
        term = gate * jnp.dot(o, wb_ref[i], preferred_element_type=F32)
        z = term if z is None else z + term
    y = jnp.dot(z.astype(BF16), wo_ref[...], preferred_element_type=F32)
    xf_ref[...] = _layer_norm_rows(DEEPNORM_ALPHA * x_ref[...] + y, g_ref[...], b_ref[...])


def _merge(oa, ob, oc, gl, x, wb, wo, g, b):
    t, d = x.shape
    tm = 512
    row = lambda i: (i, 0)
    fixed = lambda i: (0, 0)
    half = pl.BlockSpec((tm, BRANCH_WIDTH), row)
    (oa0, l0), (oa1, l1), (oa2, l2) = oa
    return pl.pallas_call(
        _merge_kernel,
        grid=(t // tm,),
        in_specs=[half] * 8 + [pl.BlockSpec((tm, SEG_GL), row), pl.BlockSpec((tm, d), row),
                               _resident((N_BRANCH, BRANCH_WIDTH, d), lambda i: (0, 0, 0)),
                               _resident((d, d), fixed),
                               pl.BlockSpec((1, d), fixed), pl.BlockSpec((1, d), fixed)],
        out_specs=pl.BlockSpec((tm, d), row),
        out_shape=jax.ShapeDtypeStruct((t, d), F32),
        compiler_params=_params("parallel"),
        name="merge_ln1",
    )(oa0, oa1, oa2, l0, l1, l2, ob, oc, gl, x, wb, wo, g, b)


GATE_LANES = 128


def _first_index_of_max(vals, idx, axis, sentinel):
    mx = jnp.max(vals, axis=axis, keepdims=True)
    return jnp.min(jnp.where(vals == mx, idx, sentinel), axis=axis, keepdims=True)


def _router_kernel(x_ref, w_ref, bias_ref, xg_ref, code_ref):
    tm = x_ref.shape[0]
    x = x_ref[...]
    logits = lax.dot_general(w_ref[...], x.astype(BF16), NT_DIMS, preferred_element_type=F32)
    scores = jax.nn.sigmoid(logits)
    choice = scores + bias_ref[...]
    neg = -jnp.inf

    c3 = choice.reshape(N_EXPERT_GROUPS, EXPERTS_PER_GROUP, tm)
    e_idx = lax.broadcasted_iota(jnp.int32, c3.shape, 1)
    first = jnp.max(c3, axis=1, keepdims=True)
    first_at = jnp.min(jnp.where(c3 == first, e_idx, EXPERTS_PER_GROUP), axis=1, keepdims=True)
    second = jnp.max(jnp.where(e_idx == first_at, neg, c3), axis=1, keepdims=True)
    group_score = (first + second).reshape(N_EXPERT_GROUPS, tm)

    g_idx = lax.broadcasted_iota(jnp.int32, group_score.shape, 0)
    group_sel = jnp.zeros(group_score.shape, jnp.bool_)
    for _ in range(TOPK_GROUPS):
        at = _first_index_of_max(group_score, g_idx, 0, N_EXPERT_GROUPS)
        hit = g_idx == at
        group_sel = group_sel | hit
        group_score = jnp.where(hit, neg, group_score)

    allowed = jnp.broadcast_to(group_sel.reshape(N_EXPERT_GROUPS, 1, tm), c3.shape).reshape(N_EXPERTS, tm)
    cand = jnp.where(allowed, choice, neg)
    x_idx = lax.broadcasted_iota(jnp.int32, cand.shape, 0)
    chosen = jnp.zeros(cand.shape, jnp.bool_)
    for _ in range(TOP_K):
        at = _first_index_of_max(cand, x_idx, 0, N_EXPERTS)
        hit = x_idx == at
        chosen = chosen | hit
        cand = jnp.where(hit, neg, cand)

    w_sel = jnp.where(chosen, scores, 0.0)
    gates = w_sel / jnp.sum(w_sel, axis=0, keepdims=True) * ROUTED_SCALE
    padded = jnp.concatenate([gates, jnp.zeros((GATE_LANES - N_EXPERTS, tm), F32)], axis=0)
    xg_ref[:, 0:D_MODEL] = x
    xg_ref[:, D_MODEL:] = padded.T
    bit = jnp.left_shift(1, g_idx)
    code_ref[...] = jnp.sum(jnp.where(group_sel, bit, 0), axis=0, keepdims=True)


def _router(x, w_t, bias):
    t, d = x.shape
    tm = 1024
    return pl.pallas_call(
        _router_kernel,
        grid=(t // tm,),
        in_specs=[pl.BlockSpec((tm, d), lambda i: (i, 0)), pl.BlockSpec((N_EXPERTS, d), lambda i: (0, 0)),
                  pl.BlockSpec((N_EXPERTS, 1), lambda i: (0, 0))],
        out_specs=[pl.BlockSpec((tm, XG_WIDTH), lambda i: (i, 0)), pl.BlockSpec((1, tm), lambda i: (0, i))],
        out_shape=[jax.ShapeDtypeStruct((t, XG_WIDTH), F32), jax.ShapeDtypeStruct((1, t), jnp.int32)],
        compiler_params=_params("parallel"),
        name="router",
    )(x, w_t, bias)


XG_WIDTH = D_MODEL + GATE_LANES
MOE_TM = 512
MOE_SUB = 128
MOE_NSUB = MOE_TM // MOE_SUB


def _group_set_rank():
    codes = [c for c in range(1 << N_EXPERT_GROUPS) if bin(c).count("1") == TOPK_GROUPS]
    seq = [codes.pop(0)]
    while codes:
        nxt = min(codes, key=lambda c: (bin(c ^ seq[-1]).count("1"), c))
        codes.remove(nxt)
        seq.append(nxt)
    rank = np.zeros((1 << N_EXPERT_GROUPS,), np.int32)
    for r, c in enumerate(seq):
        rank[c] = r
    return rank


def _dispatch_plan(code):
    t = code.shape[1]
    tiles = t // MOE_TM
    c = code[0]
    perm = jnp.argsort(jnp.asarray(_group_set_rank())[c], stable=True).astype(jnp.int32)
    bits = (c[perm][:, None] >> jnp.arange(N_EXPERT_GROUPS, dtype=jnp.int32)[None, :]) & 1
    sub = jnp.max(bits.reshape(tiles, MOE_NSUB, MOE_SUB, N_EXPERT_GROUPS), axis=2)
    tile_active = jnp.max(sub, axis=1)
    n_active = jnp.sum(tile_active, axis=1).astype(jnp.int32)
    order = jnp.argsort(1 - tile_active, axis=1, stable=True).astype(jnp.int32)
    step = jnp.arange(N_EXPERT_GROUPS, dtype=jnp.int32)[None, :]
    last = jnp.take_along_axis(order, jnp.maximum(n_active - 1, 0)[:, None], axis=1)
    groups = jnp.where(step < n_active[:, None], order, last)
    flags = jnp.transpose(sub, (0, 2, 1)).astype(jnp.int32)
    return perm.reshape(tiles, 1, MOE_TM), groups.reshape(-1), n_active, flags.reshape(-1)


def _row_dma(src_ref, dst_ref, src_row, dst_row, sem):
    return pltpu.make_async_copy(src_ref.at[pl.ds(src_row, 1)], dst_ref.at[pl.ds(dst_row, 1)], sem)


def _gather_rows_kernel(perm_ref, src_ref, out_ref, sem):
    def start(r, carry):
        _row_dma(src_ref, out_ref, perm_ref[0, 0, r], r, sem).start()
        return carry

    lax.fori_loop(0, MOE_TM, start, 0)
    pltpu.make_async_copy(src_ref.at[pl.ds(0, MOE_TM)], out_ref, sem).wait()


def _gather_rows(src, perm):
    t, width = src.shape
    return pl.pallas_call(
        _gather_rows_kernel,
        grid=(t // MOE_TM,),
        in_specs=[pl.BlockSpec((1, 1, MOE_TM), lambda i: (i, 0, 0), memory_space=pltpu.SMEM),
                  pl.BlockSpec(memory_space=pl.ANY)],
        out_specs=pl.BlockSpec((MOE_TM, width), lambda i: (i, 0)),
        out_shape=jax.ShapeDtypeStruct((t, width), src.dtype),
        scratch_shapes=[pltpu.SemaphoreType.DMA(())],
        compiler_params=_params("arbitrary"),
        name="moe_gather",
    )(perm, src)


def _scatter_rows_kernel(perm_ref, src_ref, out_ref, sem):
    def start(r, carry):
        _row_dma(src_ref, out_ref, r, perm_ref[0, 0, r], sem).start()
        return carry

    lax.fori_loop(0, MOE_TM, start, 0)
    pltpu.make_async_copy(src_ref, out_ref.at[pl.ds(0, MOE_TM)], sem).wait()


def _scatter_rows(src, perm):
    t, width = src.shape
    return pl.pallas_call(
        _scatter_rows_kernel,
        grid=(t // MOE_TM,),
        in_specs=[pl.BlockSpec((1, 1, MOE_TM), lambda i: (i, 0, 0), memory_space=pltpu.SMEM),
                  pl.BlockSpec((MOE_TM, width), lambda i: (i, 0))],
        out_specs=pl.BlockSpec(memory_space=pl.ANY),
        out_shape=jax.ShapeDtypeStruct((t, width), src.dtype),
        scratch_shapes=[pltpu.SemaphoreType.DMA(())],
        compiler_params=_params("arbitrary"),
        name="moe_scatter",
    )(perm, src)


def _swiglu(xb, wg, wu):
    return jax.nn.silu(jnp.dot(xb, wg, preferred_element_type=F32)) * jnp.dot(xb, wu, preferred_element_type=F32)


def _moe_kernel(groups_ref, nact_ref, flags_ref, xg_ref, wg_ref, wu_ref, wd_ref, sg_ref, su_ref, sd_ref,
                g_ref, b_ref, o_ref, acc_ref, xb_ref):
    i = pl.program_id(0)
    j = pl.program_id(1)

    @pl.when(j == 0)
    def _():
        xb = xg_ref[:, 0:D_MODEL].astype(BF16)
        xb_ref[...] = xb
        h = _swiglu(xb, sg_ref[...], su_ref[...])
        acc_ref[...] = jnp.dot(h.astype(BF16), sd_ref[...], preferred_element_type=F32)

    @pl.when(j < nact_ref[i])
    def _():
        group = groups_ref[i * N_EXPERT_GROUPS + j]
        wd_all = wd_ref[...].reshape(EXPERTS_PER_GROUP * D_EXPERT, D_MODEL)
        for s in range(MOE_NSUB):
            @pl.when(flags_ref[(i * N_EXPERT_GROUPS + group) * MOE_NSUB + s] != 0)
            def _():
                rows = slice(s * MOE_SUB, (s + 1) * MOE_SUB)
                xb = xb_ref[rows, :]
                gates = xg_ref[rows, D_MODEL:]
                lane = lax.broadcasted_iota(jnp.int32, gates.shape, 1)
                hs = []
                for e in range(EXPERTS_PER_GROUP):
                    gate = jnp.sum(jnp.where(lane == group * EXPERTS_PER_GROUP + e, gates, 0.0),
                                   axis=1, keepdims=True)
                    hs.append((_swiglu(xb, wg_ref[e], wu_ref[e]) * gate).astype(BF16))
                acc_ref[rows, :] += jnp.dot(jnp.concatenate(hs, axis=1), wd_all, preferred_element_type=F32)

    @pl.when(j == pl.num_programs(1) - 1)
    def _():
        o_ref[...] = _layer_norm_rows(DEEPNORM_ALPHA * xg_ref[:, 0:D_MODEL] + acc_ref[...], g_ref[...], b_ref[...])


def _moe(xg_sorted, plan, wg, wu, wd, sg, su, sd, g, b):
    t = xg_sorted.shape[0]
    d, f, tm, ng = D_MODEL, D_EXPERT, MOE_TM, N_EXPERT_GROUPS
    _, groups, n_active, flags = plan
    row = lambda i, j, *_: (i, 0)
    fixed = lambda i, j, *_: (0, 0)
    expert_block = lambda i, j, groups_ref, *_: (groups_ref[i * ng + j], 0, 0)
    grid_spec = pltpu.PrefetchScalarGridSpec(
        num_scalar_prefetch=3,
        grid=(t // tm, ng),
        in_specs=[pl.BlockSpec((tm, XG_WIDTH), row),
                  pl.BlockSpec((EXPERTS_PER_GROUP, d, f), expert_block),
                  pl.BlockSpec((EXPERTS_PER_GROUP, d, f), expert_block),
                  pl.BlockSpec((EXPERTS_PER_GROUP, f, d), expert_block),
                  pl.BlockSpec((d, f), fixed), pl.BlockSpec((d, f), fixed), pl.BlockSpec((f, d), fixed),
                  pl.BlockSpec((1, d), fixed), pl.BlockSpec((1, d), fixed)],
        out_specs=pl.BlockSpec((tm, d), row),
        scratch_shapes=[pltpu.VMEM((tm, d), F32), pltpu.VMEM((tm, d), BF16)],
    )
    return pl.pallas_call(
        _moe_kernel,
        grid_spec=grid_spec,
        out_shape=jax.ShapeDtypeStruct((t, d), F32),
        compiler_params=_params("arbitrary", "arbitrary"),
        name="moe_ln2",
    )(groups, n_active, flags, xg_sorted, wg, wu, wd, sg, su, sd, g, b)


def _moe_layer(x, w_router_t, router_bias, wg, wu, wd, sg, su, sd, g, b):
    xg, code = _router(x, w_router_t, router_bias.reshape(N_EXPERTS, 1))
    plan = _dispatch_plan(code)
    perm = plan[0]
    y_sorted = _moe(_gather_rows(xg, perm), plan, wg, wu, wd, sg, su, sd, g, b)
    return _scatter_rows(y_sorted, perm)


def _rotate_half_columns(w):
    half = QK_ROPE // 2
    return jnp.concatenate([-w[..., half:], w[..., :half]], axis=-1)


def _prep_in_proj(w_in):
    layers, d, _ = w_in.shape
    cuts = np.cumsum((WIDTH_A, WIDTH_A, WIDTH_A, Q_LORA, KV_LORA, QK_ROPE, OUT_C))
    qa, ka, va, cq, ckv, kr, qc, gl = jnp.split(w_in, [int(c) for c in cuts], axis=-1)
    lead = jnp.zeros((layers, d, QK_NOPE), w_in.dtype)
    tail = jnp.zeros((layers, d, HEAD_PAD_B - QK_NOPE - QK_ROPE), w_in.dtype)
    kr_slot = jnp.concatenate([lead, kr, tail], axis=-1)
    kr_rot_slot = jnp.concatenate([lead, _rotate_half_columns(kr), tail], axis=-1)
    return jnp.concatenate([qa, ka, va, cq, ckv, kr_slot, kr_rot_slot, qc, gl], axis=-1).astype(BF16)


def _prep_mla_weights(w_q_up, w_kv_up):
    layers = w_q_up.shape[0]
    wq = w_q_up.reshape(layers, Q_LORA, HEADS_B, QK_NOPE + QK_ROPE)
    nope, rope = wq[..., :QK_NOPE], wq[..., QK_NOPE:]
    pad = HEAD_PAD_B - QK_NOPE - QK_ROPE
    zq = lambda n: jnp.zeros((layers, Q_LORA, HEADS_B, n), w_q_up.dtype)
    wqa = jnp.concatenate([nope, rope, zq(pad)], axis=-1)
    wqb = jnp.concatenate([zq(QK_NOPE), _rotate_half_columns(rope), zq(pad)], axis=-1)
    wkv = w_kv_up.reshape(layers, KV_LORA, HEADS_B, QK_NOPE + V_DIM_B)
    zk = lambda n: jnp.zeros((layers, KV_LORA, HEADS_B, n), w_kv_up.dtype)
    wk = jnp.concatenate([wkv[..., :QK_NOPE], zk(HEAD_PAD_B - QK_NOPE)], axis=-1)
    wv = jnp.concatenate([wkv[..., QK_NOPE:], zk(V_ROWS_B - V_DIM_B)], axis=-1)
    flat = lambda w: w.reshape(layers, w.shape[1], -1).astype(BF16)
    flat_t = lambda w: jnp.swapaxes(flat(w), 1, 2)
    return flat_t(wqa), flat_t(wqb), flat(wk), flat_t(wv)


def _trunk(x, mem, emb_g, emb_b, w, depth):
    batch, seq, d = x.shape
    n_mem = mem.shape[1]
    t = batch * seq
    row2 = lambda v: v.reshape(1, -1)
    xf = _embed_ln(x.reshape(t, d), row2(emb_g), row2(emb_b))
    mem_b = mem.reshape(batch * n_mem, d).astype(BF16)
    tables = _rope_tables(seq)
    for l in range(depth):
        qkv, mla, qc, gl = _project(xf, w["in_proj"][l])
        oa = _dilated_mixer(qkv, batch, seq)
        q_t, k, v_t = _mla_prep(mla, tables, row2(w["q_norm_g"][l]), w["wqa"][l], w["wqb"][l],
                                row2(w["kv_norm_g"][l]), w["wk"][l], w["wv"][l], batch, seq)
        ob = _mla_attention(q_t, k, v_t, batch, seq)
        oc = _mem_attention(qc, _mem_kv(mem_b, w["mem_kv"][l]), batch, seq, n_mem)
        xf = _merge(oa, ob, oc, gl, xf, w["branch"][l], w["out"][l], row2(w["ln1_g"][l]), row2(w["ln1_b"][l]))
        xf = _moe_layer(xf, w["router_t"][l], w["router_bias"][l], w["exp_gate"][l], w["exp_up"][l],
                        w["exp_down"][l], w["sh_gate"][l], w["sh_up"][l], w["sh_down"][l],
                        row2(w["ln2_g"][l]), row2(w["ln2_b"][l]))
    return xf.reshape(batch, seq, d)


def kernel(x_prompt, x_sample, mem_prompt, mem_sample, emb_ln_g, emb_ln_b, w_in, q_norm_g, w_q_up, kv_norm_g,
           w_kv_up, w_mem_kv, w_branch, w_out, ln1_g, ln1_b, w_router, router_bias, w_exp_gate, w_exp_up,
           w_exp_down, w_sh_gate, w_sh_up, w_sh_down, ln2_g, ln2_b):
    wqa, wqb, wk, wv = _prep_mla_weights(w_q_up, w_kv_up)
    w = {
        "in_proj": _prep_in_proj(w_in),
        "q_norm_g": q_norm_g, "kv_norm_g": kv_norm_g, "wqa": wqa, "wqb": wqb, "wk": wk, "wv": wv,
        "mem_kv": w_mem_kv.astype(BF16), "branch": w_branch.astype(BF16), "out": w_out.astype(BF16),
        "ln1_g": ln1_g, "ln1_b": ln1_b,
        "router_t": jnp.swapaxes(w_router, 1, 2).astype(BF16), "router_bias": router_bias,
        "exp_gate": w_exp_gate.astype(BF16), "exp_up": w_exp_up.astype(BF16), "exp_down": w_exp_down.astype(BF16),
        "sh_gate": w_sh_gate.astype(BF16), "sh_up": w_sh_up.astype(BF16), "sh_down": w_sh_down.astype(BF16),
        "ln2_g": ln2_g, "ln2_b": ln2_b,
    }
    depth = w_in.shape[0]
    y_prompt = _trunk(x_prompt, mem_prompt, emb_ln_g, emb_ln_b, w, depth)
    y_sample = _trunk(x_sample, mem_sample, emb_ln_g, emb_ln_b, w, depth)
    return (y_prompt, y_sample)
```

```python
import functools
import math

import numpy as np
import jax
import jax.numpy as jnp
from jax import lax
from jax.experimental import pallas as pl
from jax.experimental.pallas import tpu as pltpu

F32 = jnp.float32
BF16 = jnp.bfloat16

D_MODEL = 1024
DEPTH = 4
DIL_GROUPS = ((128, 1), (512, 4), (2048, 16))
N_GROUPS_A = 3
HEADS_A = 4
HEAD_DIM_A = 128
GROUP_WIDTH_A = HEADS_A * HEAD_DIM_A
WIDTH_A = N_GROUPS_A * GROUP_WIDTH_A
RADIUS_A = 64
HEADS_B = 8
Q_LORA = 256
KV_LORA = 128
QK_NOPE = 64
QK_ROPE = 32
V_DIM_B = 64
ROPE_THETA = 10000.0
HEAD_PAD_B = 128
HEADS_C = 4
HEAD_DIM_C = 128
OUT_C = HEADS_C * HEAD_DIM_C
N_BRANCH = 3
BRANCH_WIDTH = 512
N_EXPERTS = 64
TOP_K = 8
N_EXPERT_GROUPS = 8
EXPERTS_PER_GROUP = N_EXPERTS // N_EXPERT_GROUPS
TOPK_GROUPS = 4
D_EXPERT = 256
ROUTED_SCALE = 2.5
DEEPNORM_ALPHA = (2 * DEPTH) ** 0.25
LN_EPS = 1e-5
RMS_EPS = 1e-6

SEG_QKV = 3 * WIDTH_A
SEG_MLA = Q_LORA + KV_LORA + 2 * HEAD_PAD_B
SEG_QC = OUT_C
SEG_GL = N_BRANCH * D_MODEL
N_PROJ = SEG_QKV + SEG_MLA + SEG_QC + SEG_GL

NEG_BIG = -1e30
VMEM_LIMIT = 56 * 2 ** 20

NT_DIMS = (((1,), (1,)), ((), ()))


def _params(*sem):
    return pltpu.CompilerParams(dimension_semantics=sem, vmem_limit_bytes=VMEM_LIMIT)


def _resident(block_shape, index_map):
    return pl.BlockSpec(block_shape, index_map, pipeline_mode=pl.Buffered(1))


def _layer_norm_rows(h, g, b):
    mu = jnp.mean(h, axis=-1, keepdims=True)
    c = h - mu
    var = jnp.mean(c * c, axis=-1, keepdims=True)
    return c * lax.rsqrt(var + LN_EPS) * g + b


def _rms_norm_rows(h, g):
    return h * lax.rsqrt(jnp.mean(h * h, axis=-1, keepdims=True) + RMS_EPS) * g


def _embed_ln_kernel(x_ref, g_ref, b_ref, xf_ref):
    xf_ref[...] = _layer_norm_rows(x_ref[...], g_ref[...], b_ref[...])


def _embed_ln(x, g, b):
    t, d = x.shape
    tm = 512
    row = lambda i: (i, 0)
    fixed = lambda i: (0, 0)
    return pl.pallas_call(
        _embed_ln_kernel,
        grid=(t // tm,),
        in_specs=[pl.BlockSpec((tm, d), row), pl.BlockSpec((1, d), fixed), pl.BlockSpec((1, d), fixed)],
        out_specs=pl.BlockSpec((tm, d), row),
        out_shape=jax.ShapeDtypeStruct((t, d), F32),
        compiler_params=_params("parallel"),
        name="embed_ln",
    )(x, g, b)


PROJ_CHUNK = 512


def _proj_kernel(x_ref, w_ref, qkv_ref, mla_ref, qc_ref, gl_ref):
    xb = x_ref[...].astype(BF16)
    col = 0
    for ref, width in ((qkv_ref, SEG_QKV), (mla_ref, SEG_MLA), (qc_ref, SEG_QC), (gl_ref, SEG_GL)):
        for c in range(0, width, PROJ_CHUNK):
            w = min(PROJ_CHUNK, width - c)
            ref[:, c:c + w] = jnp.dot(xb, w_ref[:, col + c:col + c + w],
                                      preferred_element_type=F32).astype(BF16)
        col += width


def _project(x, w):
    t, d = x.shape
    tm = 512
    row = lambda i: (i, 0)
    widths = (SEG_QKV, SEG_MLA, SEG_QC, SEG_GL)
    return pl.pallas_call(
        _proj_kernel,
        grid=(t // tm,),
        in_specs=[pl.BlockSpec((tm, d), row), _resident((d, N_PROJ), lambda i: (0, 0))],
        out_specs=[pl.BlockSpec((tm, n), row) for n in widths],
        out_shape=[jax.ShapeDtypeStruct((t, n), BF16) for n in widths],
        compiler_params=_params("parallel"),
        name="in_proj",
    )(x, w)


BAND_TQ = 512
BAND_QB = 128
BAND_KB = BAND_QB + 2 * RADIUS_A


def _band_kernel(q_ref, kp_ref, km_ref, kn_ref, vp_ref, vm_ref, vn_ref, o_ref, lse_ref, k_scr, v_scr,
                 *, seq_len, slopes):
    i = pl.program_id(1)
    r = RADIUS_A
    k_scr[0:r, :] = kp_ref[...]
    k_scr[r:r + BAND_TQ, :] = km_ref[...]
    k_scr[r + BAND_TQ:, :] = kn_ref[...]
    v_scr[0:r, :] = vp_ref[...]
    v_scr[r:r + BAND_TQ, :] = vm_ref[...]
    v_scr[r + BAND_TQ:, :] = vn_ref[...]

    row = lax.broadcasted_iota(jnp.int32, (BAND_QB, BAND_KB), 0)
    col = lax.broadcasted_iota(jnp.int32, (BAND_QB, BAND_KB), 1)
    rel = col - r - row
    dist = jnp.abs(rel).astype(F32)
    in_band = jnp.abs(rel) <= r
    scale = HEAD_DIM_A ** -0.5

    for qb in range(BAND_TQ // BAND_QB):
        key_pos = i * BAND_TQ + qb * BAND_QB - r + col
        valid = in_band & (key_pos >= 0) & (key_pos < seq_len)
        for h in range(HEADS_A):
            lanes = slice(h * HEAD_DIM_A, (h + 1) * HEAD_DIM_A)
            q = q_ref[qb * BAND_QB:(qb + 1) * BAND_QB, lanes]
            k = k_scr[qb * BAND_QB:qb * BAND_QB + BAND_KB, lanes]
            v = v_scr[qb * BAND_QB:qb * BAND_QB + BAND_KB, lanes]
            s = lax.dot_general(q, k, NT_DIMS, preferred_element_type=F32)
            logits = jnp.where(valid, s * scale - slopes[h] * dist, NEG_BIG)
            m = jnp.max(logits, axis=1, keepdims=True)
            p = jnp.exp(logits - m)
            den = jnp.sum(p, axis=1, keepdims=True)
            o = jnp.dot(p.astype(BF16), v, preferred_element_type=F32) / den
            rows = slice(qb * BAND_QB, (qb + 1) * BAND_QB)
            o_ref[rows, lanes] = o.astype(BF16)
            lse_ref[rows, lanes] = jnp.broadcast_to(m + jnp.log(den), (BAND_QB, HEAD_DIM_A))


def _band_attention(q_src, k_src, v_src, n_seq, seq_len, slopes):
    tq, r = BAND_TQ, RADIUS_A
    assert seq_len % tq == 0 and tq % r == 0
    steps = seq_len // tq
    halo_per_tile = tq // r
    halo_blocks = seq_len // r

    def main_map(cb):
        return lambda n, i: (n * steps + i, cb)

    def prev_map(cb):
        return lambda n, i: (n * halo_blocks + jnp.maximum(i * halo_per_tile - 1, 0), cb)

    def next_map(cb):
        return lambda n, i: (n * halo_blocks + jnp.minimum((i + 1) * halo_per_tile, halo_blocks - 1), cb)

    (qa, qcb), (ka, kcb), (va, vcb) = q_src, k_src, v_src
    w = GROUP_WIDTH_A
    rows = n_seq * seq_len
    return pl.pallas_call(
        functools.partial(_band_kernel, seq_len=seq_len, slopes=slopes),
        grid=(n_seq, steps),
        in_specs=[
            pl.BlockSpec((tq, w), main_map(qcb)),
            pl.BlockSpec((r, w), prev_map(kcb)), pl.BlockSpec((tq, w), main_map(kcb)),
            pl.BlockSpec((r, w), next_map(kcb)),
            pl.BlockSpec((r, w), prev_map(vcb)), pl.BlockSpec((tq, w), main_map(vcb)),
            pl.BlockSpec((r, w), next_map(vcb)),
        ],
        out_specs=[pl.BlockSpec((tq, w), main_map(0)), pl.BlockSpec((tq, w), main_map(0))],
        out_shape=[jax.ShapeDtypeStruct((rows, w), BF16), jax.ShapeDtypeStruct((rows, w), F32)],
        scratch_shapes=[pltpu.VMEM((tq + 2 * r, w), BF16), pltpu.VMEM((tq + 2 * r, w), BF16)],
        compiler_params=_params("parallel", "parallel"),
        name="band_attention",
    )(qa, ka, ka, ka, va, va, va)


def _alibi_slopes():
    n = N_GROUPS_A * HEADS_A
    return [2.0 ** (-8.0 * (i + 1) / n) for i in range(n)]


def _dilated_mixer(qkv, batch, seq):
    slopes = _alibi_slopes()
    t = batch * seq
    outs = []
    for g, (_, d) in enumerate(DIL_GROUPS):
        group_slopes = tuple(float(s * d) for s in slopes[g * HEADS_A:(g + 1) * HEADS_A])
        if d == 1:
            o, lse = _band_attention((qkv, g), (qkv, N_GROUPS_A + g), (qkv, 2 * N_GROUPS_A + g),
                                     batch, seq, group_slopes)
        else:
            length = seq // d

            def split(m):
                c0 = (m * N_GROUPS_A + g) * GROUP_WIDTH_A
                x = qkv[:, c0:c0 + GROUP_WIDTH_A].reshape(batch, length, d, GROUP_WIDTH_A)
                return x.transpose(0, 2, 1, 3).reshape(t, GROUP_WIDTH_A)

            o, lse = _band_attention((split(0), 0), (split(1), 0), (split(2), 0), batch * d, length, group_slopes)

            def merge(x):
                return x.reshape(batch, d, length, GROUP_WIDTH_A).transpose(0, 2, 1, 3).reshape(t, GROUP_WIDTH_A)

            o, lse = merge(o), merge(lse)
        outs.append((o, lse))
    return outs


V_ROWS_B = 80
MLA_TQ = 1024
MLA_TK = 2048
MLA_KS = 256
MLA_LOOKAHEAD = 1
MLA_GAP_LIMIT = 64.0
MLA_HEADS_PER_STEP = 2


def _mla_prep_kernel(mla_ref, cos_ref, sin_ref, cos_t_ref, sin_t_ref, gq_ref, wqa_ref, wqb_ref, gkv_ref,
                     wk_ref, wv_ref, qt_ref, k_ref, vt_ref):
    m = mla_ref[...]
    cq = m[:, 0:Q_LORA].astype(F32)
    ckv = m[:, Q_LORA:Q_LORA + KV_LORA].astype(F32)
    kr = m[:, Q_LORA + KV_LORA:Q_LORA + KV_LORA + HEAD_PAD_B].astype(F32)
    kr_rot = m[:, Q_LORA + KV_LORA + HEAD_PAD_B:].astype(F32)
    scale = (QK_NOPE + QK_ROPE) ** -0.5 * math.log2(math.e)

    cqn = _rms_norm_rows(cq, gq_ref[...]).astype(BF16)
    qa_t = lax.dot_general(wqa_ref[...], cqn, NT_DIMS, preferred_element_type=F32)
    qb_t = lax.dot_general(wqb_ref[...], cqn, NT_DIMS, preferred_element_type=F32)
    ckvn = _rms_norm_rows(ckv, gkv_ref[...]).astype(BF16)
    kn = jnp.dot(ckvn, wk_ref[...], preferred_element_type=F32)
    v_t = lax.dot_general(wv_ref[...], ckvn, NT_DIMS, preferred_element_type=F32)
    k_rope = kr * cos_ref[...] + kr_rot * sin_ref[...]
    cos_t, sin_t = cos_t_ref[...], sin_t_ref[...]
    row = lax.broadcasted_iota(jnp.int32, (V_ROWS_B, 1), 0)
    ones_row = (row == V_DIM_B).astype(F32)
    for h in range(HEADS_B):
        slot = slice(h * HEAD_PAD_B, (h + 1) * HEAD_PAD_B)
        qt_ref[slot, :] = ((qa_t[slot, :] * cos_t + qb_t[slot, :] * sin_t) * scale).astype(BF16)
        k_ref[:, slot] = (kn[:, slot] + k_rope).astype(BF16)
        vt_ref[0, h, 0] = (v_t[h * V_ROWS_B:(h + 1) * V_ROWS_B, :] + ones_row).astype(BF16)


def _mla_prep(mla, tables, gq, wqa_t, wqb_t, gkv, wk, wv_t, batch, seq):
    t = mla.shape[0]
    tm = MLA_TK
    chunks = seq // tm
    cos, sin, cos_t, sin_t = tables
    row = lambda i: (i, 0)
    pos = lambda i: (i % chunks, 0)
    pos_t = lambda i: (0, i % chunks)
    fixed = lambda i: (0, 0)
    wide = HEADS_B * HEAD_PAD_B
    return pl.pallas_call(
        _mla_prep_kernel,
        grid=(t // tm,),
        in_specs=[pl.BlockSpec((tm, SEG_MLA), row),
                  pl.BlockSpec((tm, HEAD_PAD_B), pos), pl.BlockSpec((tm, HEAD_PAD_B), pos),
                  pl.BlockSpec((HEAD_PAD_B, tm), pos_t), pl.BlockSpec((HEAD_PAD_B, tm), pos_t),
                  pl.BlockSpec((1, Q_LORA), fixed), pl.BlockSpec((wide, Q_LORA), fixed),
                  pl.BlockSpec((wide, Q_LORA), fixed),
                  pl.BlockSpec((1, KV_LORA), fixed), pl.BlockSpec((KV_LORA, wide), fixed),
                  pl.BlockSpec((HEADS_B * V_ROWS_B, KV_LORA), fixed)],
        out_specs=[pl.BlockSpec((wide, tm), lambda i: (i // chunks, i % chunks)),
                   pl.BlockSpec((tm, wide), row),
                   pl.BlockSpec((1, HEADS_B, 1, V_ROWS_B, tm), lambda i: (i // chunks, 0, i % chunks, 0, 0))],
        out_shape=[jax.ShapeDtypeStruct((batch * wide, seq), BF16),
                   jax.ShapeDtypeStruct((t, wide), BF16),
                   jax.ShapeDtypeStruct((batch, HEADS_B, chunks, V_ROWS_B, tm), BF16)],
        compiler_params=_params("parallel"),
        name="mla_prep",
    )(mla, cos, sin, cos_t, sin_t, gq, wqa_t, wqb_t, gkv, wk, wv_t)


def _mla_attn_kernel(qt_ref, k_ref, vt_ref, o_ref, *, seq):
    n_sub = MLA_TK // MLA_KS
    units = [(c, hh) for c in range(n_sub) for hh in range(MLA_HEADS_PER_STEP)]

    def scores(rows, hh):
        slot = slice(hh * HEAD_PAD_B, (hh + 1) * HEAD_PAD_B)
        return jnp.dot(k_ref[rows, slot], qt_ref[slot, :], preferred_element_type=F32)

    def sweep(kc, carry, update):
        def unit_scores(u):
            c, hh = units[u]
            return scores(pl.ds(pl.multiple_of(kc * MLA_TK + c * MLA_KS, MLA_KS), MLA_KS), hh)

        new = list(carry)
        pending = [unit_scores(u) for u in range(min(MLA_LOOKAHEAD, len(units)))]
        worst = None
        for u, (c, hh) in enumerate(units):
            if u + MLA_LOOKAHEAD < len(units):
                pending.append(unit_scores(u + MLA_LOOKAHEAD))
            v_blk = vt_ref[0, hh, kc, :, c * MLA_KS:(c + 1) * MLA_KS]
            new[hh], gap = update(new[hh], pending.pop(0), v_blk)
            worst = gap if worst is None else jnp.maximum(worst, gap)
        return tuple(new), worst

    def exact_update(state, s, v_blk):
        m, acc = state
        m_new = jnp.maximum(m, jnp.max(s, axis=0, keepdims=True))
        p = jnp.exp2(s - m_new).astype(BF16)
        acc = jnp.exp2(m - m_new) * acc + jnp.dot(v_blk, p, preferred_element_type=F32)
        return (m_new, acc), jnp.zeros_like(m)

    def lagged_update(state, s, v_blk):
        m, acc = state
        p = jnp.exp2(s - m).astype(BF16)
        col_max = jnp.max(s, axis=0, keepdims=True)
        gap = col_max - m
        m_new = jnp.maximum(m, col_max)
        acc = (acc + jnp.dot(v_blk, p, preferred_element_type=F32)) * jnp.exp2(m - m_new)
        return (m_new, acc), gap

    def step(kc, carry):
        fast, worst = sweep(kc, carry, lagged_update)
        overflow_risk = jnp.max(worst) > MLA_GAP_LIMIT
        return lax.cond(overflow_risk, lambda: sweep(kc, carry, exact_update)[0], lambda: fast)

    first = pl.ds(0, MLA_KS)
    init = tuple((jnp.max(scores(first, hh), axis=0, keepdims=True), jnp.zeros((V_ROWS_B, MLA_TQ), F32))
                 for hh in range(MLA_HEADS_PER_STEP))
    final = lax.fori_loop(0, seq // MLA_TK, step, init)
    heads = [acc[0:V_DIM_B, :] / acc[V_DIM_B:V_DIM_B + 1, :] for _, acc in final]
    o_ref[...] = jnp.concatenate(heads, axis=0).T.astype(BF16)


def _mla_attention(q_t, k, v_t, batch, seq):
    t = batch * seq
    tq = MLA_TQ
    steps = seq // tq
    chunks = seq // MLA_TK
    pairs = HEADS_B // MLA_HEADS_PER_STEP
    pair = MLA_HEADS_PER_STEP * HEAD_PAD_B
    return pl.pallas_call(
        functools.partial(_mla_attn_kernel, seq=seq),
        grid=(batch, pairs, steps),
        in_specs=[pl.BlockSpec((pair, tq), lambda b, hp, i: (b * pairs + hp, i)),
                  _resident((seq, pair), lambda b, hp, i: (b, hp)),
                  _resident((1, MLA_HEADS_PER_STEP, chunks, V_ROWS_B, MLA_TK), lambda b, hp, i: (b, hp, 0, 0, 0))],
        out_specs=pl.BlockSpec((tq, MLA_HEADS_PER_STEP * V_DIM_B), lambda b, hp, i: (b * steps + i, hp)),
        out_shape=jax.ShapeDtypeStruct((t, HEADS_B * V_DIM_B), BF16),
        compiler_params=_params("parallel", "parallel", "arbitrary"),
        name="mla_attention",
    )(q_t, k, v_t)


def _rope_tables(seq):
    inv_freq = 1.0 / (ROPE_THETA ** (jnp.arange(0, QK_ROPE, 2, dtype=F32) / QK_ROPE))
    ang = jnp.arange(seq, dtype=F32)[:, None] * inv_freq[None, :]
    cos, sin = jnp.cos(ang), jnp.sin(ang)
    pad = HEAD_PAD_B - QK_NOPE - QK_ROPE
    cos_s = jnp.concatenate([jnp.ones((seq, QK_NOPE), F32), cos, cos, jnp.zeros((seq, pad), F32)], axis=1)
    sin_s = jnp.concatenate([jnp.zeros((seq, QK_NOPE), F32), sin, sin, jnp.zeros((seq, pad), F32)], axis=1)
    return cos_s, sin_s, cos_s.T, sin_s.T


def _mem_kv_kernel(mem_ref, w_ref, o_ref):
    o_ref[...] = jnp.dot(mem_ref[...], w_ref[...], preferred_element_type=F32).astype(BF16)


def _mem_kv(mem_b, w):
    rows, d = mem_b.shape
    n = w.shape[1]
    tm = 256
    return pl.pallas_call(
        _mem_kv_kernel,
        grid=(rows // tm,),
        in_specs=[pl.BlockSpec((tm, d), lambda i: (i, 0)), pl.BlockSpec((d, n), lambda i: (0, 0))],
        out_specs=pl.BlockSpec((tm, n), lambda i: (i, 0)),
        out_shape=jax.ShapeDtypeStruct((rows, n), BF16),
        compiler_params=_params("parallel"),
        name="mem_kv",
    )(mem_b, w)


def _mem_attn_kernel(q_ref, kv_ref, o_ref):
    scale = HEAD_DIM_C ** -0.5
    for h in range(HEADS_C):
        lanes = slice(h * HEAD_DIM_C, (h + 1) * HEAD_DIM_C)
        k = kv_ref[:, lanes]
        v = kv_ref[:, OUT_C + h * HEAD_DIM_C:OUT_C + (h + 1) * HEAD_DIM_C]
        s = lax.dot_general(q_ref[:, lanes], k, NT_DIMS, preferred_element_type=F32) * scale
        m = jnp.max(s, axis=1, keepdims=True)
        p = jnp.exp(s - m)
        den = jnp.sum(p, axis=1, keepdims=True)
        o_ref[:, lanes] = (jnp.dot(p.astype(BF16), v, preferred_element_type=F32) / den).astype(BF16)


def _mem_attention(qc, kv, batch, seq, n_mem):
    t = batch * seq
    ts = 1024
    steps = seq // ts
    return pl.pallas_call(
        _mem_attn_kernel,
        grid=(batch, steps),
        in_specs=[pl.BlockSpec((ts, OUT_C), lambda b, i: (b * steps + i, 0)),
                  pl.BlockSpec((n_mem, 2 * OUT_C), lambda b, i: (b, 0))],
        out_specs=pl.BlockSpec((ts, OUT_C), lambda b, i: (b * steps + i, 0)),
        out_shape=jax.ShapeDtypeStruct((t, OUT_C), BF16),
        compiler_params=_params("parallel", "parallel"),
        name="mem_attention",
    )(qc, kv)


def _merge_kernel(oa0_ref, oa1_ref, oa2_ref, l0_ref, l1_ref, l2_ref, ob_ref, oc_ref, gl_ref, x_ref,
                  wb_ref, wo_ref, g_ref, b_ref, xf_ref):
    l0, l1, l2 = l0_ref[...], l1_ref[...], l2_ref[...]
    m = jnp.maximum(jnp.maximum(l0, l1), l2)
    e0, e1, e2 = jnp.exp(l0 - m), jnp.exp(l1 - m), jnp.exp(l2 - m)
    oa = (e0 * oa0_ref[...].astype(F32) + e1 * oa1_ref[...].astype(F32) + e2 * oa2_ref[...].astype(F32))
    oa = (oa / (e0 + e1 + e2)).astype(BF16)
    z = None
    for i, o in enumerate((oa, ob_ref[...], oc_ref[...])):
        gate = jax.nn.sigmoid(gl_ref[:, i * D_MODEL:(i + 1) * D_MODEL].astype(F32))
        term = gate * jnp.dot(o, wb_ref[i], preferred_element_type=F32)
        z = term if z is None else z + term
    y = jnp.dot(z.astype(BF16), wo_ref[...], preferred_element_type=F32)
    xf_ref[...] = _layer_norm_rows(DEEPNORM_ALPHA * x_ref[...] + y, g_ref[...], b_ref[...])


def _merge(oa, ob, oc, gl, x, wb, wo, g, b):
    t, d = x.shape
    tm = 512
    row = lambda i: (i, 0)
    fixed = lambda i: (0, 0)
    half = pl.BlockSpec((tm, BRANCH_WIDTH), row)
    (oa0, l0), (oa1, l1), (oa2, l2) = oa
    return pl.pallas_call(
        _merge_kernel,
        grid=(t // tm,),
        in_specs=[half] * 8 + [pl.BlockSpec((tm, SEG_GL), row), pl.BlockSpec((tm, d), row),
                               _resident((N_BRANCH, BRANCH_WIDTH, d), lambda i: (0, 0, 0)),
                               _resident((d, d), fixed),
                               pl.BlockSpec((1, d), fixed), pl.BlockSpec((1, d), fixed)],
        out_specs=pl.BlockSpec((tm, d), row),
        out_shape=jax.ShapeDtypeStruct((t, d), F32),
        compiler_params=_params("parallel"),
        name="merge_ln1",
    )(oa0, oa1, oa2, l0, l1, l2, ob, oc, gl, x, wb, wo, g, b)


GATE_LANES = 128


def _first_index_of_max(vals, idx, axis, sentinel):
    mx = jnp.max(vals, axis=axis, keepdims=True)
    return jnp.min(jnp.where(vals == mx, idx, sentinel), axis=axis, keepdims=True)


def _router_kernel(x_ref, w_ref, bias_ref, xg_ref, code_ref):
    tm = x_ref.shape[0]
    x = x_ref[...]
    logits = lax.dot_general(w_ref[...], x.astype(BF16), NT_DIMS, preferred_element_type=F32)
    scores = jax.nn.sigmoid(logits)
    choice = scores + bias_ref[...]
    neg = -jnp.inf

    c3 = choice.reshape(N_EXPERT_GROUPS, EXPERTS_PER_GROUP, tm)
    e_idx = lax.broadcasted_iota(jnp.int32, c3.shape, 1)
    first = jnp.max(c3, axis=1, keepdims=True)
    first_at = jnp.min(jnp.where(c3 == first, e_idx, EXPERTS_PER_GROUP), axis=1, keepdims=True)
    second = jnp.max(jnp.where(e_idx == first_at, neg, c3), axis=1, keepdims=True)
    group_score = (first + second).reshape(N_EXPERT_GROUPS, tm)

    g_idx = lax.broadcasted_iota(jnp.int32, group_score.shape, 0)
    group_sel = jnp.zeros(group_score.shape, jnp.bool_)
    for _ in range(TOPK_GROUPS):
        at = _first_index_of_max(group_score, g_idx, 0, N_EXPERT_GROUPS)
        hit = g_idx == at
        group_sel = group_sel | hit
        group_score = jnp.where(hit, neg, group_score)

    allowed = jnp.broadcast_to(group_sel.reshape(N_EXPERT_GROUPS, 1, tm), c3.shape).reshape(N_EXPERTS, tm)
    cand = jnp.where(allowed, choice, neg)
    x_idx = lax.broadcasted_iota(jnp.int32, cand.shape, 0)
    chosen = jnp.zeros(cand.shape, jnp.bool_)
    for _ in range(TOP_K):
        at = _first_index_of_max(cand, x_idx, 0, N_EXPERTS)
        hit = x_idx == at
        chosen = chosen | hit
        cand = jnp.where(hit, neg, cand)

    w_sel = jnp.where(chosen, scores, 0.0)
    gates = w_sel / jnp.sum(w_sel, axis=0, keepdims=True) * ROUTED_SCALE
    padded = jnp.concatenate([gates, jnp.zeros((GATE_LANES - N_EXPERTS, tm), F32)], axis=0)
    xg_ref[:, 0:D_MODEL] = x
    xg_ref[:, D_MODEL:] = padded.T
    bit = jnp.left_shift(1, g_idx)
    code_ref[...] = jnp.sum(jnp.where(group_sel, bit, 0), axis=0, keepdims=True)


def _router(x, w_t, bias):
    t, d = x.shape
    tm = 1024
    return pl.pallas_call(
        _router_kernel,
        grid=(t // tm,),
        in_specs=[pl.BlockSpec((tm, d), lambda i: (i, 0)), pl.BlockSpec((N_EXPERTS, d), lambda i: (0, 0)),
                  pl.BlockSpec((N_EXPERTS, 1), lambda i: (0, 0))],
        out_specs=[pl.BlockSpec((tm, XG_WIDTH), lambda i: (i, 0)), pl.BlockSpec((1, tm), lambda i: (0, i))],
        out_shape=[jax.ShapeDtypeStruct((t, XG_WIDTH), F32), jax.ShapeDtypeStruct((1, t), jnp.int32)],
        compiler_params=_params("parallel"),
        name="router",
    )(x, w_t, bias)


XG_WIDTH = D_MODEL + GATE_LANES
MOE_TM = 1024
MOE_SUB = 128
MOE_NSUB = MOE_TM // MOE_SUB
TOKEN_BITS = 16


def _dispatch_plan(code):
    t = code.shape[1]
    tiles = t // MOE_TM
    assert t <= 1 << TOKEN_BITS
    key = jnp.sort(code[0] * (1 << TOKEN_BITS) + jnp.arange(t, dtype=jnp.int32))
    perm = key & ((1 << TOKEN_BITS) - 1)
    bits = ((key >> TOKEN_BITS)[:, None] >> jnp.arange(N_EXPERT_GROUPS, dtype=jnp.int32)[None, :]) & 1
    sub = jnp.max(bits.reshape(tiles, MOE_NSUB, MOE_SUB, N_EXPERT_GROUPS), axis=2)
    tile_active = jnp.max(sub, axis=1)
    n_active = jnp.sum(tile_active, axis=1).astype(jnp.int32)
    order = jnp.argsort(1 - tile_active, axis=1, stable=True).astype(jnp.int32)
    step = jnp.arange(N_EXPERT_GROUPS, dtype=jnp.int32)[None, :]
    last = jnp.take_along_axis(order, jnp.maximum(n_active - 1, 0)[:, None], axis=1)
    groups = jnp.where(step < n_active[:, None], order, last)
    flags = jnp.transpose(sub, (0, 2, 1)).astype(jnp.int32)
    return perm.reshape(tiles, 1, MOE_TM), groups.reshape(-1), n_active, flags.reshape(-1)


def _row_dma(src_ref, dst_ref, src_row, dst_row, sem):
    return pltpu.make_async_copy(src_ref.at[pl.ds(src_row, 1)], dst_ref.at[pl.ds(dst_row, 1)], sem)


DMA_THREADS = 2


def _gather_rows_kernel(perm_ref, src_ref, out_ref, sem):
    def start(q, carry):
        for u in range(DMA_THREADS):
            r = q * DMA_THREADS + u
            _row_dma(src_ref, out_ref, perm_ref[0, 0, r], r, sem).start(priority=u)
        return carry

    lax.fori_loop(0, MOE_TM // DMA_THREADS, start, 0)
    pltpu.make_async_copy(src_ref.at[pl.ds(0, MOE_TM)], out_ref, sem).wait()


def _gather_rows(src, perm):
    t, width = src.shape
    return pl.pallas_call(
        _gather_rows_kernel,
        grid=(t // MOE_TM,),
        in_specs=[pl.BlockSpec((1, 1, MOE_TM), lambda i: (i, 0, 0), memory_space=pltpu.SMEM),
                  pl.BlockSpec(memory_space=pl.ANY)],
        out_specs=pl.BlockSpec((MOE_TM, width), lambda i: (i, 0)),
        out_shape=jax.ShapeDtypeStruct((t, width), src.dtype),
        scratch_shapes=[pltpu.SemaphoreType.DMA(())],
        compiler_params=_params("arbitrary"),
        name="moe_gather",
    )(perm, src)


def _scatter_rows_kernel(perm_ref, src_ref, out_ref, sem):
    def start(q, carry):
        for u in range(DMA_THREADS):
            r = q * DMA_THREADS + u
            _row_dma(src_ref, out_ref, r, perm_ref[0, 0, r], sem).start(priority=u)
        return carry

    lax.fori_loop(0, MOE_TM // DMA_THREADS, start, 0)
    pltpu.make_async_copy(src_ref, out_ref.at[pl.ds(0, MOE_TM)], sem).wait()


def _scatter_rows(src, perm):
    t, width = src.shape
    return pl.pallas_call(
        _scatter_rows_kernel,
        grid=(t // MOE_TM,),
        in_specs=[pl.BlockSpec((1, 1, MOE_TM), lambda i: (i, 0, 0), memory_space=pltpu.SMEM),
                  pl.BlockSpec((MOE_TM, width), lambda i: (i, 0))],
        out_specs=pl.BlockSpec(memory_space=pl.ANY),
        out_shape=jax.ShapeDtypeStruct((t, width), src.dtype),
        scratch_shapes=[pltpu.SemaphoreType.DMA(())],
        compiler_params=_params("arbitrary"),
        name="moe_scatter",
    )(perm, src)


def _swiglu(xb, wg, wu):
    return jax.nn.silu(jnp.dot(xb, wg, preferred_element_type=F32)) * jnp.dot(xb, wu, preferred_element_type=F32)


def _moe_kernel(groups_ref, nact_ref, flags_ref, xg_ref, wg_ref, wu_ref, wd_ref, sg_ref, su_ref, sd_ref,
                g_ref, b_ref, o_ref, acc_ref, xb_ref):
    i = pl.program_id(0)
    j = pl.program_id(1)

    @pl.when(j == 0)
    def _():
        xb = xg_ref[:, 0:D_MODEL].astype(BF16)
        xb_ref[...] = xb
        h = _swiglu(xb, sg_ref[...], su_ref[...])
        acc_ref[...] = jnp.dot(h.astype(BF16), sd_ref[...], preferred_element_type=F32)

    @pl.when(j < nact_ref[i])
    def _():
        group = groups_ref[i * N_EXPERT_GROUPS + j]
        wd_all = wd_ref[...].reshape(EXPERTS_PER_GROUP * D_EXPERT, D_MODEL)
        for s in range(MOE_NSUB):
            @pl.when(flags_ref[(i * N_EXPERT_GROUPS + group) * MOE_NSUB + s] != 0)
            def _():
                rows = slice(s * MOE_SUB, (s + 1) * MOE_SUB)
                xb = xb_ref[rows, :]
                gates = xg_ref[rows, D_MODEL:]
                lane = lax.broadcasted_iota(jnp.int32, gates.shape, 1)
                hs = []
                for e in range(EXPERTS_PER_GROUP):
                    gate = jnp.sum(jnp.where(lane == group * EXPERTS_PER_GROUP + e, gates, 0.0),
                                   axis=1, keepdims=True)
                    hs.append((_swiglu(xb, wg_ref[e], wu_ref[e]) * gate).astype(BF16))
                acc_ref[rows, :] += jnp.dot(jnp.concatenate(hs, axis=1), wd_all, preferred_element_type=F32)

    @pl.when(j == pl.num_programs(1) - 1)
    def _():
        o_ref[...] = _layer_norm_rows(DEEPNORM_ALPHA * xg_ref[:, 0:D_MODEL] + acc_ref[...], g_ref[...], b_ref[...])


def _moe(xg_sorted, plan, wg, wu, wd, sg, su, sd, g, b):
    t = xg_sorted.shape[0]
    d, f, tm, ng = D_MODEL, D_EXPERT, MOE_TM, N_EXPERT_GROUPS
    _, groups, n_active, flags = plan
    row = lambda i, j, *_: (i, 0)
    fixed = lambda i, j, *_: (0, 0)
    expert_block = lambda i, j, groups_ref, *_: (groups_ref[i * ng + j], 0, 0)
    grid_spec = pltpu.PrefetchScalarGridSpec(
        num_scalar_prefetch=3,
        grid=(t // tm, ng),
        in_specs=[pl.BlockSpec((tm, XG_WIDTH), row),
                  pl.BlockSpec((EXPERTS_PER_GROUP, d, f), expert_block),
                  pl.BlockSpec((EXPERTS_PER_GROUP, d, f), expert_block),
                  pl.BlockSpec((EXPERTS_PER_GROUP, f, d), expert_block),
                  pl.BlockSpec((d, f), fixed), pl.BlockSpec((d, f), fixed), pl.BlockSpec((f, d), fixed),
                  pl.BlockSpec((1, d), fixed), pl.BlockSpec((1, d), fixed)],
        out_specs=pl.BlockSpec((tm, d), row),
        scratch_shapes=[pltpu.VMEM((tm, d), F32), pltpu.VMEM((tm, d), BF16)],
    )
    return pl.pallas_call(
        _moe_kernel,
        grid_spec=grid_spec,
        out_shape=jax.ShapeDtypeStruct((t, d), F32),
        compiler_params=_params("arbitrary", "arbitrary"),
        name="moe_ln2",
    )(groups, n_active, flags, xg_sorted, wg, wu, wd, sg, su, sd, g, b)


def _moe_layer(x, w_router_t, router_bias, wg, wu, wd, sg, su, sd, g, b):
    xg, code = _router(x, w_router_t, router_bias.reshape(N_EXPERTS, 1))
    plan = _dispatch_plan(code)
    perm = plan[0]
    y_sorted = _moe(_gather_rows(xg, perm), plan, wg, wu, wd, sg, su, sd, g, b)
    return _scatter_rows(y_sorted, perm)


def _rotate_half_columns(w):
    half = QK_ROPE // 2
    return jnp.concatenate([-w[..., half:], w[..., :half]], axis=-1)


def _prep_in_proj(w_in):
    layers, d, _ = w_in.shape
    cuts = np.cumsum((WIDTH_A, WIDTH_A, WIDTH_A, Q_LORA, KV_LORA, QK_ROPE, OUT_C))
    qa, ka, va, cq, ckv, kr, qc, gl = jnp.split(w_in, [int(c) for c in cuts], axis=-1)
    lead = jnp.zeros((layers, d, QK_NOPE), w_in.dtype)
    tail = jnp.zeros((layers, d, HEAD_PAD_B - QK_NOPE - QK_ROPE), w_in.dtype)
    kr_slot = jnp.concatenate([lead, kr, tail], axis=-1)
    kr_rot_slot = jnp.concatenate([lead, _rotate_half_columns(kr), tail], axis=-1)
    return jnp.concatenate([qa, ka, va, cq, ckv, kr_slot, kr_rot_slot, qc, gl], axis=-1).astype(BF16)


def _prep_mla_weights(w_q_up, w_kv_up):
    layers = w_q_up.shape[0]
    wq = w_q_up.reshape(layers, Q_LORA, HEADS_B, QK_NOPE + QK_ROPE)
    nope, rope = wq[..., :QK_NOPE], wq[..., QK_NOPE:]
    pad = HEAD_PAD_B - QK_NOPE - QK_ROPE
    zq = lambda n: jnp.zeros((layers, Q_LORA, HEADS_B, n), w_q_up.dtype)
    wqa = jnp.concatenate([nope, rope, zq(pad)], axis=-1)
    wqb = jnp.concatenate([zq(QK_NOPE), _rotate_half_columns(rope), zq(pad)], axis=-1)
    wkv = w_kv_up.reshape(layers, KV_LORA, HEADS_B, QK_NOPE + V_DIM_B)
    zk = lambda n: jnp.zeros((layers, KV_LORA, HEADS_B, n), w_kv_up.dtype)
    wk = jnp.concatenate([wkv[..., :QK_NOPE], zk(HEAD_PAD_B - QK_NOPE)], axis=-1)
    wv = jnp.concatenate([wkv[..., QK_NOPE:], zk(V_ROWS_B - V_DIM_B)], axis=-1)
    flat = lambda w: w.reshape(layers, w.shape[1], -1).astype(BF16)
    flat_t = lambda w: jnp.swapaxes(flat(w), 1, 2)
    return flat_t(wqa), flat_t(wqb), flat(wk), flat_t(wv)


def _trunk(x, mem, emb_g, emb_b, w, depth):
    batch, seq, d = x.shape
    n_mem = mem.shape[1]
    t = batch * seq
    row2 = lambda v: v.reshape(1, -1)
    xf = _embed_ln(x.reshape(t, d), row2(emb_g), row2(emb_b))
    mem_b = mem.reshape(batch * n_mem, d).astype(BF16)
    tables = _rope_tables(seq)
    for l in range(depth):
        qkv, mla, qc, gl = _project(xf, w["in_proj"][l])
        oa = _dilated_mixer(qkv, batch, seq)
        q_t, k, v_t = _mla_prep(mla, tables, row2(w["q_norm_g"][l]), w["wqa"][l], w["wqb"][l],
                                row2(w["kv_norm_g"][l]), w["wk"][l], w["wv"][l], batch, seq)
        ob = _mla_attention(q_t, k, v_t, batch, seq)
        oc = _mem_attention(qc, _mem_kv(mem_b, w["mem_kv"][l]), batch, seq, n_mem)
        xf = _merge(oa, ob, oc, gl, xf, w["branch"][l], w["out"][l], row2(w["ln1_g"][l]), row2(w["ln1_b"][l]))
        xf = _moe_layer(xf, w["router_t"][l], w["router_bias"][l], w["exp_gate"][l], w["exp_up"][l],
                        w["exp_down"][l], w["sh_gate"][l], w["sh_up"][l], w["sh_down"][l],
                        row2(w["ln2_g"][l]), row2(w["ln2_b"][l]))
    return xf.reshape(batch, seq, d)


def kernel(x_prompt, x_sample, mem_prompt, mem_sample, emb_ln_g, emb_ln_b, w_in, q_norm_g, w_q_up, kv_norm_g,
           w_kv_up, w_mem_kv, w_branch, w_out, ln1_g, ln1_b, w_router, router_bias, w_exp_gate, w_exp_up,
           w_exp_down, w_sh_gate, w_sh_up, w_sh_down, ln2_g, ln2_b):
    wqa, wqb, wk, wv = _prep_mla_weights(w_q_up, w_kv_up)
    w = {
        "in_proj": _prep_in_proj(w_in),
        "q_norm_g": q_norm_g, "kv_norm_g": kv_norm_g, "wqa": wqa, "wqb": wqb, "wk": wk, "wv": wv,
        "mem_kv": w_mem_kv.astype(BF16), "branch": w_branch.astype(BF16), "out": w_out.astype(BF16),
        "ln1_g": ln1_g, "ln1_b": ln1_b,
        "router_t": jnp.swapaxes(w_router, 1, 2).astype(BF16), "router_bias": router_bias,
        "exp_gate": w_exp_gate.astype(BF16), "exp_up": w_exp_up.astype(BF16), "exp_down": w_exp_down.astype(BF16),
        "sh_gate": w_sh_gate.astype(BF16), "sh_up": w_sh_up.astype(BF16), "sh_down": w_sh_down.astype(BF16),
        "ln2_g": ln2_g, "ln2_b": ln2_b,
    }
    depth = w_in.shape[0]
    y_prompt = _trunk(x_prompt, mem_prompt, emb_ln_g, emb_ln_b, w, depth)
    y_sample = _trunk(x_sample, mem_sample, emb_ln_g, emb_ln_b, w, depth)
    return (y_prompt, y_sample)
```

```python
import functools
import math

import numpy as np
import jax
import jax.numpy as jnp
from jax import lax
from jax.experimental import pallas as pl
from jax.experimental.pallas import tpu as pltpu

F32 = jnp.float32
BF16 = jnp.bfloat16

D_MODEL = 1024
DEPTH = 4
DIL_GROUPS = ((128, 1), (512, 4), (2048, 16))
N_GROUPS_A = 3
HEADS_A = 4
HEAD_DIM_A = 128
GROUP_WIDTH_A = HEADS_A * HEAD_DIM_A
WIDTH_A = N_GROUPS_A * GROUP_WIDTH_A
RADIUS_A = 64
HEADS_B = 8
Q_LORA = 256
KV_LORA = 128
QK_NOPE = 64
QK_ROPE = 32
V_DIM_B = 64
ROPE_THETA = 10000.0
HEAD_PAD_B = 128
HEADS_C = 4
HEAD_DIM_C = 128
OUT_C = HEADS_C * HEAD_DIM_C
N_BRANCH = 3
BRANCH_WIDTH = 512
N_EXPERTS = 64
TOP_K = 8
N_EXPERT_GROUPS = 8
EXPERTS_PER_GROUP = N_EXPERTS // N_EXPERT_GROUPS
TOPK_GROUPS = 4
D_EXPERT = 256
ROUTED_SCALE = 2.5
DEEPNORM_ALPHA = (2 * DEPTH) ** 0.25
LN_EPS = 1e-5
RMS_EPS = 1e-6

SEG_QKV = 3 * WIDTH_A
SEG_GROUP = 3 * GROUP_WIDTH_A
SEG_MLA = Q_LORA + KV_LORA + 2 * HEAD_PAD_B
SEG_QC = OUT_C
SEG_GL = N_BRANCH * D_MODEL
N_PROJ = SEG_QKV + SEG_MLA + SEG_QC + SEG_GL

NEG_BIG = -1e30
VMEM_LIMIT = 56 * 2 ** 20

NT_DIMS = (((1,), (1,)), ((), ()))


def _params(*sem):
    return pltpu.CompilerParams(dimension_semantics=sem, vmem_limit_bytes=VMEM_LIMIT)


def _resident(block_shape, index_map):
    return pl.BlockSpec(block_shape, index_map, pipeline_mode=pl.Buffered(1))


def _layer_norm_rows(h, g, b):
    mu = jnp.mean(h, axis=-1, keepdims=True)
    c = h - mu
    var = jnp.mean(c * c, axis=-1, keepdims=True)
    return c * lax.rsqrt(var + LN_EPS) * g + b


def _rms_norm_rows(h, g):
    return h * lax.rsqrt(jnp.mean(h * h, axis=-1, keepdims=True) + RMS_EPS) * g


def _embed_ln_kernel(x_ref, g_ref, b_ref, xf_ref):
    xf_ref[...] = _layer_norm_rows(x_ref[...], g_ref[...], b_ref[...])


def _embed_ln(x, g, b):
    t, d = x.shape
    tm = 512
    row = lambda i: (i, 0)
    fixed = lambda i: (0, 0)
    return pl.pallas_call(
        _embed_ln_kernel,
        grid=(t // tm,),
        in_specs=[pl.BlockSpec((tm, d), row), pl.BlockSpec((1, d), fixed), pl.BlockSpec((1, d), fixed)],
        out_specs=pl.BlockSpec((tm, d), row),
        out_shape=jax.ShapeDtypeStruct((t, d), F32),
        compiler_params=_params("parallel"),
        name="embed_ln",
    )(x, g, b)


PROJ_CHUNK = 512
PROJ_TM = 512
LANE_SLABS = GROUP_WIDTH_A // 128


def _proj_kernel(x_ref, w_ref, qkv0_ref, cm1_ref, cm2_ref, mla_ref, qc_ref, gl_ref, slab_ref):
    xb = x_ref[...].astype(BF16)

    def chunk(col, width):
        return jnp.dot(xb, w_ref[:, col:col + width], preferred_element_type=F32)

    col = 0
    for c in range(0, SEG_GROUP, PROJ_CHUNK):
        qkv0_ref[:, c:c + PROJ_CHUNK] = chunk(col + c, PROJ_CHUNK).astype(BF16)
    col += SEG_GROUP
    for ref, (_, d) in ((cm1_ref, DIL_GROUPS[1]), (cm2_ref, DIL_GROUPS[2])):
        n = PROJ_TM // d
        for c in range(0, SEG_GROUP, GROUP_WIDTH_A):
            res = chunk(col + c, GROUP_WIDTH_A)
            for s in range(LANE_SLABS):
                slab_ref[s] = res[:, s * 128:(s + 1) * 128]
            for r in range(d):
                piece = jnp.concatenate([slab_ref[s, pl.ds(r, n, stride=d), :] for s in range(LANE_SLABS)], axis=1)
                ref[0, r, :, c:c + GROUP_WIDTH_A] = piece.astype(BF16)
        col += SEG_GROUP
    for ref, width in ((mla_ref, SEG_MLA), (qc_ref, SEG_QC), (gl_ref, SEG_GL)):
        for c in range(0, width, PROJ_CHUNK):
            w = min(PROJ_CHUNK, width - c)
            ref[:, c:c + w] = chunk(col + c, w).astype(BF16)
        col += width


def _project(x, w, batch, seq):
    t, d = x.shape
    tm = PROJ_TM
    tiles_per_seq = seq // tm
    row = lambda i: (i, 0)

    def class_major(dil):
        shape = (batch, dil, seq // dil, SEG_GROUP)
        spec = pl.BlockSpec((1, dil, tm // dil, SEG_GROUP), lambda i: (i // tiles_per_seq, 0, i % tiles_per_seq, 0))
        return jax.ShapeDtypeStruct(shape, BF16), spec

    (cm1_shape, cm1_spec), (cm2_shape, cm2_spec) = class_major(DIL_GROUPS[1][1]), class_major(DIL_GROUPS[2][1])
    flat = (SEG_GROUP, SEG_MLA, SEG_QC, SEG_GL)
    flat_shapes = [jax.ShapeDtypeStruct((t, n), BF16) for n in flat]
    flat_specs = [pl.BlockSpec((tm, n), row) for n in flat]
    return pl.pallas_call(
        _proj_kernel,
        grid=(t // tm,),
        in_specs=[pl.BlockSpec((tm, d), row), _resident((d, N_PROJ), lambda i: (0, 0))],
        out_specs=[flat_specs[0], cm1_spec, cm2_spec] + flat_specs[1:],
        out_shape=[flat_shapes[0], cm1_shape, cm2_shape] + flat_shapes[1:],
        scratch_shapes=[pltpu.VMEM((LANE_SLABS, tm, 128), F32)],
        compiler_params=_params("parallel"),
        name="in_proj",
    )(x, w)


BAND_TQ = 512
BAND_QB = 128
BAND_KB = BAND_QB + 2 * RADIUS_A


def _band_kernel(q_ref, kp_ref, km_ref, kn_ref, vp_ref, vm_ref, vn_ref, o_ref, lse_ref, k_scr, v_scr,
                 *, seq_len, slopes):
    i = pl.program_id(1)
    r = RADIUS_A
    k_scr[0:r, :] = kp_ref[...]
    k_scr[r:r + BAND_TQ, :] = km_ref[...]
    k_scr[r + BAND_TQ:, :] = kn_ref[...]
    v_scr[0:r, :] = vp_ref[...]
    v_scr[r:r + BAND_TQ, :] = vm_ref[...]
    v_scr[r + BAND_TQ:, :] = vn_ref[...]

    row = lax.broadcasted_iota(jnp.int32, (BAND_QB, BAND_KB), 0)
    col = lax.broadcasted_iota(jnp.int32, (BAND_QB, BAND_KB), 1)
    rel = col - r - row
    dist = jnp.abs(rel).astype(F32)
    in_band = jnp.abs(rel) <= r
    scale = HEAD_DIM_A ** -0.5

    for qb in range(BAND_TQ // BAND_QB):
        key_pos = i * BAND_TQ + qb * BAND_QB - r + col
        valid = in_band & (key_pos >= 0) & (key_pos < seq_len)
        for h in range(HEADS_A):
            lanes = slice(h * HEAD_DIM_A, (h + 1) * HEAD_DIM_A)
            q = q_ref[qb * BAND_QB:(qb + 1) * BAND_QB, lanes]
            k = k_scr[qb * BAND_QB:qb * BAND_QB + BAND_KB, lanes]
            v = v_scr[qb * BAND_QB:qb * BAND_QB + BAND_KB, lanes]
            s = lax.dot_general(q, k, NT_DIMS, preferred_element_type=F32)
            logits = jnp.where(valid, s * scale - slopes[h] * dist, NEG_BIG)
            m = jnp.max(logits, axis=1, keepdims=True)
            p = jnp.exp(logits - m)
            den = jnp.sum(p, axis=1, keepdims=True)
            o = jnp.dot(p.astype(BF16), v, preferred_element_type=F32) / den
            rows = slice(qb * BAND_QB, (qb + 1) * BAND_QB)
            o_ref[rows, lanes] = o.astype(BF16)
            lse_ref[rows, lanes] = jnp.broadcast_to(m + jnp.log(den), (BAND_QB, HEAD_DIM_A))


def _band_attention(q_src, k_src, v_src, n_seq, seq_len, slopes):
    tq, r = BAND_TQ, RADIUS_A
    assert seq_len % tq == 0 and tq % r == 0
    steps = seq_len // tq
    halo_per_tile = tq // r
    halo_blocks = seq_len // r

    def main_map(cb):
        return lambda n, i: (n * steps + i, cb)

    def prev_map(cb):
        return lambda n, i: (n * halo_blocks + jnp.maximum(i * halo_per_tile - 1, 0), cb)

    def next_map(cb):
        return lambda n, i: (n * halo_blocks + jnp.minimum((i + 1) * halo_per_tile, halo_blocks - 1), cb)

    (qa, qcb), (ka, kcb), (va, vcb) = q_src, k_src, v_src
    w = GROUP_WIDTH_A
    rows = n_seq * seq_len
    return pl.pallas_call(
        functools.partial(_band_kernel, seq_len=seq_len, slopes=slopes),
        grid=(n_seq, steps),
        in_specs=[
            pl.BlockSpec((tq, w), main_map(qcb)),
            pl.BlockSpec((r, w), prev_map(kcb)), pl.BlockSpec((tq, w), main_map(kcb)),
            pl.BlockSpec((r, w), next_map(kcb)),
            pl.BlockSpec((r, w), prev_map(vcb)), pl.BlockSpec((tq, w), main_map(vcb)),
            pl.BlockSpec((r, w), next_map(vcb)),
        ],
        out_specs=[pl.BlockSpec((tq, w), main_map(0)), pl.BlockSpec((tq, w), main_map(0))],
        out_shape=[jax.ShapeDtypeStruct((rows, w), BF16), jax.ShapeDtypeStruct((rows, w), F32)],
        scratch_shapes=[pltpu.VMEM((tq + 2 * r, w), BF16), pltpu.VMEM((tq + 2 * r, w), BF16)],
        compiler_params=_params("parallel", "parallel"),
        name="band_attention",
    )(qa, ka, ka, ka, va, va, va)


def _alibi_slopes():
    n = N_GROUPS_A * HEADS_A
    return [2.0 ** (-8.0 * (i + 1) / n) for i in range(n)]


def _dilated_mixer(group_qkv, batch, seq):
    slopes = _alibi_slopes()
    outs = []
    for g, (_, d) in enumerate(DIL_GROUPS):
        group_slopes = tuple(float(s * d) for s in slopes[g * HEADS_A:(g + 1) * HEADS_A])
        rows = group_qkv[g].reshape(batch * seq, SEG_GROUP)
        o, lse = _band_attention((rows, 0), (rows, 1), (rows, 2), batch * d, seq // d, group_slopes)
        if d > 1:
            o = o.reshape(batch, d, seq // d, GROUP_WIDTH_A)
            lse = lse.reshape(batch, d, seq // d, GROUP_WIDTH_A)
        outs.append((o, lse))
    return outs


V_ROWS_B = 80
MLA_TQ = 1024
MLA_TK = 2048
MLA_KS = 256
MLA_QS = 512
MLA_LOOKAHEAD = 2
MLA_GAP_LIMIT = 64.0
MLA_HEADS_PER_STEP = 2


def _mla_prep_kernel(mla_ref, cos_ref, sin_ref, cos_t_ref, sin_t_ref, gq_ref, wqa_ref, wqb_ref, gkv_ref,
                     wk_ref, wv_ref, qt_ref, k_ref, vt_ref):
    m = mla_ref[...]
    cq = m[:, 0:Q_LORA].astype(F32)
    ckv = m[:, Q_LORA:Q_LORA + KV_LORA].astype(F32)
    kr = m[:, Q_LORA + KV_LORA:Q_LORA + KV_LORA + HEAD_PAD_B].astype(F32)
    kr_rot = m[:, Q_LORA + KV_LORA + HEAD_PAD_B:].astype(F32)
    scale = (QK_NOPE + QK_ROPE) ** -0.5 * math.log2(math.e)

    cqn = _rms_norm_rows(cq, gq_ref[...]).astype(BF16)
    qa_t = lax.dot_general(wqa_ref[...], cqn, NT_DIMS, preferred_element_type=F32)
    qb_t = lax.dot_general(wqb_ref[...], cqn, NT_DIMS, preferred_element_type=F32)
    ckvn = _rms_norm_rows(ckv, gkv_ref[...]).astype(BF16)
    kn = jnp.dot(ckvn, wk_ref[...], preferred_element_type=F32)
    v_t = lax.dot_general(wv_ref[...], ckvn, NT_DIMS, preferred_element_type=F32)
    k_rope = kr * cos_ref[...] + kr_rot * sin_ref[...]
    cos_t, sin_t = cos_t_ref[...], sin_t_ref[...]
    row = lax.broadcasted_iota(jnp.int32, (V_ROWS_B, 1), 0)
    ones_row = (row == V_DIM_B).astype(F32)
    for h in range(HEADS_B):
        slot = slice(h * HEAD_PAD_B, (h + 1) * HEAD_PAD_B)
        qt_ref[slot, :] = ((qa_t[slot, :] * cos_t + qb_t[slot, :] * sin_t) * scale).astype(BF16)
        k_ref[:, slot] = (kn[:, slot] + k_rope).astype(BF16)
        vt_ref[0, h, 0] = (v_t[h * V_ROWS_B:(h + 1) * V_ROWS_B, :] + ones_row).astype(BF16)


def _mla_prep(mla, tables, gq, wqa_t, wqb_t, gkv, wk, wv_t, batch, seq):
    t = mla.shape[0]
    tm = MLA_TK
    chunks = seq // tm
    cos, sin, cos_t, sin_t = tables
    row = lambda i: (i, 0)
    pos = lambda i: (i % chunks, 0)
    pos_t = lambda i: (0, i % chunks)
    fixed = lambda i: (0, 0)
    wide = HEADS_B * HEAD_PAD_B
    return pl.pallas_call(
        _mla_prep_kernel,
        grid=(t // tm,),
        in_specs=[pl.BlockSpec((tm, SEG_MLA), row),
                  pl.BlockSpec((tm, HEAD_PAD_B), pos), pl.BlockSpec((tm, HEAD_PAD_B), pos),
                  pl.BlockSpec((HEAD_PAD_B, tm), pos_t), pl.BlockSpec((HEAD_PAD_B, tm), pos_t),
                  pl.BlockSpec((1, Q_LORA), fixed), pl.BlockSpec((wide, Q_LORA), fixed),
                  pl.BlockSpec((wide, Q_LORA), fixed),
                  pl.BlockSpec((1, KV_LORA), fixed), pl.BlockSpec((KV_LORA, wide), fixed),
                  pl.BlockSpec((HEADS_B * V_ROWS_B, KV_LORA), fixed)],
        out_specs=[pl.BlockSpec((wide, tm), lambda i: (i // chunks, i % chunks)),
                   pl.BlockSpec((tm, wide), row),
                   pl.BlockSpec((1, HEADS_B, 1, V_ROWS_B, tm), lambda i: (i // chunks, 0, i % chunks, 0, 0))],
        out_shape=[jax.ShapeDtypeStruct((batch * wide, seq), BF16),
                   jax.ShapeDtypeStruct((t, wide), BF16),
                   jax.ShapeDtypeStruct((batch, HEADS_B, chunks, V_ROWS_B, tm), BF16)],
        compiler_params=_params("parallel"),
        name="mla_prep",
    )(mla, cos, sin, cos_t, sin_t, gq, wqa_t, wqb_t, gkv, wk, wv_t)


def _mla_attn_kernel(qt_ref, k_ref, vt_ref, o_ref, *, seq):
    n_sub = MLA_TK // MLA_KS
    nq = MLA_TQ // MLA_QS
    units = [(c, hh, j) for c in range(n_sub) for hh in range(MLA_HEADS_PER_STEP) for j in range(nq)]

    def scores(rows, hh, j):
        slot = slice(hh * HEAD_PAD_B, (hh + 1) * HEAD_PAD_B)
        return jnp.dot(k_ref[rows, slot], qt_ref[slot, j * MLA_QS:(j + 1) * MLA_QS],
                       preferred_element_type=F32)

    def sweep(kc, carry, update):
        def unit_scores(u):
            c, hh, j = units[u]
            return scores(pl.ds(pl.multiple_of(kc * MLA_TK + c * MLA_KS, MLA_KS), MLA_KS), hh, j)

        new = list(carry)
        pending = [unit_scores(u) for u in range(min(MLA_LOOKAHEAD, len(units)))]
        worst = None
        for u, (c, hh, j) in enumerate(units):
            if u + MLA_LOOKAHEAD < len(units):
                pending.append(unit_scores(u + MLA_LOOKAHEAD))
            v_blk = vt_ref[0, hh, kc, :, c * MLA_KS:(c + 1) * MLA_KS]
            new[hh * nq + j], gap = update(new[hh * nq + j], pending.pop(0), v_blk)
            worst = gap if worst is None else jnp.maximum(worst, gap)
        return tuple(new), worst

    def exact_update(state, s, v_blk):
        m, acc = state
        m_new = jnp.maximum(m, jnp.max(s, axis=0, keepdims=True))
        p = jnp.exp2(s - m_new).astype(BF16)
        acc = jnp.exp2(m - m_new) * acc + jnp.dot(v_blk, p, preferred_element_type=F32)
        return (m_new, acc), jnp.zeros_like(m)

    def lagged_update(state, s, v_blk):
        m, acc = state
        p = jnp.exp2(s - m).astype(BF16)
        col_max = jnp.max(s, axis=0, keepdims=True)
        gap = col_max - m
        m_new = jnp.maximum(m, col_max)
        acc = (acc + jnp.dot(v_blk, p, preferred_element_type=F32)) * jnp.exp2(m - m_new)
        return (m_new, acc), gap

    def step(kc, carry):
        fast, worst = sweep(kc, carry, lagged_update)
        overflow_risk = jnp.max(worst) > MLA_GAP_LIMIT
        return lax.cond(overflow_risk, lambda: sweep(kc, carry, exact_update)[0], lambda: fast)

    first = pl.ds(0, MLA_KS)
    init = tuple((jnp.max(scores(first, hh, j), axis=0, keepdims=True), jnp.zeros((V_ROWS_B, MLA_QS), F32))
                 for hh in range(MLA_HEADS_PER_STEP) for j in range(nq))
    final = lax.fori_loop(0, seq // MLA_TK, step, init)
    parts = [acc[0:V_DIM_B, :] / acc[V_DIM_B:V_DIM_B + 1, :] for _, acc in final]
    heads = [jnp.concatenate(parts[hh * nq:(hh + 1) * nq], axis=1) for hh in range(MLA_HEADS_PER_STEP)]
    o_ref[...] = jnp.concatenate(heads, axis=0).T.astype(BF16)


def _mla_attention(q_t, k, v_t, batch, seq):
    t = batch * seq
    tq = MLA_TQ
    steps = seq // tq
    chunks = seq // MLA_TK
    pairs = HEADS_B // MLA_HEADS_PER_STEP
    pair = MLA_HEADS_PER_STEP * HEAD_PAD_B
    return pl.pallas_call(
        functools.partial(_mla_attn_kernel, seq=seq),
        grid=(batch, pairs, steps),
        in_specs=[pl.BlockSpec((pair, tq), lambda b, hp, i: (b * pairs + hp, i)),
                  _resident((seq, pair), lambda b, hp, i: (b, hp)),
                  _resident((1, MLA_HEADS_PER_STEP, chunks, V_ROWS_B, MLA_TK), lambda b, hp, i: (b, hp, 0, 0, 0))],
        out_specs=pl.BlockSpec((tq, MLA_HEADS_PER_STEP * V_DIM_B), lambda b, hp, i: (b * steps + i, hp)),
        out_shape=jax.ShapeDtypeStruct((t, HEADS_B * V_DIM_B), BF16),
        compiler_params=_params("parallel", "parallel", "arbitrary"),
        name="mla_attention",
    )(q_t, k, v_t)


def _rope_tables(seq):
    inv_freq = 1.0 / (ROPE_THETA ** (jnp.arange(0, QK_ROPE, 2, dtype=F32) / QK_ROPE))
    ang = jnp.arange(seq, dtype=F32)[:, None] * inv_freq[None, :]
    cos, sin = jnp.cos(ang), jnp.sin(ang)
    pad = HEAD_PAD_B - QK_NOPE - QK_ROPE
    cos_s = jnp.concatenate([jnp.ones((seq, QK_NOPE), F32), cos, cos, jnp.zeros((seq, pad), F32)], axis=1)
    sin_s = jnp.concatenate([jnp.zeros((seq, QK_NOPE), F32), sin, sin, jnp.zeros((seq, pad), F32)], axis=1)
    return cos_s, sin_s, cos_s.T, sin_s.T


def _mem_kv_kernel(mem_ref, w_ref, o_ref):
    o_ref[...] = jnp.dot(mem_ref[...], w_ref[...], preferred_element_type=F32).astype(BF16)


def _mem_kv(mem_b, w):
    rows, d = mem_b.shape
    n = w.shape[1]
    tm = 256
    return pl.pallas_call(
        _mem_kv_kernel,
        grid=(rows // tm,),
        in_specs=[pl.BlockSpec((tm, d), lambda i: (i, 0)), pl.BlockSpec((d, n), lambda i: (0, 0))],
        out_specs=pl.BlockSpec((tm, n), lambda i: (i, 0)),
        out_shape=jax.ShapeDtypeStruct((rows, n), BF16),
        compiler_params=_params("parallel"),
        name="mem_kv",
    )(mem_b, w)


def _mem_attn_kernel(q_ref, kv_ref, o_ref):
    scale = HEAD_DIM_C ** -0.5
    for h in range(HEADS_C):
        lanes = slice(h * HEAD_DIM_C, (h + 1) * HEAD_DIM_C)
        k = kv_ref[:, lanes]
        v = kv_ref[:, OUT_C + h * HEAD_DIM_C:OUT_C + (h + 1) * HEAD_DIM_C]
        s = lax.dot_general(q_ref[:, lanes], k, NT_DIMS, preferred_element_type=F32) * scale
        m = jnp.max(s, axis=1, keepdims=True)
        p = jnp.exp(s - m)
        den = jnp.sum(p, axis=1, keepdims=True)
        o_ref[:, lanes] = (jnp.dot(p.astype(BF16), v, preferred_element_type=F32) / den).astype(BF16)


def _mem_attention(qc, kv, batch, seq, n_mem):
    t = batch * seq
    ts = 1024
    steps = seq // ts
    return pl.pallas_call(
        _mem_attn_kernel,
        grid=(batch, steps),
        in_specs=[pl.BlockSpec((ts, OUT_C), lambda b, i: (b * steps + i, 0)),
                  pl.BlockSpec((n_mem, 2 * OUT_C), lambda b, i: (b, 0))],
        out_specs=pl.BlockSpec((ts, OUT_C), lambda b, i: (b * steps + i, 0)),
        out_shape=jax.ShapeDtypeStruct((t, OUT_C), BF16),
        compiler_params=_params("parallel", "parallel"),
        name="mem_attention",
    )(qc, kv)


def _token_order(cm_ref, slab_ref):
    d, n = cm_ref.shape[1], cm_ref.shape[2]
    for r in range(d):
        blk = cm_ref[0, r].astype(F32)
        for s in range(LANE_SLABS):
            slab_ref[s, pl.ds(r, n, stride=d), :] = blk[:, s * 128:(s + 1) * 128]
    return jnp.concatenate([slab_ref[s] for s in range(LANE_SLABS)], axis=1)


def _merge_kernel(oa0_ref, oa1_ref, oa2_ref, l0_ref, l1_ref, l2_ref, ob_ref, oc_ref, gl_ref, x_ref,
                  wb_ref, wo_ref, g_ref, b_ref, xf_ref, slab_ref):
    l0 = l0_ref[...]
    l1, l2 = _token_order(l1_ref, slab_ref), _token_order(l2_ref, slab_ref)
    m = jnp.maximum(jnp.maximum(l0, l1), l2)
    e0, e1, e2 = jnp.exp(l0 - m), jnp.exp(l1 - m), jnp.exp(l2 - m)
    oa = e0 * oa0_ref[...].astype(F32) + e1 * _token_order(oa1_ref, slab_ref) + e2 * _token_order(oa2_ref, slab_ref)
    oa = (oa / (e0 + e1 + e2)).astype(BF16)
    z = None
    for i, o in enumerate((oa, ob_ref[...], oc_ref[...])):
        gate = jax.nn.sigmoid(gl_ref[:, i * D_MODEL:(i + 1) * D_MODEL].astype(F32))
        term = gate * jnp.dot(o, wb_ref[i], preferred_element_type=F32)
        z = term if z is None else z + term
    y = jnp.dot(z.astype(BF16), wo_ref[...], preferred_element_type=F32)
    xf_ref[...] = _layer_norm_rows(DEEPNORM_ALPHA * x_ref[...] + y, g_ref[...], b_ref[...])


def _merge(oa, ob, oc, gl, x, wb, wo, g, b, seq):
    t, d = x.shape
    tm = PROJ_TM
    tiles_per_seq = seq // tm
    row = lambda i: (i, 0)
    fixed = lambda i: (0, 0)
    half = pl.BlockSpec((tm, BRANCH_WIDTH), row)

    def class_major(dil):
        return pl.BlockSpec((1, dil, tm // dil, GROUP_WIDTH_A),
                            lambda i: (i // tiles_per_seq, 0, i % tiles_per_seq, 0))

    cm1, cm2 = class_major(DIL_GROUPS[1][1]), class_major(DIL_GROUPS[2][1])
    (oa0, l0), (oa1, l1), (oa2, l2) = oa
    return pl.pallas_call(
        _merge_kernel,
        grid=(t // tm,),
        in_specs=[half, cm1, cm2, half, cm1, cm2, half, half,
                  pl.BlockSpec((tm, SEG_GL), row), pl.BlockSpec((tm, d), row),
                  _resident((N_BRANCH, BRANCH_WIDTH, d), lambda i: (0, 0, 0)),
                  _resident((d, d), fixed),
                  pl.BlockSpec((1, d), fixed), pl.BlockSpec((1, d), fixed)],
        out_specs=pl.BlockSpec((tm, d), row),
        out_shape=jax.ShapeDtypeStruct((t, d), F32),
        scratch_shapes=[pltpu.VMEM((LANE_SLABS, tm, 128), F32)],
        compiler_params=_params("parallel"),
        name="merge_ln1",
    )(oa0, oa1, oa2, l0, l1, l2, ob, oc, gl, x, wb, wo, g, b)


GATE_LANES = 128


def _first_index_of_max(vals, idx, axis, sentinel):
    mx = jnp.max(vals, axis=axis, keepdims=True)
    return jnp.min(jnp.where(vals == mx, idx, sentinel), axis=axis, keepdims=True)


def _router_kernel(x_ref, w_ref, bias_ref, xg_ref, code_ref):
    tm = x_ref.shape[0]
    x = x_ref[...]
    logits = lax.dot_general(w_ref[...], x.astype(BF16), NT_DIMS, preferred_element_type=F32)
    scores = jax.nn.sigmoid(logits)
    choice = scores + bias_ref[...]
    neg = -jnp.inf

    c3 = choice.reshape(N_EXPERT_GROUPS, EXPERTS_PER_GROUP, tm)
    e_idx = lax.broadcasted_iota(jnp.int32, c3.shape, 1)
    first = jnp.max(c3, axis=1, keepdims=True)
    first_at = jnp.min(jnp.where(c3 == first, e_idx, EXPERTS_PER_GROUP), axis=1, keepdims=True)
    second = jnp.max(jnp.where(e_idx == first_at, neg, c3), axis=1, keepdims=True)
    group_score = (first + second).reshape(N_EXPERT_GROUPS, tm)

    g_idx = lax.broadcasted_iota(jnp.int32, group_score.shape, 0)
    group_sel = jnp.zeros(group_score.shape, jnp.bool_)
    for _ in range(TOPK_GROUPS):
        at = _first_index_of_max(group_score, g_idx, 0, N_EXPERT_GROUPS)
        hit = g_idx == at
        group_sel = group_sel | hit
        group_score = jnp.where(hit, neg, group_score)

    allowed = jnp.broadcast_to(group_sel.reshape(N_EXPERT_GROUPS, 1, tm), c3.shape).reshape(N_EXPERTS, tm)
    cand = jnp.where(allowed, choice, neg)
    x_idx = lax.broadcasted_iota(jnp.int32, cand.shape, 0)
    chosen = jnp.zeros(cand.shape, jnp.bool_)
    for _ in range(TOP_K):
        at = _first_index_of_max(cand, x_idx, 0, N_EXPERTS)
        hit = x_idx == at
        chosen = chosen | hit
        cand = jnp.where(hit, neg, cand)

    w_sel = jnp.where(chosen, scores, 0.0)
    gates = w_sel / jnp.sum(w_sel, axis=0, keepdims=True) * ROUTED_SCALE
    padded = jnp.concatenate([gates, jnp.zeros((GATE_LANES - N_EXPERTS, tm), F32)], axis=0)
    xg_ref[:, 0:D_MODEL] = x
    xg_ref[:, D_MODEL:] = padded.T
    bit = jnp.left_shift(1, g_idx)
    code_ref[...] = jnp.sum(jnp.where(group_sel, bit, 0), axis=0, keepdims=True)


def _router(x, w_t, bias):
    t, d = x.shape
    tm = 1024
    return pl.pallas_call(
        _router_kernel,
        grid=(t // tm,),
        in_specs=[pl.BlockSpec((tm, d), lambda i: (i, 0)), pl.BlockSpec((N_EXPERTS, d), lambda i: (0, 0)),
                  pl.BlockSpec((N_EXPERTS, 1), lambda i: (0, 0))],
        out_specs=[pl.BlockSpec((tm, XG_WIDTH), lambda i: (i, 0)), pl.BlockSpec((1, tm), lambda i: (0, i))],
        out_shape=[jax.ShapeDtypeStruct((t, XG_WIDTH), F32), jax.ShapeDtypeStruct((1, t), jnp.int32)],
        compiler_params=_params("parallel"),
        name="router",
    )(x, w_t, bias)


XG_WIDTH = D_MODEL + GATE_LANES
MOE_TM = 1024
MOE_SUB = 128
MOE_NSUB = MOE_TM // MOE_SUB
TOKEN_BITS = 16


def _dispatch_plan(code):
    t = code.shape[1]
    tiles = t // MOE_TM
    assert t <= 1 << TOKEN_BITS
    key = jnp.sort(code[0] * (1 << TOKEN_BITS) + jnp.arange(t, dtype=jnp.int32))
    perm = key & ((1 << TOKEN_BITS) - 1)
    bits = ((key >> TOKEN_BITS)[:, None] >> jnp.arange(N_EXPERT_GROUPS, dtype=jnp.int32)[None, :]) & 1
    sub = jnp.max(bits.reshape(tiles, MOE_NSUB, MOE_SUB, N_EXPERT_GROUPS), axis=2)
    tile_active = jnp.max(sub, axis=1)
    n_active = jnp.sum(tile_active, axis=1).astype(jnp.int32)
    order = jnp.argsort(1 - tile_active, axis=1, stable=True).astype(jnp.int32)
    step = jnp.arange(N_EXPERT_GROUPS, dtype=jnp.int32)[None, :]
    last = jnp.take_along_axis(order, jnp.maximum(n_active - 1, 0)[:, None], axis=1)
    groups = jnp.where(step < n_active[:, None], order, last)
    flags = jnp.transpose(sub, (0, 2, 1)).astype(jnp.int32)
    return perm.reshape(tiles, 1, MOE_TM), groups.reshape(-1), n_active, flags.reshape(-1)


def _row_dma(src_ref, dst_ref, src_row, dst_row, sem):
    return pltpu.make_async_copy(src_ref.at[pl.ds(src_row, 1)], dst_ref.at[pl.ds(dst_row, 1)], sem)


DMA_THREADS = 2


def _gather_rows_kernel(perm_ref, src_ref, out_ref, sem):
    def start(q, carry):
        for u in range(DMA_THREADS):
            r = q * DMA_THREADS + u
            _row_dma(src_ref, out_ref, perm_ref[0, 0, r], r, sem).start(priority=u)
        return carry

    lax.fori_loop(0, MOE_TM // DMA_THREADS, start, 0)
    pltpu.make_async_copy(src_ref.at[pl.ds(0, MOE_TM)], out_ref, sem).wait()


def _gather_rows(src, perm):
    t, width = src.shape
    return pl.pallas_call(
        _gather_rows_kernel,
        grid=(t // MOE_TM,),
        in_specs=[pl.BlockSpec((1, 1, MOE_TM), lambda i: (i, 0, 0), memory_space=pltpu.SMEM),
                  pl.BlockSpec(memory_space=pl.ANY)],
        out_specs=pl.BlockSpec((MOE_TM, width), lambda i: (i, 0)),
        out_shape=jax.ShapeDtypeStruct((t, width), src.dtype),
        scratch_shapes=[pltpu.SemaphoreType.DMA(())],
        compiler_params=_params("arbitrary"),
        name="moe_gather",
    )(perm, src)


def _scatter_rows_kernel(perm_ref, src_ref, out_ref, sem):
    def start(q, carry):
        for u in range(DMA_THREADS):
            r = q * DMA_THREADS + u
            _row_dma(src_ref, out_ref, r, perm_ref[0, 0, r], sem).start(priority=u)
        return carry

    lax.fori_loop(0, MOE_TM // DMA_THREADS, start, 0)
    pltpu.make_async_copy(src_ref, out_ref.at[pl.ds(0, MOE_TM)], sem).wait()


def _scatter_rows(src, perm):
    t, width = src.shape
    return pl.pallas_call(
        _scatter_rows_kernel,
        grid=(t // MOE_TM,),
        in_specs=[pl.BlockSpec((1, 1, MOE_TM), lambda i: (i, 0, 0), memory_space=pltpu.SMEM),
                  pl.BlockSpec((MOE_TM, width), lambda i: (i, 0))],
        out_specs=pl.BlockSpec(memory_space=pl.ANY),
        out_shape=jax.ShapeDtypeStruct((t, width), src.dtype),
        scratch_shapes=[pltpu.SemaphoreType.DMA(())],
        compiler_params=_params("arbitrary"),
        name="moe_scatter",
    )(perm, src)


def _swiglu(xb, wg, wu):
    return jax.nn.silu(jnp.dot(xb, wg, preferred_element_type=F32)) * jnp.dot(xb, wu, preferred_element_type=F32)


def _moe_kernel(groups_ref, nact_ref, flags_ref, xg_ref, wg_ref, wu_ref, wd_ref, sg_ref, su_ref, sd_ref,
                g_ref, b_ref, o_ref, acc_ref, xb_ref):
    i = pl.program_id(0)
    j = pl.program_id(1)

    @pl.when(j == 0)
    def _():
        xb = xg_ref[:, 0:D_MODEL].astype(BF16)
        xb_ref[...] = xb
        h = _swiglu(xb, sg_ref[...], su_ref[...])
        acc_ref[...] = jnp.dot(h.astype(BF16), sd_ref[...], preferred_element_type=F32)

    @pl.when(j < nact_ref[i])
    def _():
        group = groups_ref[i * N_EXPERT_GROUPS + j]
        wd_all = wd_ref[...].reshape(EXPERTS_PER_GROUP * D_EXPERT, D_MODEL)
        for s in range(MOE_NSUB):
            @pl.when(flags_ref[(i * N_EXPERT_GROUPS + group) * MOE_NSUB + s] != 0)
            def _():
                rows = slice(s * MOE_SUB, (s + 1) * MOE_SUB)
                xb = xb_ref[rows, :]
                gates = xg_ref[rows, D_MODEL:]
                lane = lax.broadcasted_iota(jnp.int32, gates.shape, 1)
                hs = []
                for e in range(EXPERTS_PER_GROUP):
                    gate = jnp.sum(jnp.where(lane == group * EXPERTS_PER_GROUP + e, gates, 0.0),
                                   axis=1, keepdims=True)
                    hs.append((_swiglu(xb, wg_ref[e], wu_ref[e]) * gate).astype(BF16))
                acc_ref[rows, :] += jnp.dot(jnp.concatenate(hs, axis=1), wd_all, preferred_element_type=F32)

    @pl.when(j == pl.num_programs(1) - 1)
    def _():
        o_ref[...] = _layer_norm_rows(DEEPNORM_ALPHA * xg_ref[:, 0:D_MODEL] + acc_ref[...], g_ref[...], b_ref[...])


def _moe(xg_sorted, plan, wg, wu, wd, sg, su, sd, g, b):
    t = xg_sorted.shape[0]
    d, f, tm, ng = D_MODEL, D_EXPERT, MOE_TM, N_EXPERT_GROUPS
    _, groups, n_active, flags = plan
    row = lambda i, j, *_: (i, 0)
    fixed = lambda i, j, *_: (0, 0)
    expert_block = lambda i, j, groups_ref, *_: (groups_ref[i * ng + j], 0, 0)
    grid_spec = pltpu.PrefetchScalarGridSpec(
        num_scalar_prefetch=3,
        grid=(t // tm, ng),
        in_specs=[pl.BlockSpec((tm, XG_WIDTH), row),
                  pl.BlockSpec((EXPERTS_PER_GROUP, d, f), expert_block),
                  pl.BlockSpec((EXPERTS_PER_GROUP, d, f), expert_block),
                  pl.BlockSpec((EXPERTS_PER_GROUP, f, d), expert_block),
                  pl.BlockSpec((d, f), fixed), pl.BlockSpec((d, f), fixed), pl.BlockSpec((f, d), fixed),
                  pl.BlockSpec((1, d), fixed), pl.BlockSpec((1, d), fixed)],
        out_specs=pl.BlockSpec((tm, d), row),
        scratch_shapes=[pltpu.VMEM((tm, d), F32), pltpu.VMEM((tm, d), BF16)],
    )
    return pl.pallas_call(
        _moe_kernel,
        grid_spec=grid_spec,
        out_shape=jax.ShapeDtypeStruct((t, d), F32),
        compiler_params=_params("arbitrary", "arbitrary"),
        name="moe_ln2",
    )(groups, n_active, flags, xg_sorted, wg, wu, wd, sg, su, sd, g, b)


def _moe_layer(x, w_router_t, router_bias, wg, wu, wd, sg, su, sd, g, b):
    xg, code = _router(x, w_router_t, router_bias.reshape(N_EXPERTS, 1))
    plan = _dispatch_plan(code)
    perm = plan[0]
    y_sorted = _moe(_gather_rows(xg, perm), plan, wg, wu, wd, sg, su, sd, g, b)
    return _scatter_rows(y_sorted, perm)


def _rotate_half_columns(w):
    half = QK_ROPE // 2
    return jnp.concatenate([-w[..., half:], w[..., :half]], axis=-1)


def _prep_in_proj(w_in):
    layers, d, _ = w_in.shape
    cuts = np.cumsum((WIDTH_A, WIDTH_A, WIDTH_A, Q_LORA, KV_LORA, QK_ROPE, OUT_C))
    qa, ka, va, cq, ckv, kr, qc, gl = jnp.split(w_in, [int(c) for c in cuts], axis=-1)
    lead = jnp.zeros((layers, d, QK_NOPE), w_in.dtype)
    tail = jnp.zeros((layers, d, HEAD_PAD_B - QK_NOPE - QK_ROPE), w_in.dtype)
    kr_slot = jnp.concatenate([lead, kr, tail], axis=-1)
    kr_rot_slot = jnp.concatenate([lead, _rotate_half_columns(kr), tail], axis=-1)
    per_group = [m[..., g * GROUP_WIDTH_A:(g + 1) * GROUP_WIDTH_A] for g in range(N_GROUPS_A) for m in (qa, ka, va)]
    return jnp.concatenate(per_group + [cq, ckv, kr_slot, kr_rot_slot, qc, gl], axis=-1).astype(BF16)


def _prep_mla_weights(w_q_up, w_kv_up):
    layers = w_q_up.shape[0]
    wq = w_q_up.reshape(layers, Q_LORA, HEADS_B, QK_NOPE + QK_ROPE)
    nope, rope = wq[..., :QK_NOPE], wq[..., QK_NOPE:]
    pad = HEAD_PAD_B - QK_NOPE - QK_ROPE
    zq = lambda n: jnp.zeros((layers, Q_LORA, HEADS_B, n), w_q_up.dtype)
    wqa = jnp.concatenate([nope, rope, zq(pad)], axis=-1)
    wqb = jnp.concatenate([zq(QK_NOPE), _rotate_half_columns(rope), zq(pad)], axis=-1)
    wkv = w_kv_up.reshape(layers, KV_LORA, HEADS_B, QK_NOPE + V_DIM_B)
    zk = lambda n: jnp.zeros((layers, KV_LORA, HEADS_B, n), w_kv_up.dtype)
    wk = jnp.concatenate([wkv[..., :QK_NOPE], zk(HEAD_PAD_B - QK_NOPE)], axis=-1)
    wv = jnp.concatenate([wkv[..., QK_NOPE:], zk(V_ROWS_B - V_DIM_B)], axis=-1)
    flat = lambda w: w.reshape(layers, w.shape[1], -1).astype(BF16)
    flat_t = lambda w: jnp.swapaxes(flat(w), 1, 2)
    return flat_t(wqa), flat_t(wqb), flat(wk), flat_t(wv)


def _trunk(x, mem, emb_g, emb_b, w, depth):
    batch, seq, d = x.shape
    n_mem = mem.shape[1]
    t = batch * seq
    row2 = lambda v: v.reshape(1, -1)
    xf = _embed_ln(x.reshape(t, d), row2(emb_g), row2(emb_b))
    mem_b = mem.reshape(batch * n_mem, d).astype(BF16)
    tables = _rope_tables(seq)
    for l in range(depth):
        qkv0, cm1, cm2, mla, qc, gl = _project(xf, w["in_proj"][l], batch, seq)
        oa = _dilated_mixer((qkv0, cm1, cm2), batch, seq)
        q_t, k, v_t = _mla_prep(mla, tables, row2(w["q_norm_g"][l]), w["wqa"][l], w["wqb"][l],
                                row2(w["kv_norm_g"][l]), w["wk"][l], w["wv"][l], batch, seq)
        ob = _mla_attention(q_t, k, v_t, batch, seq)
        oc = _mem_attention(qc, _mem_kv(mem_b, w["mem_kv"][l]), batch, seq, n_mem)
        xf = _merge(oa, ob, oc, gl, xf, w["branch"][l], w["out"][l], row2(w["ln1_g"][l]), row2(w["ln1_b"][l]), seq)
        xf = _moe_layer(xf, w["router_t"][l], w["router_bias"][l], w["exp_gate"][l], w["exp_up"][l],
                        w["exp_down"][l], w["sh_gate"][l], w["sh_up"][l], w["sh_down"][l],
                        row2(w["ln2_g"][l]), row2(w["ln2_b"][l]))
    return xf.reshape(batch, seq, d)


def kernel(x_prompt, x_sample, mem_prompt, mem_sample, emb_ln_g, emb_ln_b, w_in, q_norm_g, w_q_up, kv_norm_g,
           w_kv_up, w_mem_kv, w_branch, w_out, ln1_g, ln1_b, w_router, router_bias, w_exp_gate, w_exp_up,
           w_exp_down, w_sh_gate, w_sh_up, w_sh_down, ln2_g, ln2_b):
    wqa, wqb, wk, wv = _prep_mla_weights(w_q_up, w_kv_up)
    w = {
        "in_proj": _prep_in_proj(w_in),
        "q_norm_g": q_norm_g, "kv_norm_g": kv_norm_g, "wqa": wqa, "wqb": wqb, "wk": wk, "wv": wv,
        "mem_kv": w_mem_kv.astype(BF16), "branch": w_branch.astype(BF16), "out": w_out.astype(BF16),
        "ln1_g": ln1_g, "ln1_b": ln1_b,
        "router_t": jnp.swapaxes(w_router, 1, 2).astype(BF16), "router_bias": router_bias,
        "exp_gate": w_exp_gate.astype(BF16), "exp_up": w_exp_up.astype(BF16), "exp_down": w_exp_down.astype(BF16),
        "sh_gate": w_sh_gate.astype(BF16), "sh_up": w_sh_up.astype(BF16), "sh_down": w_sh_down.astype(BF16),
        "ln2_g": ln2_g, "ln2_b": ln2_b,
    }
    depth = w_in.shape[0]
    y_prompt = _trunk(x_prompt, mem_prompt, emb_ln_g, emb_ln_b, w, depth)
    y_sample = _trunk(x_sample, mem_sample, emb_ln_g, emb_ln_b, w, depth)
    return (y_prompt, y_sample)
```

```python
import functools
import math

import numpy as np
import jax
import jax.numpy as jnp
from jax import lax
from jax.experimental import pallas as pl
from jax.experimental.pallas import tpu as pltpu

F32 = jnp.float32
BF16 = jnp.bfloat16

D_MODEL = 1024
DEPTH = 4
DIL_GROUPS = ((128, 1), (512, 4), (2048, 16))
N_GROUPS_A = 3
HEADS_A = 4
HEAD_DIM_A = 128
GROUP_WIDTH_A = HEADS_A * HEAD_DIM_A
WIDTH_A = N_GROUPS_A * GROUP_WIDTH_A
RADIUS_A = 64
HEADS_B = 8
Q_LORA = 256
KV_LORA = 128
QK_NOPE = 64
QK_ROPE = 32
V_DIM_B = 64
ROPE_THETA = 10000.0
HEAD_PAD_B = 128
HEADS_C = 4
HEAD_DIM_C = 128
OUT_C = HEADS_C * HEAD_DIM_C
N_BRANCH = 3
BRANCH_WIDTH = 512
N_EXPERTS = 64
TOP_K = 8
N_EXPERT_GROUPS = 8
EXPERTS_PER_GROUP = N_EXPERTS // N_EXPERT_GROUPS
TOPK_GROUPS = 4
D_EXPERT = 256
ROUTED_SCALE = 2.5
DEEPNORM_ALPHA = (2 * DEPTH) ** 0.25
LN_EPS = 1e-5
RMS_EPS = 1e-6

SEG_QKV = 3 * WIDTH_A
SEG_GROUP = 3 * GROUP_WIDTH_A
SEG_MLA = Q_LORA + KV_LORA + 2 * HEAD_PAD_B
SEG_QC = OUT_C
SEG_GL = N_BRANCH * D_MODEL
N_PROJ = SEG_QKV + SEG_MLA + SEG_QC + SEG_GL

NEG_BIG = -1e30
VMEM_LIMIT = 56 * 2 ** 20

NT_DIMS = (((1,), (1,)), ((), ()))


def _params(*sem):
    return pltpu.CompilerParams(dimension_semantics=sem, vmem_limit_bytes=VMEM_LIMIT)


def _resident(block_shape, index_map):
    return pl.BlockSpec(block_shape, index_map, pipeline_mode=pl.Buffered(1))


def _layer_norm_rows(h, g, b):
    mu = jnp.mean(h, axis=-1, keepdims=True)
    c = h - mu
    var = jnp.mean(c * c, axis=-1, keepdims=True)
    return c * lax.rsqrt(var + LN_EPS) * g + b


def _rms_norm_rows(h, g):
    return h * lax.rsqrt(jnp.mean(h * h, axis=-1, keepdims=True) + RMS_EPS) * g


def _embed_ln_kernel(x_ref, g_ref, b_ref, xf_ref):
    xf_ref[...] = _layer_norm_rows(x_ref[...], g_ref[...], b_ref[...])


def _embed_ln(x, g, b):
    t, d = x.shape
    tm = 512
    row = lambda i: (i, 0)
    fixed = lambda i: (0, 0)
    return pl.pallas_call(
        _embed_ln_kernel,
        grid=(t // tm,),
        in_specs=[pl.BlockSpec((tm, d), row), pl.BlockSpec((1, d), fixed), pl.BlockSpec((1, d), fixed)],
        out_specs=pl.BlockSpec((tm, d), row),
        out_shape=jax.ShapeDtypeStruct((t, d), F32),
        compiler_params=_params("parallel"),
        name="embed_ln",
    )(x, g, b)


PROJ_CHUNK = 512
PROJ_TM = 512
LANE_SLABS = GROUP_WIDTH_A // 128


def _proj_kernel(x_ref, w_ref, qkv0_ref, cm1_ref, cm2_ref, mla_ref, qc_ref, gl_ref, slab_ref):
    xb = x_ref[...].astype(BF16)

    def chunk(col, width):
        return jnp.dot(xb, w_ref[:, col:col + width], preferred_element_type=F32)

    col = 0
    for c in range(0, SEG_GROUP, PROJ_CHUNK):
        qkv0_ref[:, c:c + PROJ_CHUNK] = chunk(col + c, PROJ_CHUNK).astype(BF16)
    col += SEG_GROUP
    for ref, (_, d) in ((cm1_ref, DIL_GROUPS[1]), (cm2_ref, DIL_GROUPS[2])):
        n = PROJ_TM // d
        for c in range(0, SEG_GROUP, GROUP_WIDTH_A):
            res = chunk(col + c, GROUP_WIDTH_A)
            for s in range(LANE_SLABS):
                slab_ref[s] = res[:, s * 128:(s + 1) * 128]
            for r in range(d):
                piece = jnp.concatenate([slab_ref[s, pl.ds(r, n, stride=d), :] for s in range(LANE_SLABS)], axis=1)
                ref[0, r, :, c:c + GROUP_WIDTH_A] = piece.astype(BF16)
        col += SEG_GROUP
    for ref, width in ((mla_ref, SEG_MLA), (qc_ref, SEG_QC), (gl_ref, SEG_GL)):
        for c in range(0, width, PROJ_CHUNK):
            w = min(PROJ_CHUNK, width - c)
            ref[:, c:c + w] = chunk(col + c, w).astype(BF16)
        col += width


def _project(x, row_offset, w, batch, seq):
    d = x.shape[1]
    t = batch * seq
    tm = PROJ_TM
    tiles_per_seq = seq // tm
    assert row_offset % tm == 0
    row = lambda i: (i, 0)
    x_row = lambda i: (row_offset // tm + i, 0)

    def class_major(dil):
        shape = (batch, dil, seq // dil, SEG_GROUP)
        spec = pl.BlockSpec((1, dil, tm // dil, SEG_GROUP), lambda i: (i // tiles_per_seq, 0, i % tiles_per_seq, 0))
        return jax.ShapeDtypeStruct(shape, BF16), spec

    (cm1_shape, cm1_spec), (cm2_shape, cm2_spec) = class_major(DIL_GROUPS[1][1]), class_major(DIL_GROUPS[2][1])
    flat = (SEG_GROUP, SEG_MLA, SEG_QC, SEG_GL)
    flat_shapes = [jax.ShapeDtypeStruct((t, n), BF16) for n in flat]
    flat_specs = [pl.BlockSpec((tm, n), row) for n in flat]
    return pl.pallas_call(
        _proj_kernel,
        grid=(t // tm,),
        in_specs=[pl.BlockSpec((tm, d), x_row), _resident((d, N_PROJ), lambda i: (0, 0))],
        out_specs=[flat_specs[0], cm1_spec, cm2_spec] + flat_specs[1:],
        out_shape=[flat_shapes[0], cm1_shape, cm2_shape] + flat_shapes[1:],
        scratch_shapes=[pltpu.VMEM((LANE_SLABS, tm, 128), F32)],
        compiler_params=_params("parallel"),
        name="in_proj",
    )(x, w)


BAND_TQ = 512
BAND_QB = 128
BAND_KB = BAND_QB + 2 * RADIUS_A


def _band_kernel(q_ref, kp_ref, km_ref, kn_ref, vp_ref, vm_ref, vn_ref, o_ref, lse_ref, k_scr, v_scr,
                 *, seq_len, slopes):
    i = pl.program_id(1)
    r = RADIUS_A
    k_scr[0:r, :] = kp_ref[...]
    k_scr[r:r + BAND_TQ, :] = km_ref[...]
    k_scr[r + BAND_TQ:, :] = kn_ref[...]
    v_scr[0:r, :] = vp_ref[...]
    v_scr[r:r + BAND_TQ, :] = vm_ref[...]
    v_scr[r + BAND_TQ:, :] = vn_ref[...]

    row = lax.broadcasted_iota(jnp.int32, (BAND_QB, BAND_KB), 0)
    col = lax.broadcasted_iota(jnp.int32, (BAND_QB, BAND_KB), 1)
    rel = col - r - row
    dist = jnp.abs(rel).astype(F32)
    in_band = jnp.abs(rel) <= r
    scale = HEAD_DIM_A ** -0.5

    for qb in range(BAND_TQ // BAND_QB):
        key_pos = i * BAND_TQ + qb * BAND_QB - r + col
        valid = in_band & (key_pos >= 0) & (key_pos < seq_len)
        for h in range(HEADS_A):
            lanes = slice(h * HEAD_DIM_A, (h + 1) * HEAD_DIM_A)
            q = q_ref[qb * BAND_QB:(qb + 1) * BAND_QB, lanes]
            k = k_scr[qb * BAND_QB:qb * BAND_QB + BAND_KB, lanes]
            v = v_scr[qb * BAND_QB:qb * BAND_QB + BAND_KB, lanes]
            s = lax.dot_general(q, k, NT_DIMS, preferred_element_type=F32)
            logits = jnp.where(valid, s * scale - slopes[h] * dist, NEG_BIG)
            m = jnp.max(logits, axis=1, keepdims=True)
            p = jnp.exp(logits - m)
            den = jnp.sum(p, axis=1, keepdims=True)
            o = jnp.dot(p.astype(BF16), v, preferred_element_type=F32) / den
            rows = slice(qb * BAND_QB, (qb + 1) * BAND_QB)
            o_ref[rows, lanes] = o.astype(BF16)
            lse_ref[rows, lanes] = jnp.broadcast_to(m + jnp.log(den), (BAND_QB, HEAD_DIM_A))


def _band_attention(q_src, k_src, v_src, n_seq, seq_len, slopes):
    tq, r = BAND_TQ, RADIUS_A
    assert seq_len % tq == 0 and tq % r == 0
    steps = seq_len // tq
    halo_per_tile = tq // r
    halo_blocks = seq_len // r

    def main_map(cb):
        return lambda n, i: (n * steps + i, cb)

    def prev_map(cb):
        return lambda n, i: (n * halo_blocks + jnp.maximum(i * halo_per_tile - 1, 0), cb)

    def next_map(cb):
        return lambda n, i: (n * halo_blocks + jnp.minimum((i + 1) * halo_per_tile, halo_blocks - 1), cb)

    (qa, qcb), (ka, kcb), (va, vcb) = q_src, k_src, v_src
    w = GROUP_WIDTH_A
    rows = n_seq * seq_len
    return pl.pallas_call(
        functools.partial(_band_kernel, seq_len=seq_len, slopes=slopes),
        grid=(n_seq, steps),
        in_specs=[
            pl.BlockSpec((tq, w), main_map(qcb)),
            pl.BlockSpec((r, w), prev_map(kcb)), pl.BlockSpec((tq, w), main_map(kcb)),
            pl.BlockSpec((r, w), next_map(kcb)),
            pl.BlockSpec((r, w), prev_map(vcb)), pl.BlockSpec((tq, w), main_map(vcb)),
            pl.BlockSpec((r, w), next_map(vcb)),
        ],
        out_specs=[pl.BlockSpec((tq, w), main_map(0)), pl.BlockSpec((tq, w), main_map(0))],
        out_shape=[jax.ShapeDtypeStruct((rows, w), BF16), jax.ShapeDtypeStruct((rows, w), F32)],
        scratch_shapes=[pltpu.VMEM((tq + 2 * r, w), BF16), pltpu.VMEM((tq + 2 * r, w), BF16)],
        compiler_params=_params("parallel", "parallel"),
        name="band_attention",
    )(qa, ka, ka, ka, va, va, va)


def _alibi_slopes():
    n = N_GROUPS_A * HEADS_A
    return [2.0 ** (-8.0 * (i + 1) / n) for i in range(n)]


def _dilated_mixer(group_qkv, batch, seq):
    slopes = _alibi_slopes()
    outs = []
    for g, (_, d) in enumerate(DIL_GROUPS):
        group_slopes = tuple(float(s * d) for s in slopes[g * HEADS_A:(g + 1) * HEADS_A])
        rows = group_qkv[g].reshape(batch * seq, SEG_GROUP)
        o, lse = _band_attention((rows, 0), (rows, 1), (rows, 2), batch * d, seq // d, group_slopes)
        if d > 1:
            o = o.reshape(batch, d, seq // d, GROUP_WIDTH_A)
            lse = lse.reshape(batch, d, seq // d, GROUP_WIDTH_A)
        outs.append((o, lse))
    return outs


V_ROWS_B = 80
MLA_TQ = 1024
MLA_TK = 2048
MLA_KS = 256
MLA_QS = 512
MLA_LOOKAHEAD = 2
MLA_GAP_LIMIT = 64.0
MLA_HEADS_PER_STEP = 2


def _mla_prep_kernel(mla_ref, cos_ref, sin_ref, cos_t_ref, sin_t_ref, gq_ref, wqa_ref, wqb_ref, gkv_ref,
                     wk_ref, wv_ref, qt_ref, k_ref, vt_ref):
    m = mla_ref[...]
    cq = m[:, 0:Q_LORA].astype(F32)
    ckv = m[:, Q_LORA:Q_LORA + KV_LORA].astype(F32)
    kr = m[:, Q_LORA + KV_LORA:Q_LORA + KV_LORA + HEAD_PAD_B].astype(F32)
    kr_rot = m[:, Q_LORA + KV_LORA + HEAD_PAD_B:].astype(F32)
    scale = (QK_NOPE + QK_ROPE) ** -0.5 * math.log2(math.e)

    cqn = _rms_norm_rows(cq, gq_ref[...]).astype(BF16)
    qa_t = lax.dot_general(wqa_ref[...], cqn, NT_DIMS, preferred_element_type=F32)
    qb_t = lax.dot_general(wqb_ref[...], cqn, NT_DIMS, preferred_element_type=F32)
    ckvn = _rms_norm_rows(ckv, gkv_ref[...]).astype(BF16)
    kn = jnp.dot(ckvn, wk_ref[...], preferred_element_type=F32)
    v_t = lax.dot_general(wv_ref[...], ckvn, NT_DIMS, preferred_element_type=F32)
    k_rope = kr * cos_ref[...] + kr_rot * sin_ref[...]
    cos_t, sin_t = cos_t_ref[...], sin_t_ref[...]
    row = lax.broadcasted_iota(jnp.int32, (V_ROWS_B, 1), 0)
    ones_row = (row == V_DIM_B).astype(F32)
    for h in range(HEADS_B):
        slot = slice(h * HEAD_PAD_B, (h + 1) * HEAD_PAD_B)
        qt_ref[slot, :] = ((qa_t[slot, :] * cos_t + qb_t[slot, :] * sin_t) * scale).astype(BF16)
        k_ref[:, slot] = (kn[:, slot] + k_rope).astype(BF16)
        vt_ref[0, h, 0] = (v_t[h * V_ROWS_B:(h + 1) * V_ROWS_B, :] + ones_row).astype(BF16)


def _mla_prep(mla, tables, gq, wqa_t, wqb_t, gkv, wk, wv_t, batch, seq):
    t = mla.shape[0]
    tm = MLA_TK
    chunks = seq // tm
    cos, sin, cos_t, sin_t = tables
    row = lambda i: (i, 0)
    pos = lambda i: (i % chunks, 0)
    pos_t = lambda i: (0, i % chunks)
    fixed = lambda i: (0, 0)
    wide = HEADS_B * HEAD_PAD_B
    return pl.pallas_call(
        _mla_prep_kernel,
        grid=(t // tm,),
        in_specs=[pl.BlockSpec((tm, SEG_MLA), row),
                  pl.BlockSpec((tm, HEAD_PAD_B), pos), pl.BlockSpec((tm, HEAD_PAD_B), pos),
                  pl.BlockSpec((HEAD_PAD_B, tm), pos_t), pl.BlockSpec((HEAD_PAD_B, tm), pos_t),
                  pl.BlockSpec((1, Q_LORA), fixed), pl.BlockSpec((wide, Q_LORA), fixed),
                  pl.BlockSpec((wide, Q_LORA), fixed),
                  pl.BlockSpec((1, KV_LORA), fixed), pl.BlockSpec((KV_LORA, wide), fixed),
                  pl.BlockSpec((HEADS_B * V_ROWS_B, KV_LORA), fixed)],
        out_specs=[pl.BlockSpec((wide, tm), lambda i: (i // chunks, i % chunks)),
                   pl.BlockSpec((tm, wide), row),
                   pl.BlockSpec((1, HEADS_B, 1, V_ROWS_B, tm), lambda i: (i // chunks, 0, i % chunks, 0, 0))],
        out_shape=[jax.ShapeDtypeStruct((batch * wide, seq), BF16),
                   jax.ShapeDtypeStruct((t, wide), BF16),
                   jax.ShapeDtypeStruct((batch, HEADS_B, chunks, V_ROWS_B, tm), BF16)],
        compiler_params=_params("parallel"),
        name="mla_prep",
    )(mla, cos, sin, cos_t, sin_t, gq, wqa_t, wqb_t, gkv, wk, wv_t)


def _mla_attn_kernel(qt_ref, k_ref, vt_ref, o_ref, *, seq):
    n_sub = MLA_TK // MLA_KS
    nq = MLA_TQ // MLA_QS
    units = [(c, hh, j) for c in range(n_sub) for hh in range(MLA_HEADS_PER_STEP) for j in range(nq)]

    def scores(rows, hh, j):
        slot = slice(hh * HEAD_PAD_B, (hh + 1) * HEAD_PAD_B)
        return jnp.dot(k_ref[rows, slot], qt_ref[slot, j * MLA_QS:(j + 1) * MLA_QS],
                       preferred_element_type=F32)

    def sweep(kc, carry, update):
        def unit_scores(u):
            c, hh, j = units[u]
            return scores(pl.ds(pl.multiple_of(kc * MLA_TK + c * MLA_KS, MLA_KS), MLA_KS), hh, j)

        new = list(carry)
        pending = [unit_scores(u) for u in range(min(MLA_LOOKAHEAD, len(units)))]
        worst = None
        for u, (c, hh, j) in enumerate(units):
            if u + MLA_LOOKAHEAD < len(units):
                pending.append(unit_scores(u + MLA_LOOKAHEAD))
            v_blk = vt_ref[0, hh, kc, :, c * MLA_KS:(c + 1) * MLA_KS]
            new[hh * nq + j], gap = update(new[hh * nq + j], pending.pop(0), v_blk)
            worst = gap if worst is None else jnp.maximum(worst, gap)
        return tuple(new), worst

    def exact_update(state, s, v_blk):
        m, acc = state
        m_new = jnp.maximum(m, jnp.max(s, axis=0, keepdims=True))
        p = jnp.exp2(s - m_new).astype(BF16)
        acc = jnp.exp2(m - m_new) * acc + jnp.dot(v_blk, p, preferred_element_type=F32)
        return (m_new, acc), jnp.zeros_like(m)

    def lagged_update(state, s, v_blk):
        m, acc = state
        p = jnp.exp2(s - m).astype(BF16)
        col_max = jnp.max(s, axis=0, keepdims=True)
        gap = col_max - m
        m_new = jnp.maximum(m, col_max)
        acc = (acc + jnp.dot(v_blk, p, preferred_element_type=F32)) * jnp.exp2(m - m_new)
        return (m_new, acc), gap

    def step(kc, carry):
        fast, worst = sweep(kc, carry, lagged_update)
        overflow_risk = jnp.max(worst) > MLA_GAP_LIMIT
        return lax.cond(overflow_risk, lambda: sweep(kc, carry, exact_update)[0], lambda: fast)

    first = pl.ds(0, MLA_KS)
    init = tuple((jnp.max(scores(first, hh, j), axis=0, keepdims=True), jnp.zeros((V_ROWS_B, MLA_QS), F32))
                 for hh in range(MLA_HEADS_PER_STEP) for j in range(nq))
    final = lax.fori_loop(0, seq // MLA_TK, step, init)
    parts = [acc[0:V_DIM_B, :] / acc[V_DIM_B:V_DIM_B + 1, :] for _, acc in final]
    heads = [jnp.concatenate(parts[hh * nq:(hh + 1) * nq], axis=1) for hh in range(MLA_HEADS_PER_STEP)]
    o_ref[...] = jnp.concatenate(heads, axis=0).T.astype(BF16)


def _mla_attention(q_t, k, v_t, batch, seq):
    t = batch * seq
    tq = MLA_TQ
    steps = seq // tq
    chunks = seq // MLA_TK
    pairs = HEADS_B // MLA_HEADS_PER_STEP
    pair = MLA_HEADS_PER_STEP * HEAD_PAD_B
    return pl.pallas_call(
        functools.partial(_mla_attn_kernel, seq=seq),
        grid=(batch, pairs, steps),
        in_specs=[pl.BlockSpec((pair, tq), lambda b, hp, i: (b * pairs + hp, i)),
                  _resident((seq, pair), lambda b, hp, i: (b, hp)),
                  _resident((1, MLA_HEADS_PER_STEP, chunks, V_ROWS_B, MLA_TK), lambda b, hp, i: (b, hp, 0, 0, 0))],
        out_specs=pl.BlockSpec((tq, MLA_HEADS_PER_STEP * V_DIM_B), lambda b, hp, i: (b * steps + i, hp)),
        out_shape=jax.ShapeDtypeStruct((t, HEADS_B * V_DIM_B), BF16),
        compiler_params=_params("parallel", "parallel", "arbitrary"),
        name="mla_attention",
    )(q_t, k, v_t)


def _rope_tables(seq):
    inv_freq = 1.0 / (ROPE_THETA ** (jnp.arange(0, QK_ROPE, 2, dtype=F32) / QK_ROPE))
    ang = jnp.arange(seq, dtype=F32)[:, None] * inv_freq[None, :]
    cos, sin = jnp.cos(ang), jnp.sin(ang)
    pad = HEAD_PAD_B - QK_NOPE - QK_ROPE
    cos_s = jnp.concatenate([jnp.ones((seq, QK_NOPE), F32), cos, cos, jnp.zeros((seq, pad), F32)], axis=1)
    sin_s = jnp.concatenate([jnp.zeros((seq, QK_NOPE), F32), sin, sin, jnp.zeros((seq, pad), F32)], axis=1)
    return cos_s, sin_s, cos_s.T, sin_s.T


def _mem_kv_kernel(mem_ref, w_ref, o_ref):
    o_ref[...] = jnp.dot(mem_ref[...], w_ref[...], preferred_element_type=F32).astype(BF16)


def _mem_kv(mem_b, w):
    rows, d = mem_b.shape
    n = w.shape[1]
    tm = 256
    return pl.pallas_call(
        _mem_kv_kernel,
        grid=(rows // tm,),
        in_specs=[pl.BlockSpec((tm, d), lambda i: (i, 0)), pl.BlockSpec((d, n), lambda i: (0, 0))],
        out_specs=pl.BlockSpec((tm, n), lambda i: (i, 0)),
        out_shape=jax.ShapeDtypeStruct((rows, n), BF16),
        compiler_params=_params("parallel"),
        name="mem_kv",
    )(mem_b, w)


def _mem_attn_kernel(q_ref, kv_ref, o_ref):
    scale = HEAD_DIM_C ** -0.5
    for h in range(HEADS_C):
        lanes = slice(h * HEAD_DIM_C, (h + 1) * HEAD_DIM_C)
        k = kv_ref[:, lanes]
        v = kv_ref[:, OUT_C + h * HEAD_DIM_C:OUT_C + (h + 1) * HEAD_DIM_C]
        s = lax.dot_general(q_ref[:, lanes], k, NT_DIMS, preferred_element_type=F32) * scale
        m = jnp.max(s, axis=1, keepdims=True)
        p = jnp.exp(s - m)
        den = jnp.sum(p, axis=1, keepdims=True)
        o_ref[:, lanes] = (jnp.dot(p.astype(BF16), v, preferred_element_type=F32) / den).astype(BF16)


def _mem_attention(qc, kv, batch, seq, n_mem):
    t = batch * seq
    ts = 1024
    steps = seq // ts
    return pl.pallas_call(
        _mem_attn_kernel,
        grid=(batch, steps),
        in_specs=[pl.BlockSpec((ts, OUT_C), lambda b, i: (b * steps + i, 0)),
                  pl.BlockSpec((n_mem, 2 * OUT_C), lambda b, i: (b, 0))],
        out_specs=pl.BlockSpec((ts, OUT_C), lambda b, i: (b * steps + i, 0)),
        out_shape=jax.ShapeDtypeStruct((t, OUT_C), BF16),
        compiler_params=_params("parallel", "parallel"),
        name="mem_attention",
    )(qc, kv)


def _token_order(cm_ref, slab_ref):
    d, n = cm_ref.shape[1], cm_ref.shape[2]
    for r in range(d):
        blk = cm_ref[0, r].astype(F32)
        for s in range(LANE_SLABS):
            slab_ref[s, pl.ds(r, n, stride=d), :] = blk[:, s * 128:(s + 1) * 128]
    return jnp.concatenate([slab_ref[s] for s in range(LANE_SLABS)], axis=1)


def _merge_kernel(oa0_ref, oa1_ref, oa2_ref, l0_ref, l1_ref, l2_ref, ob_ref, oc_ref, gl_ref, x_ref,
                  wb_ref, wo_ref, g_ref, b_ref, xf_ref, slab_ref):
    l0 = l0_ref[...]
    l1, l2 = _token_order(l1_ref, slab_ref), _token_order(l2_ref, slab_ref)
    m = jnp.maximum(jnp.maximum(l0, l1), l2)
    e0, e1, e2 = jnp.exp(l0 - m), jnp.exp(l1 - m), jnp.exp(l2 - m)
    oa = e0 * oa0_ref[...].astype(F32) + e1 * _token_order(oa1_ref, slab_ref) + e2 * _token_order(oa2_ref, slab_ref)
    oa = (oa / (e0 + e1 + e2)).astype(BF16)
    z = None
    for i, o in enumerate((oa, ob_ref[...], oc_ref[...])):
        gate = jax.nn.sigmoid(gl_ref[:, i * D_MODEL:(i + 1) * D_MODEL].astype(F32))
        term = gate * jnp.dot(o, wb_ref[i], preferred_element_type=F32)
        z = term if z is None else z + term
    y = jnp.dot(z.astype(BF16), wo_ref[...], preferred_element_type=F32)
    xf_ref[...] = _layer_norm_rows(DEEPNORM_ALPHA * x_ref[...] + y, g_ref[...], b_ref[...])


def _merge(oa, ob, oc, gl, x, row_offset, wb, wo, g, b, batch, seq):
    d = x.shape[1]
    t = batch * seq
    tm = PROJ_TM
    tiles_per_seq = seq // tm
    assert row_offset % tm == 0
    row = lambda i: (i, 0)
    x_row = lambda i: (row_offset // tm + i, 0)
    fixed = lambda i: (0, 0)
    half = pl.BlockSpec((tm, BRANCH_WIDTH), row)

    def class_major(dil):
        return pl.BlockSpec((1, dil, tm // dil, GROUP_WIDTH_A),
                            lambda i: (i // tiles_per_seq, 0, i % tiles_per_seq, 0))

    cm1, cm2 = class_major(DIL_GROUPS[1][1]), class_major(DIL_GROUPS[2][1])
    (oa0, l0), (oa1, l1), (oa2, l2) = oa
    return pl.pallas_call(
        _merge_kernel,
        grid=(t // tm,),
        in_specs=[half, cm1, cm2, half, cm1, cm2, half, half,
                  pl.BlockSpec((tm, SEG_GL), row), pl.BlockSpec((tm, d), x_row),
                  _resident((N_BRANCH, BRANCH_WIDTH, d), lambda i: (0, 0, 0)),
                  _resident((d, d), fixed),
                  pl.BlockSpec((1, d), fixed), pl.BlockSpec((1, d), fixed)],
        out_specs=pl.BlockSpec((tm, d), row),
        out_shape=jax.ShapeDtypeStruct((t, d), F32),
        scratch_shapes=[pltpu.VMEM((LANE_SLABS, tm, 128), F32)],
        compiler_params=_params("parallel"),
        name="merge_ln1",
    )(oa0, oa1, oa2, l0, l1, l2, ob, oc, gl, x, wb, wo, g, b)


GATE_LANES = 128


def _first_index_of_max(vals, idx, axis, sentinel):
    mx = jnp.max(vals, axis=axis, keepdims=True)
    return jnp.min(jnp.where(vals == mx, idx, sentinel), axis=axis, keepdims=True)


def _router_kernel(x_ref, w_ref, bias_ref, xg_ref, code_ref):
    tm = x_ref.shape[0]
    x = x_ref[...]
    logits = lax.dot_general(w_ref[...], x.astype(BF16), NT_DIMS, preferred_element_type=F32)
    scores = jax.nn.sigmoid(logits)
    choice = scores + bias_ref[...]
    neg = -jnp.inf

    c3 = choice.reshape(N_EXPERT_GROUPS, EXPERTS_PER_GROUP, tm)
    e_idx = lax.broadcasted_iota(jnp.int32, c3.shape, 1)
    first = jnp.max(c3, axis=1, keepdims=True)
    first_at = jnp.min(jnp.where(c3 == first, e_idx, EXPERTS_PER_GROUP), axis=1, keepdims=True)
    second = jnp.max(jnp.where(e_idx == first_at, neg, c3), axis=1, keepdims=True)
    group_score = (first + second).reshape(N_EXPERT_GROUPS, tm)

    g_idx = lax.broadcasted_iota(jnp.int32, group_score.shape, 0)
    group_sel = jnp.zeros(group_score.shape, jnp.bool_)
    for _ in range(TOPK_GROUPS):
        at = _first_index_of_max(group_score, g_idx, 0, N_EXPERT_GROUPS)
        hit = g_idx == at
        group_sel = group_sel | hit
        group_score = jnp.where(hit, neg, group_score)

    allowed = jnp.broadcast_to(group_sel.reshape(N_EXPERT_GROUPS, 1, tm), c3.shape).reshape(N_EXPERTS, tm)
    cand = jnp.where(allowed, choice, neg)
    x_idx = lax.broadcasted_iota(jnp.int32, cand.shape, 0)
    chosen = jnp.zeros(cand.shape, jnp.bool_)
    for _ in range(TOP_K):
        at = _first_index_of_max(cand, x_idx, 0, N_EXPERTS)
        hit = x_idx == at
        chosen = chosen | hit
        cand = jnp.where(hit, neg, cand)

    w_sel = jnp.where(chosen, scores, 0.0)
    gates = w_sel / jnp.sum(w_sel, axis=0, keepdims=True) * ROUTED_SCALE
    padded = jnp.concatenate([gates, jnp.zeros((GATE_LANES - N_EXPERTS, tm), F32)], axis=0)
    xg_ref[:, 0:D_MODEL] = x
    xg_ref[:, D_MODEL:] = padded.T
    bit = jnp.left_shift(1, g_idx)
    code_ref[...] = jnp.sum(jnp.where(group_sel, bit, 0), axis=0, keepdims=True)


def _router(x, w_t, bias):
    t, d = x.shape
    tm = 1024
    return pl.pallas_call(
        _router_kernel,
        grid=(t // tm,),
        in_specs=[pl.BlockSpec((tm, d), lambda i: (i, 0)), pl.BlockSpec((N_EXPERTS, d), lambda i: (0, 0)),
                  pl.BlockSpec((N_EXPERTS, 1), lambda i: (0, 0))],
        out_specs=[pl.BlockSpec((tm, XG_WIDTH), lambda i: (i, 0)), pl.BlockSpec((1, tm), lambda i: (0, i))],
        out_shape=[jax.ShapeDtypeStruct((t, XG_WIDTH), F32), jax.ShapeDtypeStruct((1, t), jnp.int32)],
        compiler_params=_params("parallel"),
        name="router",
    )(x, w_t, bias)


XG_WIDTH = D_MODEL + GATE_LANES
MOE_TM = 1024
MOE_SUB = 128
MOE_NSUB = MOE_TM // MOE_SUB
TOKEN_BITS = 16


def _dispatch_plan(code):
    t = code.shape[1]
    tiles = t // MOE_TM
    assert t <= 1 << TOKEN_BITS
    key = jnp.sort(code[0] * (1 << TOKEN_BITS) + jnp.arange(t, dtype=jnp.int32))
    perm = key & ((1 << TOKEN_BITS) - 1)
    bits = ((key >> TOKEN_BITS)[:, None] >> jnp.arange(N_EXPERT_GROUPS, dtype=jnp.int32)[None, :]) & 1
    sub = jnp.max(bits.reshape(tiles, MOE_NSUB, MOE_SUB, N_EXPERT_GROUPS), axis=2)
    tile_active = jnp.max(sub, axis=1)
    n_active = jnp.sum(tile_active, axis=1).astype(jnp.int32)
    order = jnp.argsort(1 - tile_active, axis=1, stable=True).astype(jnp.int32)
    step = jnp.arange(N_EXPERT_GROUPS, dtype=jnp.int32)[None, :]
    last = jnp.take_along_axis(order, jnp.maximum(n_active - 1, 0)[:, None], axis=1)
    groups = jnp.where(step < n_active[:, None], order, last)
    flags = jnp.transpose(sub, (0, 2, 1)).astype(jnp.int32)
    return perm.reshape(tiles, 1, MOE_TM), groups.reshape(-1), n_active, flags.reshape(-1)


def _row_dma(src_ref, dst_ref, src_row, dst_row, sem):
    return pltpu.make_async_copy(src_ref.at[pl.ds(src_row, 1)], dst_ref.at[pl.ds(dst_row, 1)], sem)


DMA_THREADS = 2


def _gather_rows_kernel(perm_ref, src_ref, out_ref, sem):
    def start(q, carry):
        for u in range(DMA_THREADS):
            r = q * DMA_THREADS + u
            _row_dma(src_ref, out_ref, perm_ref[0, 0, r], r, sem).start(priority=u)
        return carry

    lax.fori_loop(0, MOE_TM // DMA_THREADS, start, 0)
    pltpu.make_async_copy(src_ref.at[pl.ds(0, MOE_TM)], out_ref, sem).wait()


def _gather_rows(src, perm):
    t, width = src.shape
    return pl.pallas_call(
        _gather_rows_kernel,
        grid=(t // MOE_TM,),
        in_specs=[pl.BlockSpec((1, 1, MOE_TM), lambda i: (i, 0, 0), memory_space=pltpu.SMEM),
                  pl.BlockSpec(memory_space=pl.ANY)],
        out_specs=pl.BlockSpec((MOE_TM, width), lambda i: (i, 0)),
        out_shape=jax.ShapeDtypeStruct((t, width), src.dtype),
        scratch_shapes=[pltpu.SemaphoreType.DMA(())],
        compiler_params=_params("arbitrary"),
        name="moe_gather",
    )(perm, src)


def _scatter_rows_kernel(perm_ref, src_ref, out_ref, sem):
    def start(q, carry):
        for u in range(DMA_THREADS):
            r = q * DMA_THREADS + u
            _row_dma(src_ref, out_ref, r, perm_ref[0, 0, r], sem).start(priority=u)
        return carry

    lax.fori_loop(0, MOE_TM // DMA_THREADS, start, 0)
    pltpu.make_async_copy(src_ref, out_ref.at[pl.ds(0, MOE_TM)], sem).wait()


def _scatter_rows(src, perm):
    t, width = src.shape
    return pl.pallas_call(
        _scatter_rows_kernel,
        grid=(t // MOE_TM,),
        in_specs=[pl.BlockSpec((1, 1, MOE_TM), lambda i: (i, 0, 0), memory_space=pltpu.SMEM),
                  pl.BlockSpec((MOE_TM, width), lambda i: (i, 0))],
        out_specs=pl.BlockSpec(memory_space=pl.ANY),
        out_shape=jax.ShapeDtypeStruct((t, width), src.dtype),
        scratch_shapes=[pltpu.SemaphoreType.DMA(())],
        compiler_params=_params("arbitrary"),
        name="moe_scatter",
    )(perm, src)


def _swiglu(xb, wg, wu):
    return jax.nn.silu(jnp.dot(xb, wg, preferred_element_type=F32)) * jnp.dot(xb, wu, preferred_element_type=F32)


def _moe_kernel(groups_ref, nact_ref, flags_ref, xg_ref, wg_ref, wu_ref, wd_ref, sg_ref, su_ref, sd_ref,
                g_ref, b_ref, o_ref, acc_ref, xb_ref):
    i = pl.program_id(0)
    j = pl.program_id(1)

    @pl.when(j == 0)
    def _():
        xb = xg_ref[:, 0:D_MODEL].astype(BF16)
        xb_ref[...] = xb
        h = _swiglu(xb, sg_ref[...], su_ref[...])
        acc_ref[...] = jnp.dot(h.astype(BF16), sd_ref[...], preferred_element_type=F32)

    @pl.when(j < nact_ref[i])
    def _():
        group = groups_ref[i * N_EXPERT_GROUPS + j]
        wd_all = wd_ref[...].reshape(EXPERTS_PER_GROUP * D_EXPERT, D_MODEL)
        for s in range(MOE_NSUB):
            @pl.when(flags_ref[(i * N_EXPERT_GROUPS + group) * MOE_NSUB + s] != 0)
            def _():
                rows = slice(s * MOE_SUB, (s + 1) * MOE_SUB)
                xb = xb_ref[rows, :]
                gates = xg_ref[rows, D_MODEL:]
                lane = lax.broadcasted_iota(jnp.int32, gates.shape, 1)
                hs = []
                for e in range(EXPERTS_PER_GROUP):
                    gate = jnp.sum(jnp.where(lane == group * EXPERTS_PER_GROUP + e, gates, 0.0),
                                   axis=1, keepdims=True)
                    hs.append((_swiglu(xb, wg_ref[e], wu_ref[e]) * gate).astype(BF16))
                acc_ref[rows, :] += jnp.dot(jnp.concatenate(hs, axis=1), wd_all, preferred_element_type=F32)

    @pl.when(j == pl.num_programs(1) - 1)
    def _():
        o_ref[...] = _layer_norm_rows(DEEPNORM_ALPHA * xg_ref[:, 0:D_MODEL] + acc_ref[...], g_ref[...], b_ref[...])


def _moe(xg_sorted, plan, wg, wu, wd, sg, su, sd, g, b):
    t = xg_sorted.shape[0]
    d, f, tm, ng = D_MODEL, D_EXPERT, MOE_TM, N_EXPERT_GROUPS
    _, groups, n_active, flags = plan
    row = lambda i, j, *_: (i, 0)
    fixed = lambda i, j, *_: (0, 0)
    expert_block = lambda i, j, groups_ref, *_: (groups_ref[i * ng + j], 0, 0)
    grid_spec = pltpu.PrefetchScalarGridSpec(
        num_scalar_prefetch=3,
        grid=(t // tm, ng),
        in_specs=[pl.BlockSpec((tm, XG_WIDTH), row),
                  pl.BlockSpec((EXPERTS_PER_GROUP, d, f), expert_block),
                  pl.BlockSpec((EXPERTS_PER_GROUP, d, f), expert_block),
                  pl.BlockSpec((EXPERTS_PER_GROUP, f, d), expert_block),
                  pl.BlockSpec((d, f), fixed), pl.BlockSpec((d, f), fixed), pl.BlockSpec((f, d), fixed),
                  pl.BlockSpec((1, d), fixed), pl.BlockSpec((1, d), fixed)],
        out_specs=pl.BlockSpec((tm, d), row),
        scratch_shapes=[pltpu.VMEM((tm, d), F32), pltpu.VMEM((tm, d), BF16)],
    )
    return pl.pallas_call(
        _moe_kernel,
        grid_spec=grid_spec,
        out_shape=jax.ShapeDtypeStruct((t, d), F32),
        compiler_params=_params("arbitrary", "arbitrary"),
        name="moe_ln2",
    )(groups, n_active, flags, xg_sorted, wg, wu, wd, sg, su, sd, g, b)


def _moe_layer(x, w_router_t, router_bias, wg, wu, wd, sg, su, sd, g, b):
    xg, code = _router(x, w_router_t, router_bias.reshape(N_EXPERTS, 1))
    plan = _dispatch_plan(code)
    perm = plan[0]
    y_sorted = _moe(_gather_rows(xg, perm), plan, wg, wu, wd, sg, su, sd, g, b)
    return _scatter_rows(y_sorted, perm)


def _rotate_half_columns(w):
    half = QK_ROPE // 2
    return jnp.concatenate([-w[..., half:], w[..., :half]], axis=-1)


def _prep_in_proj(w_in):
    layers, d, _ = w_in.shape
    cuts = np.cumsum((WIDTH_A, WIDTH_A, WIDTH_A, Q_LORA, KV_LORA, QK_ROPE, OUT_C))
    qa, ka, va, cq, ckv, kr, qc, gl = jnp.split(w_in, [int(c) for c in cuts], axis=-1)
    lead = jnp.zeros((layers, d, QK_NOPE), w_in.dtype)
    tail = jnp.zeros((layers, d, HEAD_PAD_B - QK_NOPE - QK_ROPE), w_in.dtype)
    kr_slot = jnp.concatenate([lead, kr, tail], axis=-1)
    kr_rot_slot = jnp.concatenate([lead, _rotate_half_columns(kr), tail], axis=-1)
    per_group = [m[..., g * GROUP_WIDTH_A:(g + 1) * GROUP_WIDTH_A] for g in range(N_GROUPS_A) for m in (qa, ka, va)]
    return jnp.concatenate(per_group + [cq, ckv, kr_slot, kr_rot_slot, qc, gl], axis=-1).astype(BF16)


def _prep_mla_weights(w_q_up, w_kv_up):
    layers = w_q_up.shape[0]
    wq = w_q_up.reshape(layers, Q_LORA, HEADS_B, QK_NOPE + QK_ROPE)
    nope, rope = wq[..., :QK_NOPE], wq[..., QK_NOPE:]
    pad = HEAD_PAD_B - QK_NOPE - QK_ROPE
    zq = lambda n: jnp.zeros((layers, Q_LORA, HEADS_B, n), w_q_up.dtype)
    wqa = jnp.concatenate([nope, rope, zq(pad)], axis=-1)
    wqb = jnp.concatenate([zq(QK_NOPE), _rotate_half_columns(rope), zq(pad)], axis=-1)
    wkv = w_kv_up.reshape(layers, KV_LORA, HEADS_B, QK_NOPE + V_DIM_B)
    zk = lambda n: jnp.zeros((layers, KV_LORA, HEADS_B, n), w_kv_up.dtype)
    wk = jnp.concatenate([wkv[..., :QK_NOPE], zk(HEAD_PAD_B - QK_NOPE)], axis=-1)
    wv = jnp.concatenate([wkv[..., QK_NOPE:], zk(V_ROWS_B - V_DIM_B)], axis=-1)
    flat = lambda w: w.reshape(layers, w.shape[1], -1).astype(BF16)
    flat_t = lambda w: jnp.swapaxes(flat(w), 1, 2)
    return flat_t(wqa), flat_t(wqb), flat(wk), flat_t(wv)


def _encoder(groups, emb_g, emb_b, w, depth):
    d = groups[0][0].shape[-1]
    row2 = lambda v: v.reshape(1, -1)
    sizes = [x.shape[0] * x.shape[1] for x, _ in groups]
    offsets = [sum(sizes[:i]) for i in range(len(sizes))]
    xf = _embed_ln(jnp.concatenate([x.reshape(-1, d) for x, _ in groups], axis=0), row2(emb_g), row2(emb_b))
    mems = [mem.reshape(-1, d).astype(BF16) for _, mem in groups]
    tables = [_rope_tables(x.shape[1]) for x, _ in groups]
    for l in range(depth):
        mixed = []
        for (x, mem), off, mem_b, table in zip(groups, offsets, mems, tables):
            batch, seq, _ = x.shape
            qkv0, cm1, cm2, mla, qc, gl = _project(xf, off, w["in_proj"][l], batch, seq)
            oa = _dilated_mixer((qkv0, cm1, cm2), batch, seq)
            q_t, k, v_t = _mla_prep(mla, table, row2(w["q_norm_g"][l]), w["wqa"][l], w["wqb"][l],
                                    row2(w["kv_norm_g"][l]), w["wk"][l], w["wv"][l], batch, seq)
            ob = _mla_attention(q_t, k, v_t, batch, seq)
            oc = _mem_attention(qc, _mem_kv(mem_b, w["mem_kv"][l]), batch, seq, mem.shape[1])
            mixed.append(_merge(oa, ob, oc, gl, xf, off, w["branch"][l], w["out"][l],
                                row2(w["ln1_g"][l]), row2(w["ln1_b"][l]), batch, seq))
        xf = _moe_layer(jnp.concatenate(mixed, axis=0), w["router_t"][l], w["router_bias"][l], w["exp_gate"][l],
                        w["exp_up"][l], w["exp_down"][l], w["sh_gate"][l], w["sh_up"][l], w["sh_down"][l],
                        row2(w["ln2_g"][l]), row2(w["ln2_b"][l]))
    return [xf[off:off + n].reshape(x.shape) for (x, _), off, n in zip(groups, offsets, sizes)]


def kernel(x_prompt, x_sample, mem_prompt, mem_sample, emb_ln_g, emb_ln_b, w_in, q_norm_g, w_q_up, kv_norm_g,
           w_kv_up, w_mem_kv, w_branch, w_out, ln1_g, ln1_b, w_router, router_bias, w_exp_gate, w_exp_up,
           w_exp_down, w_sh_gate, w_sh_up, w_sh_down, ln2_g, ln2_b):
    wqa, wqb, wk, wv = _prep_mla_weights(w_q_up, w_kv_up)
    w = {
        "in_proj": _prep_in_proj(w_in),
        "q_norm_g": q_norm_g, "kv_norm_g": kv_norm_g, "wqa": wqa, "wqb": wqb, "wk": wk, "wv": wv,
        "mem_kv": w_mem_kv.astype(BF16), "branch": w_branch.astype(BF16), "out": w_out.astype(BF16),
        "ln1_g": ln1_g, "ln1_b": ln1_b,
        "router_t": jnp.swapaxes(w_router, 1, 2).astype(BF16), "router_bias": router_bias,
        "exp_gate": w_exp_gate.astype(BF16), "exp_up": w_exp_up.astype(BF16), "exp_down": w_exp_down.astype(BF16),
        "sh_gate": w_sh_gate.astype(BF16), "sh_up": w_sh_up.astype(BF16), "sh_down": w_sh_down.astype(BF16),
        "ln2_g": ln2_g, "ln2_b": ln2_b,
    }
    depth = w_in.shape[0]
    y_prompt, y_sample = _encoder([(x_prompt, mem_prompt), (x_sample, mem_sample)], emb_ln_g, emb_ln_b, w, depth)
    return (y_prompt, y_sample)
```

```python
import functools
import math

import numpy as np
import jax
import jax.numpy as jnp
from jax import lax
from jax.experimental import pallas as pl
from jax.experimental.pallas import tpu as pltpu

F32 = jnp.float32
BF16 = jnp.bfloat16

D_MODEL = 1024
DEPTH = 4
DIL_GROUPS = ((128, 1), (512, 4), (2048, 16))
N_GROUPS_A = 3
HEADS_A = 4
HEAD_DIM_A = 128
GROUP_WIDTH_A = HEADS_A * HEAD_DIM_A
WIDTH_A = N_GROUPS_A * GROUP_WIDTH_A
RADIUS_A = 64
HEADS_B = 8
Q_LORA = 256
KV_LORA = 128
QK_NOPE = 64
QK_ROPE = 32
V_DIM_B = 64
ROPE_THETA = 10000.0
HEAD_PAD_B = 128
HEADS_C = 4
HEAD_DIM_C = 128
OUT_C = HEADS_C * HEAD_DIM_C
N_BRANCH = 3
BRANCH_WIDTH = 512
N_EXPERTS = 64
TOP_K = 8
N_EXPERT_GROUPS = 8
EXPERTS_PER_GROUP = N_EXPERTS // N_EXPERT_GROUPS
TOPK_GROUPS = 4
D_EXPERT = 256
ROUTED_SCALE = 2.5
DEEPNORM_ALPHA = (2 * DEPTH) ** 0.25
LN_EPS = 1e-5
RMS_EPS = 1e-6

SEG_QKV = 3 * WIDTH_A
SEG_GROUP = 3 * GROUP_WIDTH_A
SEG_MLA = Q_LORA + KV_LORA + 2 * HEAD_PAD_B
SEG_QC = OUT_C
SEG_GL = N_BRANCH * D_MODEL
N_PROJ = SEG_QKV + SEG_MLA + SEG_QC + SEG_GL

NEG_BIG = -1e30
VMEM_LIMIT = 56 * 2 ** 20

NT_DIMS = (((1,), (1,)), ((), ()))


def _params(*sem):
    return pltpu.CompilerParams(dimension_semantics=sem, vmem_limit_bytes=VMEM_LIMIT)


def _resident(block_shape, index_map):
    return pl.BlockSpec(block_shape, index_map, pipeline_mode=pl.Buffered(1))


def _layer_norm_rows(h, g, b):
    mu = jnp.mean(h, axis=-1, keepdims=True)
    c = h - mu
    var = jnp.mean(c * c, axis=-1, keepdims=True)
    return c * lax.rsqrt(var + LN_EPS) * g + b


def _rms_norm_rows(h, g):
    return h * lax.rsqrt(jnp.mean(h * h, axis=-1, keepdims=True) + RMS_EPS) * g


def _embed_ln_kernel(x_ref, g_ref, b_ref, xf_ref):
    xf_ref[...] = _layer_norm_rows(x_ref[...], g_ref[...], b_ref[...])


def _embed_ln(x, g, b):
    t, d = x.shape
    tm = 512
    row = lambda i: (i, 0)
    fixed = lambda i: (0, 0)
    return pl.pallas_call(
        _embed_ln_kernel,
        grid=(t // tm,),
        in_specs=[pl.BlockSpec((tm, d), row), pl.BlockSpec((1, d), fixed), pl.BlockSpec((1, d), fixed)],
        out_specs=pl.BlockSpec((tm, d), row),
        out_shape=jax.ShapeDtypeStruct((t, d), F32),
        compiler_params=_params("parallel"),
        name="embed_ln",
    )(x, g, b)


PROJ_CHUNK = 512
PROJ_TM = 512
LANE_SLABS = GROUP_WIDTH_A // 128


def _proj_kernel(x_ref, w_ref, qkv0_ref, cm1_ref, cm2_ref, mla_ref, qc_ref, gl_ref, slab_ref):
    xb = x_ref[...].astype(BF16)

    def chunk(col, width):
        return jnp.dot(xb, w_ref[:, col:col + width], preferred_element_type=F32)

    col = 0
    for c in range(0, SEG_GROUP, PROJ_CHUNK):
        qkv0_ref[:, c:c + PROJ_CHUNK] = chunk(col + c, PROJ_CHUNK).astype(BF16)
    col += SEG_GROUP
    for ref, (_, d) in ((cm1_ref, DIL_GROUPS[1]), (cm2_ref, DIL_GROUPS[2])):
        n = PROJ_TM // d
        for c in range(0, SEG_GROUP, GROUP_WIDTH_A):
            res = chunk(col + c, GROUP_WIDTH_A)
            for s in range(LANE_SLABS):
                slab_ref[s] = res[:, s * 128:(s + 1) * 128]
            for r in range(d):
                piece = jnp.concatenate([slab_ref[s, pl.ds(r, n, stride=d), :] for s in range(LANE_SLABS)], axis=1)
                ref[0, r, :, c:c + GROUP_WIDTH_A] = piece.astype(BF16)
        col += SEG_GROUP
    for ref, width in ((mla_ref, SEG_MLA), (qc_ref, SEG_QC), (gl_ref, SEG_GL)):
        for c in range(0, width, PROJ_CHUNK):
            w = min(PROJ_CHUNK, width - c)
            ref[:, c:c + w] = chunk(col + c, w).astype(BF16)
        col += width


def _project(x, row_offset, w, batch, seq):
    d = x.shape[1]
    t = batch * seq
    tm = PROJ_TM
    tiles_per_seq = seq // tm
    assert row_offset % tm == 0
    row = lambda i: (i, 0)
    x_row = lambda i: (row_offset // tm + i, 0)

    def class_major(dil):
        shape = (batch, dil, seq // dil, SEG_GROUP)
        spec = pl.BlockSpec((1, dil, tm // dil, SEG_GROUP), lambda i: (i // tiles_per_seq, 0, i % tiles_per_seq, 0))
        return jax.ShapeDtypeStruct(shape, BF16), spec

    (cm1_shape, cm1_spec), (cm2_shape, cm2_spec) = class_major(DIL_GROUPS[1][1]), class_major(DIL_GROUPS[2][1])
    flat = (SEG_GROUP, SEG_MLA, SEG_QC, SEG_GL)
    flat_shapes = [jax.ShapeDtypeStruct((t, n), BF16) for n in flat]
    flat_specs = [pl.BlockSpec((tm, n), row) for n in flat]
    return pl.pallas_call(
        _proj_kernel,
        grid=(t // tm,),
        in_specs=[pl.BlockSpec((tm, d), x_row), _resident((d, N_PROJ), lambda i: (0, 0))],
        out_specs=[flat_specs[0], cm1_spec, cm2_spec] + flat_specs[1:],
        out_shape=[flat_shapes[0], cm1_shape, cm2_shape] + flat_shapes[1:],
        scratch_shapes=[pltpu.VMEM((LANE_SLABS, tm, 128), F32)],
        compiler_params=_params("parallel"),
        name="in_proj",
    )(x, w)


BAND_TQ = 512
BAND_QB = 128
BAND_KB = BAND_QB + 2 * RADIUS_A


def _band_kernel(q_ref, kp_ref, km_ref, kn_ref, vp_ref, vm_ref, vn_ref, o_ref, lse_ref, k_scr, v_scr,
                 *, seq_len, slopes):
    i = pl.program_id(1)
    r = RADIUS_A
    k_scr[0:r, :] = kp_ref[...]
    k_scr[r:r + BAND_TQ, :] = km_ref[...]
    k_scr[r + BAND_TQ:, :] = kn_ref[...]
    v_scr[0:r, :] = vp_ref[...]
    v_scr[r:r + BAND_TQ, :] = vm_ref[...]
    v_scr[r + BAND_TQ:, :] = vn_ref[...]

    row = lax.broadcasted_iota(jnp.int32, (BAND_QB, BAND_KB), 0)
    col = lax.broadcasted_iota(jnp.int32, (BAND_QB, BAND_KB), 1)
    rel = col - r - row
    dist = jnp.abs(rel).astype(F32)
    in_band = jnp.abs(rel) <= r
    scale = HEAD_DIM_A ** -0.5

    for qb in range(BAND_TQ // BAND_QB):
        key_pos = i * BAND_TQ + qb * BAND_QB - r + col
        valid = in_band & (key_pos >= 0) & (key_pos < seq_len)
        for h in range(HEADS_A):
            lanes = slice(h * HEAD_DIM_A, (h + 1) * HEAD_DIM_A)
            q = q_ref[qb * BAND_QB:(qb + 1) * BAND_QB, lanes]
            k = k_scr[qb * BAND_QB:qb * BAND_QB + BAND_KB, lanes]
            v = v_scr[qb * BAND_QB:qb * BAND_QB + BAND_KB, lanes]
            s = lax.dot_general(q, k, NT_DIMS, preferred_element_type=F32)
            logits = jnp.where(valid, s * scale - slopes[h] * dist, NEG_BIG)
            m = jnp.max(logits, axis=1, keepdims=True)
            p = jnp.exp(logits - m)
            den = jnp.sum(p, axis=1, keepdims=True)
            o = jnp.dot(p.astype(BF16), v, preferred_element_type=F32) / den
            rows = slice(qb * BAND_QB, (qb + 1) * BAND_QB)
            o_ref[rows, lanes] = o.astype(BF16)
            lse_ref[rows, lanes] = jnp.broadcast_to(m + jnp.log(den), (BAND_QB, HEAD_DIM_A))


def _band_attention(q_src, k_src, v_src, n_seq, seq_len, slopes):
    tq, r = BAND_TQ, RADIUS_A
    assert seq_len % tq == 0 and tq % r == 0
    steps = seq_len // tq
    halo_per_tile = tq // r
    halo_blocks = seq_len // r

    def main_map(cb):
        return lambda n, i: (n * steps + i, cb)

    def prev_map(cb):
        return lambda n, i: (n * halo_blocks + jnp.maximum(i * halo_per_tile - 1, 0), cb)

    def next_map(cb):
        return lambda n, i: (n * halo_blocks + jnp.minimum((i + 1) * halo_per_tile, halo_blocks - 1), cb)

    (qa, qcb), (ka, kcb), (va, vcb) = q_src, k_src, v_src
    w = GROUP_WIDTH_A
    rows = n_seq * seq_len
    return pl.pallas_call(
        functools.partial(_band_kernel, seq_len=seq_len, slopes=slopes),
        grid=(n_seq, steps),
        in_specs=[
            pl.BlockSpec((tq, w), main_map(qcb)),
            pl.BlockSpec((r, w), prev_map(kcb)), pl.BlockSpec((tq, w), main_map(kcb)),
            pl.BlockSpec((r, w), next_map(kcb)),
            pl.BlockSpec((r, w), prev_map(vcb)), pl.BlockSpec((tq, w), main_map(vcb)),
            pl.BlockSpec((r, w), next_map(vcb)),
        ],
        out_specs=[pl.BlockSpec((tq, w), main_map(0)), pl.BlockSpec((tq, w), main_map(0))],
        out_shape=[jax.ShapeDtypeStruct((rows, w), BF16), jax.ShapeDtypeStruct((rows, w), F32)],
        scratch_shapes=[pltpu.VMEM((tq + 2 * r, w), BF16), pltpu.VMEM((tq + 2 * r, w), BF16)],
        compiler_params=_params("parallel", "parallel"),
        name="band_attention",
    )(qa, ka, ka, ka, va, va, va)


def _alibi_slopes():
    n = N_GROUPS_A * HEADS_A
    return [2.0 ** (-8.0 * (i + 1) / n) for i in range(n)]


def _dilated_mixer(group_qkv, batch, seq):
    slopes = _alibi_slopes()
    outs = []
    for g, (_, d) in enumerate(DIL_GROUPS):
        group_slopes = tuple(float(s * d) for s in slopes[g * HEADS_A:(g + 1) * HEADS_A])
        rows = group_qkv[g].reshape(batch * seq, SEG_GROUP)
        o, lse = _band_attention((rows, 0), (rows, 1), (rows, 2), batch * d, seq // d, group_slopes)
        if d > 1:
            o = o.reshape(batch, d, seq // d, GROUP_WIDTH_A)
            lse = lse.reshape(batch, d, seq // d, GROUP_WIDTH_A)
        outs.append((o, lse))
    return outs


V_ROWS_B = 80
MLA_TQ = 1024
MLA_TK = 4096
MLA_VCHUNK = 1024
MLA_KS = 256
MLA_QS = 512
MLA_LOOKAHEAD = 2
MLA_GAP_LIMIT = 64.0
MLA_HEADS_PER_STEP = 2


def _mla_prep_kernel(mla_ref, cos_ref, sin_ref, cos_t_ref, sin_t_ref, gq_ref, wqa_ref, wqb_ref, gkv_ref,
                     wk_ref, wv_ref, qt_ref, k_ref, vt_ref):
    m = mla_ref[...]
    cq = m[:, 0:Q_LORA].astype(F32)
    ckv = m[:, Q_LORA:Q_LORA + KV_LORA].astype(F32)
    kr = m[:, Q_LORA + KV_LORA:Q_LORA + KV_LORA + HEAD_PAD_B].astype(F32)
    kr_rot = m[:, Q_LORA + KV_LORA + HEAD_PAD_B:].astype(F32)
    scale = (QK_NOPE + QK_ROPE) ** -0.5 * math.log2(math.e)

    cqn = _rms_norm_rows(cq, gq_ref[...]).astype(BF16)
    qa_t = lax.dot_general(wqa_ref[...], cqn, NT_DIMS, preferred_element_type=F32)
    qb_t = lax.dot_general(wqb_ref[...], cqn, NT_DIMS, preferred_element_type=F32)
    ckvn = _rms_norm_rows(ckv, gkv_ref[...]).astype(BF16)
    kn = jnp.dot(ckvn, wk_ref[...], preferred_element_type=F32)
    v_t = lax.dot_general(wv_ref[...], ckvn, NT_DIMS, preferred_element_type=F32)
    k_rope = kr * cos_ref[...] + kr_rot * sin_ref[...]
    cos_t, sin_t = cos_t_ref[...], sin_t_ref[...]
    row = lax.broadcasted_iota(jnp.int32, (V_ROWS_B, 1), 0)
    ones_row = (row == V_DIM_B).astype(F32)
    for h in range(HEADS_B):
        slot = slice(h * HEAD_PAD_B, (h + 1) * HEAD_PAD_B)
        qt_ref[slot, :] = ((qa_t[slot, :] * cos_t + qb_t[slot, :] * sin_t) * scale).astype(BF16)
        k_ref[:, slot] = (kn[:, slot] + k_rope).astype(BF16)
        vt_ref[0, h, 0] = (v_t[h * V_ROWS_B:(h + 1) * V_ROWS_B, :] + ones_row).astype(BF16)


def _mla_prep(mla, tables, gq, wqa_t, wqb_t, gkv, wk, wv_t, batch, seq):
    t = mla.shape[0]
    tm = MLA_VCHUNK
    chunks = seq // tm
    cos, sin, cos_t, sin_t = tables
    row = lambda i: (i, 0)
    pos = lambda i: (i % chunks, 0)
    pos_t = lambda i: (0, i % chunks)
    fixed = lambda i: (0, 0)
    wide = HEADS_B * HEAD_PAD_B
    return pl.pallas_call(
        _mla_prep_kernel,
        grid=(t // tm,),
        in_specs=[pl.BlockSpec((tm, SEG_MLA), row),
                  pl.BlockSpec((tm, HEAD_PAD_B), pos), pl.BlockSpec((tm, HEAD_PAD_B), pos),
                  pl.BlockSpec((HEAD_PAD_B, tm), pos_t), pl.BlockSpec((HEAD_PAD_B, tm), pos_t),
                  pl.BlockSpec((1, Q_LORA), fixed), pl.BlockSpec((wide, Q_LORA), fixed),
                  pl.BlockSpec((wide, Q_LORA), fixed),
                  pl.BlockSpec((1, KV_LORA), fixed), pl.BlockSpec((KV_LORA, wide), fixed),
                  pl.BlockSpec((HEADS_B * V_ROWS_B, KV_LORA), fixed)],
        out_specs=[pl.BlockSpec((wide, tm), lambda i: (i // chunks, i % chunks)),
                   pl.BlockSpec((tm, wide), row),
                   pl.BlockSpec((1, HEADS_B, 1, V_ROWS_B, tm), lambda i: (i // chunks, 0, i % chunks, 0, 0))],
        out_shape=[jax.ShapeDtypeStruct((batch * wide, seq), BF16),
                   jax.ShapeDtypeStruct((t, wide), BF16),
                   jax.ShapeDtypeStruct((batch, HEADS_B, chunks, V_ROWS_B, tm), BF16)],
        compiler_params=_params("parallel"),
        name="mla_prep",
    )(mla, cos, sin, cos_t, sin_t, gq, wqa_t, wqb_t, gkv, wk, wv_t)


def _mla_attn_kernel(qt_ref, k_ref, vt_ref, o_ref, *, seq):
    n_sub = MLA_TK // MLA_KS
    nq = MLA_TQ // MLA_QS
    units = [(c, hh, j) for c in range(n_sub) for hh in range(MLA_HEADS_PER_STEP) for j in range(nq)]

    def scores(rows, hh, j):
        slot = slice(hh * HEAD_PAD_B, (hh + 1) * HEAD_PAD_B)
        return jnp.dot(k_ref[rows, slot], qt_ref[slot, j * MLA_QS:(j + 1) * MLA_QS],
                       preferred_element_type=F32)

    def sweep(kc, carry, update):
        def unit_scores(u):
            c, hh, j = units[u]
            return scores(pl.ds(pl.multiple_of(kc * MLA_TK + c * MLA_KS, MLA_KS), MLA_KS), hh, j)

        new = list(carry)
        pending = [unit_scores(u) for u in range(min(MLA_LOOKAHEAD, len(units)))]
        worst = None
        for u, (c, hh, j) in enumerate(units):
            if u + MLA_LOOKAHEAD < len(units):
                pending.append(unit_scores(u + MLA_LOOKAHEAD))
            v_chunk, v_off = divmod(c * MLA_KS, MLA_VCHUNK)
            v_blk = vt_ref[0, hh, kc * (MLA_TK // MLA_VCHUNK) + v_chunk, :, v_off:v_off + MLA_KS]
            new[hh * nq + j], gap = update(new[hh * nq + j], pending.pop(0), v_blk)
            worst = gap if worst is None else jnp.maximum(worst, gap)
        return tuple(new), worst

    def exact_update(state, s, v_blk):
        m, acc = state
        m_new = jnp.maximum(m, jnp.max(s, axis=0, keepdims=True))
        p = jnp.exp2(s - m_new).astype(BF16)
        acc = jnp.exp2(m - m_new) * acc + jnp.dot(v_blk, p, preferred_element_type=F32)
        return (m_new, acc), jnp.zeros_like(m)

    def lagged_update(state, s, v_blk):
        m, acc = state
        p = jnp.exp2(s - m).astype(BF16)
        col_max = jnp.max(s, axis=0, keepdims=True)
        gap = col_max - m
        m_new = jnp.maximum(m, col_max)
        acc = (acc + jnp.dot(v_blk, p, preferred_element_type=F32)) * jnp.exp2(m - m_new)
        return (m_new, acc), gap

    def step(kc, carry):
        fast, worst = sweep(kc, carry, lagged_update)
        overflow_risk = jnp.max(worst) > MLA_GAP_LIMIT
        return lax.cond(overflow_risk, lambda: sweep(kc, carry, exact_update)[0], lambda: fast)

    first = pl.ds(0, MLA_KS)
    init = tuple((jnp.max(scores(first, hh, j), axis=0, keepdims=True), jnp.zeros((V_ROWS_B, MLA_QS), F32))
                 for hh in range(MLA_HEADS_PER_STEP) for j in range(nq))
    final = lax.fori_loop(0, seq // MLA_TK, step, init)
    parts = [acc[0:V_DIM_B, :] / acc[V_DIM_B:V_DIM_B + 1, :] for _, acc in final]
    heads = [jnp.concatenate(parts[hh * nq:(hh + 1) * nq], axis=1) for hh in range(MLA_HEADS_PER_STEP)]
    o_ref[...] = jnp.concatenate(heads, axis=0).T.astype(BF16)


def _mla_attention(q_t, k, v_t, batch, seq):
    t = batch * seq
    tq = MLA_TQ
    steps = seq // tq
    chunks = seq // MLA_VCHUNK
    pairs = HEADS_B // MLA_HEADS_PER_STEP
    pair = MLA_HEADS_PER_STEP * HEAD_PAD_B
    return pl.pallas_call(
        functools.partial(_mla_attn_kernel, seq=seq),
        grid=(batch, pairs, steps),
        in_specs=[pl.BlockSpec((pair, tq), lambda b, hp, i: (b * pairs + hp, i)),
                  _resident((seq, pair), lambda b, hp, i: (b, hp)),
                  _resident((1, MLA_HEADS_PER_STEP, chunks, V_ROWS_B, MLA_VCHUNK), lambda b, hp, i: (b, hp, 0, 0, 0))],
        out_specs=pl.BlockSpec((tq, MLA_HEADS_PER_STEP * V_DIM_B), lambda b, hp, i: (b * steps + i, hp)),
        out_shape=jax.ShapeDtypeStruct((t, HEADS_B * V_DIM_B), BF16),
        compiler_params=_params("parallel", "parallel", "arbitrary"),
        name="mla_attention",
    )(q_t, k, v_t)


def _rope_tables(seq):
    inv_freq = 1.0 / (ROPE_THETA ** (jnp.arange(0, QK_ROPE, 2, dtype=F32) / QK_ROPE))
    ang = jnp.arange(seq, dtype=F32)[:, None] * inv_freq[None, :]
    cos, sin = jnp.cos(ang), jnp.sin(ang)
    pad = HEAD_PAD_B - QK_NOPE - QK_ROPE
    cos_s = jnp.concatenate([jnp.ones((seq, QK_NOPE), F32), cos, cos, jnp.zeros((seq, pad), F32)], axis=1)
    sin_s = jnp.concatenate([jnp.zeros((seq, QK_NOPE), F32), sin, sin, jnp.zeros((seq, pad), F32)], axis=1)
    return cos_s, sin_s, cos_s.T, sin_s.T


def _mem_kv_kernel(mem_ref, w_ref, o_ref):
    o_ref[...] = jnp.dot(mem_ref[...], w_ref[...], preferred_element_type=F32).astype(BF16)


def _mem_kv(mem_b, w):
    rows, d = mem_b.shape
    n = w.shape[1]
    tm = 256
    return pl.pallas_call(
        _mem_kv_kernel,
        grid=(rows // tm,),
        in_specs=[pl.BlockSpec((tm, d), lambda i: (i, 0)), pl.BlockSpec((d, n), lambda i: (0, 0))],
        out_specs=pl.BlockSpec((tm, n), lambda i: (i, 0)),
        out_shape=jax.ShapeDtypeStruct((rows, n), BF16),
        compiler_params=_params("parallel"),
        name="mem_kv",
    )(mem_b, w)


def _mem_attn_kernel(q_ref, kv_ref, o_ref):
    scale = HEAD_DIM_C ** -0.5
    for h in range(HEADS_C):
        lanes = slice(h * HEAD_DIM_C, (h + 1) * HEAD_DIM_C)
        k = kv_ref[:, lanes]
        v = kv_ref[:, OUT_C + h * HEAD_DIM_C:OUT_C + (h + 1) * HEAD_DIM_C]
        s = lax.dot_general(q_ref[:, lanes], k, NT_DIMS, preferred_element_type=F32) * scale
        m = jnp.max(s, axis=1, keepdims=True)
        p = jnp.exp(s - m)
        den = jnp.sum(p, axis=1, keepdims=True)
        o_ref[:, lanes] = (jnp.dot(p.astype(BF16), v, preferred_element_type=F32) / den).astype(BF16)


def _mem_attention(qc, kv, batch, seq, n_mem):
    t = batch * seq
    ts = 1024
    steps = seq // ts
    return pl.pallas_call(
        _mem_attn_kernel,
        grid=(batch, steps),
        in_specs=[pl.BlockSpec((ts, OUT_C), lambda b, i: (b * steps + i, 0)),
                  pl.BlockSpec((n_mem, 2 * OUT_C), lambda b, i: (b, 0))],
        out_specs=pl.BlockSpec((ts, OUT_C), lambda b, i: (b * steps + i, 0)),
        out_shape=jax.ShapeDtypeStruct((t, OUT_C), BF16),
        compiler_params=_params("parallel", "parallel"),
        name="mem_attention",
    )(qc, kv)


def _token_order(cm_ref, slab_ref):
    d, n = cm_ref.shape[1], cm_ref.shape[2]
    for r in range(d):
        blk = cm_ref[0, r].astype(F32)
        for s in range(LANE_SLABS):
            slab_ref[s, pl.ds(r, n, stride=d), :] = blk[:, s * 128:(s + 1) * 128]
    return jnp.concatenate([slab_ref[s] for s in range(LANE_SLABS)], axis=1)


def _merge_kernel(oa0_ref, oa1_ref, oa2_ref, l0_ref, l1_ref, l2_ref, ob_ref, oc_ref, gl_ref, x_ref,
                  wb_ref, wo_ref, g_ref, b_ref, xf_ref, slab_ref):
    l0 = l0_ref[...]
    l1, l2 = _token_order(l1_ref, slab_ref), _token_order(l2_ref, slab_ref)
    m = jnp.maximum(jnp.maximum(l0, l1), l2)
    e0, e1, e2 = jnp.exp(l0 - m), jnp.exp(l1 - m), jnp.exp(l2 - m)
    oa = e0 * oa0_ref[...].astype(F32) + e1 * _token_order(oa1_ref, slab_ref) + e2 * _token_order(oa2_ref, slab_ref)
    oa = (oa / (e0 + e1 + e2)).astype(BF16)
    z = None
    for i, o in enumerate((oa, ob_ref[...], oc_ref[...])):
        gate = jax.nn.sigmoid(gl_ref[:, i * D_MODEL:(i + 1) * D_MODEL].astype(F32))
        term = gate * jnp.dot(o, wb_ref[i], preferred_element_type=F32)
        z = term if z is None else z + term
    y = jnp.dot(z.astype(BF16), wo_ref[...], preferred_element_type=F32)
    xf_ref[...] = _layer_norm_rows(DEEPNORM_ALPHA * x_ref[...] + y, g_ref[...], b_ref[...])


def _merge(oa, ob, oc, gl, x, row_offset, wb, wo, g, b, batch, seq):
    d = x.shape[1]
    t = batch * seq
    tm = PROJ_TM
    tiles_per_seq = seq // tm
    assert row_offset % tm == 0
    row = lambda i: (i, 0)
    x_row = lambda i: (row_offset // tm + i, 0)
    fixed = lambda i: (0, 0)
    half = pl.BlockSpec((tm, BRANCH_WIDTH), row)

    def class_major(dil):
        return pl.BlockSpec((1, dil, tm // dil, GROUP_WIDTH_A),
                            lambda i: (i // tiles_per_seq, 0, i % tiles_per_seq, 0))

    cm1, cm2 = class_major(DIL_GROUPS[1][1]), class_major(DIL_GROUPS[2][1])
    (oa0, l0), (oa1, l1), (oa2, l2) = oa
    return pl.pallas_call(
        _merge_kernel,
        grid=(t // tm,),
        in_specs=[half, cm1, cm2, half, cm1, cm2, half, half,
                  pl.BlockSpec((tm, SEG_GL), row), pl.BlockSpec((tm, d), x_row),
                  _resident((N_BRANCH, BRANCH_WIDTH, d), lambda i: (0, 0, 0)),
                  _resident((d, d), fixed),
                  pl.BlockSpec((1, d), fixed), pl.BlockSpec((1, d), fixed)],
        out_specs=pl.BlockSpec((tm, d), x_row),
        out_shape=jax.ShapeDtypeStruct(x.shape, F32),
        input_output_aliases={9: 0},
        scratch_shapes=[pltpu.VMEM((LANE_SLABS, tm, 128), F32)],
        compiler_params=_params("arbitrary"),
        name="merge_ln1",
    )(oa0, oa1, oa2, l0, l1, l2, ob, oc, gl, x, wb, wo, g, b)


GATE_LANES = 128


def _first_index_of_max(vals, idx, axis, sentinel):
    mx = jnp.max(vals, axis=axis, keepdims=True)
    return jnp.min(jnp.where(vals == mx, idx, sentinel), axis=axis, keepdims=True)


def _router_kernel(x_ref, w_ref, bias_ref, xg_ref, code_ref):
    tm = x_ref.shape[0]
    x = x_ref[...]
    logits = lax.dot_general(w_ref[...], x.astype(BF16), NT_DIMS, preferred_element_type=F32)
    scores = jax.nn.sigmoid(logits)
    choice = scores + bias_ref[...]
    neg = -jnp.inf

    c3 = choice.reshape(N_EXPERT_GROUPS, EXPERTS_PER_GROUP, tm)
    e_idx = lax.broadcasted_iota(jnp.int32, c3.shape, 1)
    first = jnp.max(c3, axis=1, keepdims=True)
    first_at = jnp.min(jnp.where(c3 == first, e_idx, EXPERTS_PER_GROUP), axis=1, keepdims=True)
    second = jnp.max(jnp.where(e_idx == first_at, neg, c3), axis=1, keepdims=True)
    group_score = (first + second).reshape(N_EXPERT_GROUPS, tm)

    g_idx = lax.broadcasted_iota(jnp.int32, group_score.shape, 0)
    group_sel = jnp.zeros(group_score.shape, jnp.bool_)
    for _ in range(TOPK_GROUPS):
        at = _first_index_of_max(group_score, g_idx, 0, N_EXPERT_GROUPS)
        hit = g_idx == at
        group_sel = group_sel | hit
        group_score = jnp.where(hit, neg, group_score)

    allowed = jnp.broadcast_to(group_sel.reshape(N_EXPERT_GROUPS, 1, tm), c3.shape).reshape(N_EXPERTS, tm)
    cand = jnp.where(allowed, choice, neg)
    x_idx = lax.broadcasted_iota(jnp.int32, cand.shape, 0)
    chosen = jnp.zeros(cand.shape, jnp.bool_)
    for _ in range(TOP_K):
        at = _first_index_of_max(cand, x_idx, 0, N_EXPERTS)
        hit = x_idx == at
        chosen = chosen | hit
        cand = jnp.where(hit, neg, cand)

    w_sel = jnp.where(chosen, scores, 0.0)
    gates = w_sel / jnp.sum(w_sel, axis=0, keepdims=True) * ROUTED_SCALE
    padded = jnp.concatenate([gates, jnp.zeros((GATE_LANES - N_EXPERTS, tm), F32)], axis=0)
    xg_ref[:, 0:D_MODEL] = x
    xg_ref[:, D_MODEL:] = padded.T
    bit = jnp.left_shift(1, g_idx)
    code_ref[...] = jnp.sum(jnp.where(group_sel, bit, 0), axis=0, keepdims=True)


def _router(x, w_t, bias):
    t, d = x.shape
    tm = 1024
    return pl.pallas_call(
        _router_kernel,
        grid=(t // tm,),
        in_specs=[pl.BlockSpec((tm, d), lambda i: (i, 0)), pl.BlockSpec((N_EXPERTS, d), lambda i: (0, 0)),
                  pl.BlockSpec((N_EXPERTS, 1), lambda i: (0, 0))],
        out_specs=[pl.BlockSpec((tm, XG_WIDTH), lambda i: (i, 0)), pl.BlockSpec((1, tm), lambda i: (0, i))],
        out_shape=[jax.ShapeDtypeStruct((t, XG_WIDTH), F32), jax.ShapeDtypeStruct((1, t), jnp.int32)],
        compiler_params=_params("parallel"),
        name="router",
    )(x, w_t, bias)


XG_WIDTH = D_MODEL + GATE_LANES
MOE_TM = 1024
MOE_SUB = 128
MOE_NSUB = MOE_TM // MOE_SUB
TOKEN_BITS = 16


def _dispatch_plan(code):
    t = code.shape[1]
    tiles = t // MOE_TM
    assert t <= 1 << TOKEN_BITS
    key = jnp.sort(code[0] * (1 << TOKEN_BITS) + jnp.arange(t, dtype=jnp.int32))
    perm = key & ((1 << TOKEN_BITS) - 1)
    bits = ((key >> TOKEN_BITS)[:, None] >> jnp.arange(N_EXPERT_GROUPS, dtype=jnp.int32)[None, :]) & 1
    sub = jnp.max(bits.reshape(tiles, MOE_NSUB, MOE_SUB, N_EXPERT_GROUPS), axis=2)
    tile_active = jnp.max(sub, axis=1)
    n_active = jnp.sum(tile_active, axis=1).astype(jnp.int32)
    order = jnp.argsort(1 - tile_active, axis=1, stable=True).astype(jnp.int32)
    step = jnp.arange(N_EXPERT_GROUPS, dtype=jnp.int32)[None, :]
    last = jnp.take_along_axis(order, jnp.maximum(n_active - 1, 0)[:, None], axis=1)
    groups = jnp.where(step < n_active[:, None], order, last)
    flags = jnp.transpose(sub, (0, 2, 1)).astype(jnp.int32)
    return perm.reshape(tiles, 1, MOE_TM), groups.reshape(-1), n_active, flags.reshape(-1)


def _row_dma(src_ref, dst_ref, src_row, dst_row, sem):
    return pltpu.make_async_copy(src_ref.at[pl.ds(src_row, 1)], dst_ref.at[pl.ds(dst_row, 1)], sem)


DMA_THREADS = 2


def _gather_rows_kernel(perm_ref, src_ref, out_ref, sem):
    def start(q, carry):
        for u in range(DMA_THREADS):
            r = q * DMA_THREADS + u
            _row_dma(src_ref, out_ref, perm_ref[0, 0, r], r, sem).start(priority=u)
        return carry

    lax.fori_loop(0, MOE_TM // DMA_THREADS, start, 0)
    pltpu.make_async_copy(src_ref.at[pl.ds(0, MOE_TM)], out_ref, sem).wait()


def _gather_rows(src, perm):
    t, width = src.shape
    return pl.pallas_call(
        _gather_rows_kernel,
        grid=(t // MOE_TM,),
        in_specs=[pl.BlockSpec((1, 1, MOE_TM), lambda i: (i, 0, 0), memory_space=pltpu.SMEM),
                  pl.BlockSpec(memory_space=pl.ANY)],
        out_specs=pl.BlockSpec((MOE_TM, width), lambda i: (i, 0)),
        out_shape=jax.ShapeDtypeStruct((t, width), src.dtype),
        scratch_shapes=[pltpu.SemaphoreType.DMA(())],
        compiler_params=_params("arbitrary"),
        name="moe_gather",
    )(perm, src)


def _scatter_rows_kernel(perm_ref, src_ref, out_ref, sem):
    def start(q, carry):
        for u in range(DMA_THREADS):
            r = q * DMA_THREADS + u
            _row_dma(src_ref, out_ref, r, perm_ref[0, 0, r], sem).start(priority=u)
        return carry

    lax.fori_loop(0, MOE_TM // DMA_THREADS, start, 0)
    pltpu.make_async_copy(src_ref, out_ref.at[pl.ds(0, MOE_TM)], sem).wait()


def _scatter_rows(src, perm):
    t, width = src.shape
    return pl.pallas_call(
        _scatter_rows_kernel,
        grid=(t // MOE_TM,),
        in_specs=[pl.BlockSpec((1, 1, MOE_TM), lambda i: (i, 0, 0), memory_space=pltpu.SMEM),
                  pl.BlockSpec((MOE_TM, width), lambda i: (i, 0))],
        out_specs=pl.BlockSpec(memory_space=pl.ANY),
        out_shape=jax.ShapeDtypeStruct((t, width), src.dtype),
        scratch_shapes=[pltpu.SemaphoreType.DMA(())],
        compiler_params=_params("arbitrary"),
        name="moe_scatter",
    )(perm, src)


def _swiglu(xb, wg, wu):
    return jax.nn.silu(jnp.dot(xb, wg, preferred_element_type=F32)) * jnp.dot(xb, wu, preferred_element_type=F32)


def _moe_kernel(groups_ref, nact_ref, flags_ref, xg_ref, wg_ref, wu_ref, wd_ref, sg_ref, su_ref, sd_ref,
                g_ref, b_ref, o_ref, acc_ref, xb_ref):
    i = pl.program_id(0)
    j = pl.program_id(1)

    @pl.when(j == 0)
    def _():
        xb = xg_ref[:, 0:D_MODEL].astype(BF16)
        xb_ref[...] = xb
        h = _swiglu(xb, sg_ref[...], su_ref[...])
        acc_ref[...] = jnp.dot(h.astype(BF16), sd_ref[...], preferred_element_type=F32)

    @pl.when(j < nact_ref[i])
    def _():
        group = groups_ref[i * N_EXPERT_GROUPS + j]
        wd_all = wd_ref[...].reshape(EXPERTS_PER_GROUP * D_EXPERT, D_MODEL)
        for s in range(MOE_NSUB):
            @pl.when(flags_ref[(i * N_EXPERT_GROUPS + group) * MOE_NSUB + s] != 0)
            def _():
                rows = slice(s * MOE_SUB, (s + 1) * MOE_SUB)
                xb = xb_ref[rows, :]
                gates = xg_ref[rows, D_MODEL:]
                lane = lax.broadcasted_iota(jnp.int32, gates.shape, 1)
                hs = []
                for e in range(EXPERTS_PER_GROUP):
                    gate = jnp.sum(jnp.where(lane == group * EXPERTS_PER_GROUP + e, gates, 0.0),
                                   axis=1, keepdims=True)
                    hs.append((_swiglu(xb, wg_ref[e], wu_ref[e]) * gate).astype(BF16))
                acc_ref[rows, :] += jnp.dot(jnp.concatenate(hs, axis=1), wd_all, preferred_element_type=F32)

    @pl.when(j == pl.num_programs(1) - 1)
    def _():
        o_ref[...] = _layer_norm_rows(DEEPNORM_ALPHA * xg_ref[:, 0:D_MODEL] + acc_ref[...], g_ref[...], b_ref[...])


def _moe(xg_sorted, plan, wg, wu, wd, sg, su, sd, g, b):
    t = xg_sorted.shape[0]
    d, f, tm, ng = D_MODEL, D_EXPERT, MOE_TM, N_EXPERT_GROUPS
    _, groups, n_active, flags = plan
    row = lambda i, j, *_: (i, 0)
    fixed = lambda i, j, *_: (0, 0)
    expert_block = lambda i, j, groups_ref, *_: (groups_ref[i * ng + j], 0, 0)
    grid_spec = pltpu.PrefetchScalarGridSpec(
        num_scalar_prefetch=3,
        grid=(t // tm, ng),
        in_specs=[pl.BlockSpec((tm, XG_WIDTH), row),
                  pl.BlockSpec((EXPERTS_PER_GROUP, d, f), expert_block),
                  pl.BlockSpec((EXPERTS_PER_GROUP, d, f), expert_block),
                  pl.BlockSpec((EXPERTS_PER_GROUP, f, d), expert_block),
                  pl.BlockSpec((d, f), fixed), pl.BlockSpec((d, f), fixed), pl.BlockSpec((f, d), fixed),
                  pl.BlockSpec((1, d), fixed), pl.BlockSpec((1, d), fixed)],
        out_specs=pl.BlockSpec((tm, d), row),
        scratch_shapes=[pltpu.VMEM((tm, d), F32), pltpu.VMEM((tm, d), BF16)],
    )
    return pl.pallas_call(
        _moe_kernel,
        grid_spec=grid_spec,
        out_shape=jax.ShapeDtypeStruct((t, d), F32),
        compiler_params=_params("arbitrary", "arbitrary"),
        name="moe_ln2",
    )(groups, n_active, flags, xg_sorted, wg, wu, wd, sg, su, sd, g, b)


def _moe_layer(x, w_router_t, router_bias, wg, wu, wd, sg, su, sd, g, b):
    xg, code = _router(x, w_router_t, router_bias.reshape(N_EXPERTS, 1))
    plan = _dispatch_plan(code)
    perm = plan[0]
    y_sorted = _moe(_gather_rows(xg, perm), plan, wg, wu, wd, sg, su, sd, g, b)
    return _scatter_rows(y_sorted, perm)


def _rotate_half_columns(w):
    half = QK_ROPE // 2
    return jnp.concatenate([-w[..., half:], w[..., :half]], axis=-1)


def _prep_in_proj(w_in):
    layers, d, _ = w_in.shape
    cuts = np.cumsum((WIDTH_A, WIDTH_A, WIDTH_A, Q_LORA, KV_LORA, QK_ROPE, OUT_C))
    qa, ka, va, cq, ckv, kr, qc, gl = jnp.split(w_in, [int(c) for c in cuts], axis=-1)
    lead = jnp.zeros((layers, d, QK_NOPE), w_in.dtype)
    tail = jnp.zeros((layers, d, HEAD_PAD_B - QK_NOPE - QK_ROPE), w_in.dtype)
    kr_slot = jnp.concatenate([lead, kr, tail], axis=-1)
    kr_rot_slot = jnp.concatenate([lead, _rotate_half_columns(kr), tail], axis=-1)
    per_group = [m[..., g * GROUP_WIDTH_A:(g + 1) * GROUP_WIDTH_A] for g in range(N_GROUPS_A) for m in (qa, ka, va)]
    return jnp.concatenate(per_group + [cq, ckv, kr_slot, kr_rot_slot, qc, gl], axis=-1).astype(BF16)


def _prep_mla_weights(w_q_up, w_kv_up):
    layers = w_q_up.shape[0]
    wq = w_q_up.reshape(layers, Q_LORA, HEADS_B, QK_NOPE + QK_ROPE)
    nope, rope = wq[..., :QK_NOPE], wq[..., QK_NOPE:]
    pad = HEAD_PAD_B - QK_NOPE - QK_ROPE
    zq = lambda n: jnp.zeros((layers, Q_LORA, HEADS_B, n), w_q_up.dtype)
    wqa = jnp.concatenate([nope, rope, zq(pad)], axis=-1)
    wqb = jnp.concatenate([zq(QK_NOPE), _rotate_half_columns(rope), zq(pad)], axis=-1)
    wkv = w_kv_up.reshape(layers, KV_LORA, HEADS_B, QK_NOPE + V_DIM_B)
    zk = lambda n: jnp.zeros((layers, KV_LORA, HEADS_B, n), w_kv_up.dtype)
    wk = jnp.concatenate([wkv[..., :QK_NOPE], zk(HEAD_PAD_B - QK_NOPE)], axis=-1)
    wv = jnp.concatenate([wkv[..., QK_NOPE:], zk(V_ROWS_B - V_DIM_B)], axis=-1)
    flat = lambda w: w.reshape(layers, w.shape[1], -1).astype(BF16)
    flat_t = lambda w: jnp.swapaxes(flat(w), 1, 2)
    return flat_t(wqa), flat_t(wqb), flat(wk), flat_t(wv)


def _encoder(groups, emb_g, emb_b, w, depth):
    d = groups[0][0].shape[-1]
    row2 = lambda v: v.reshape(1, -1)
    sizes = [x.shape[0] * x.shape[1] for x, _ in groups]
    offsets = [sum(sizes[:i]) for i in range(len(sizes))]
    xf = _embed_ln(jnp.concatenate([x.reshape(-1, d) for x, _ in groups], axis=0), row2(emb_g), row2(emb_b))
    mems = [mem.reshape(-1, d).astype(BF16) for _, mem in groups]
    tables = [_rope_tables(x.shape[1]) for x, _ in groups]
    for l in range(depth):
        for (x, mem), off, mem_b, table in zip(groups, offsets, mems, tables):
            batch, seq, _ = x.shape
            qkv0, cm1, cm2, mla, qc, gl = _project(xf, off, w["in_proj"][l], batch, seq)
            oa = _dilated_mixer((qkv0, cm1, cm2), batch, seq)
            q_t, k, v_t = _mla_prep(mla, table, row2(w["q_norm_g"][l]), w["wqa"][l], w["wqb"][l],
                                    row2(w["kv_norm_g"][l]), w["wk"][l], w["wv"][l], batch, seq)
            ob = _mla_attention(q_t, k, v_t, batch, seq)
            oc = _mem_attention(qc, _mem_kv(mem_b, w["mem_kv"][l]), batch, seq, mem.shape[1])
            xf = _merge(oa, ob, oc, gl, xf, off, w["branch"][l], w["out"][l],
                        row2(w["ln1_g"][l]), row2(w["ln1_b"][l]), batch, seq)
        xf = _moe_layer(xf, w["router_t"][l], w["router_bias"][l], w["exp_gate"][l],
                        w["exp_up"][l], w["exp_down"][l], w["sh_gate"][l], w["sh_up"][l], w["sh_down"][l],
                        row2(w["ln2_g"][l]), row2(w["ln2_b"][l]))
    return [xf[off:off + n].reshape(x.shape) for (x, _), off, n in zip(groups, offsets, sizes)]


def kernel(x_prompt, x_sample, mem_prompt, mem_sample, emb_ln_g, emb_ln_b, w_in, q_norm_g, w_q_up, kv_norm_g,
           w_kv_up, w_mem_kv, w_branch, w_out, ln1_g, ln1_b, w_router, router_bias, w_exp_gate, w_exp_up,
           w_exp_down, w_sh_gate, w_sh_up, w_sh_down, ln2_g, ln2_b):
    wqa, wqb, wk, wv = _prep_mla_weights(w_q_up, w_kv_up)
    w = {
        "in_proj": _prep_in_proj(w_in),
        "q_norm_g": q_norm_g, "kv_norm_g": kv_norm_g, "wqa": wqa, "wqb": wqb, "wk": wk, "wv": wv,
        "mem_kv": w_mem_kv.astype(BF16), "branch": w_branch.astype(BF16), "out": w_out.astype(BF16),
        "ln1_g": ln1_g, "ln1_b": ln1_b,
        "router_t": jnp.swapaxes(w_router, 1, 2).astype(BF16), "router_bias": router_bias,
        "exp_gate": w_exp_gate.astype(BF16), "exp_up": w_exp_up.astype(BF16), "exp_down": w_exp_down.astype(BF16),
        "sh_gate": w_sh_gate.astype(BF16), "sh_up": w_sh_up.astype(BF16), "sh_down": w_sh_down.astype(BF16),
        "ln2_g": ln2_g, "ln2_b": ln2_b,
    }
    depth = w_in.shape[0]
    y_prompt, y_sample = _encoder([(x_prompt, mem_prompt), (x_sample, mem_sample)], emb_ln_g, emb_ln_b, w, depth)
    return (y_prompt, y_sample)
```

```python
import functools
import math

import numpy as np
import jax
import jax.numpy as jnp
from jax import lax
from jax.experimental import pallas as pl
from jax.experimental.pallas import tpu as pltpu

F32 = jnp.float32
BF16 = jnp.bfloat16

D_MODEL = 1024
DEPTH = 4
DIL_GROUPS = ((128, 1), (512, 4), (2048, 16))
N_GROUPS_A = 3
HEADS_A = 4
HEAD_DIM_A = 128
GROUP_WIDTH_A = HEADS_A * HEAD_DIM_A
WIDTH_A = N_GROUPS_A * GROUP_WIDTH_A
RADIUS_A = 64
HEADS_B = 8
Q_LORA = 256
KV_LORA = 128
QK_NOPE = 64
QK_ROPE = 32
V_DIM_B = 64
ROPE_THETA = 10000.0
HEAD_PAD_B = 128
HEADS_C = 4
HEAD_DIM_C = 128
OUT_C = HEADS_C * HEAD_DIM_C
N_BRANCH = 3
BRANCH_WIDTH = 512
N_EXPERTS = 64
TOP_K = 8
N_EXPERT_GROUPS = 8
EXPERTS_PER_GROUP = N_EXPERTS // N_EXPERT_GROUPS
TOPK_GROUPS = 4
D_EXPERT = 256
ROUTED_SCALE = 2.5
DEEPNORM_ALPHA = (2 * DEPTH) ** 0.25
LN_EPS = 1e-5
RMS_EPS = 1e-6

SEG_QKV = 3 * WIDTH_A
SEG_GROUP = 3 * GROUP_WIDTH_A
SEG_MLA = Q_LORA + KV_LORA + 2 * HEAD_PAD_B
SEG_QC = OUT_C
SEG_GL = N_BRANCH * D_MODEL
N_PROJ = SEG_QKV + SEG_MLA + SEG_QC + SEG_GL

NEG_BIG = -1e30
VMEM_LIMIT = 56 * 2 ** 20

NT_DIMS = (((1,), (1,)), ((), ()))


def _params(*sem):
    return pltpu.CompilerParams(dimension_semantics=sem, vmem_limit_bytes=VMEM_LIMIT)


def _resident(block_shape, index_map):
    return pl.BlockSpec(block_shape, index_map, pipeline_mode=pl.Buffered(1))


def _layer_norm_rows(h, g, b):
    mu = jnp.mean(h, axis=-1, keepdims=True)
    c = h - mu
    var = jnp.mean(c * c, axis=-1, keepdims=True)
    return c * lax.rsqrt(var + LN_EPS) * g + b


def _rms_norm_rows(h, g):
    return h * lax.rsqrt(jnp.mean(h * h, axis=-1, keepdims=True) + RMS_EPS) * g


X_SLABS = D_MODEL // 128
XG_SLABS = X_SLABS + 1


def _load_token_tiles(ref, n_tokens, rows_per_token):
    return jnp.concatenate([ref[pl.ds(c, n_tokens, stride=rows_per_token), :] for c in range(X_SLABS)], axis=1)


def _store_token_tiles(ref, value, rows_per_token):
    for c in range(X_SLABS):
        ref[pl.ds(c, value.shape[0], stride=rows_per_token), :] = value[:, c * 128:(c + 1) * 128]


def _embed_ln_kernel(x_ref, g_ref, b_ref, xf_ref):
    _store_token_tiles(xf_ref, _layer_norm_rows(x_ref[...], g_ref[...], b_ref[...]), X_SLABS)


def _embed_ln(x, g, b):
    t, d = x.shape
    tm = 512
    row = lambda i: (i, 0)
    fixed = lambda i: (0, 0)
    return pl.pallas_call(
        _embed_ln_kernel,
        grid=(t // tm,),
        in_specs=[pl.BlockSpec((tm, d), row), pl.BlockSpec((1, d), fixed), pl.BlockSpec((1, d), fixed)],
        out_specs=pl.BlockSpec((tm * X_SLABS, 128), row),
        out_shape=jax.ShapeDtypeStruct((t * X_SLABS, 128), F32),
        compiler_params=_params("parallel"),
        name="embed_ln",
    )(x, g, b)


PROJ_CHUNK = 512
PROJ_TM = 512
LANE_SLABS = GROUP_WIDTH_A // 128


def _proj_kernel(x_ref, w_ref, qkv0_ref, cm1_ref, cm2_ref, mla_ref, qc_ref, gl_ref, slab_ref):
    xb = _load_token_tiles(x_ref, PROJ_TM, X_SLABS).astype(BF16)

    def chunk(col, width):
        return jnp.dot(xb, w_ref[:, col:col + width], preferred_element_type=F32)

    col = 0
    for c in range(0, SEG_GROUP, PROJ_CHUNK):
        qkv0_ref[:, c:c + PROJ_CHUNK] = chunk(col + c, PROJ_CHUNK).astype(BF16)
    col += SEG_GROUP
    for ref, (_, d) in ((cm1_ref, DIL_GROUPS[1]), (cm2_ref, DIL_GROUPS[2])):
        n = PROJ_TM // d
        for c in range(0, SEG_GROUP, GROUP_WIDTH_A):
            res = chunk(col + c, GROUP_WIDTH_A)
            for s in range(LANE_SLABS):
                slab_ref[s] = res[:, s * 128:(s + 1) * 128]
            for r in range(d):
                piece = jnp.concatenate([slab_ref[s, pl.ds(r, n, stride=d), :] for s in range(LANE_SLABS)], axis=1)
                ref[0, r, :, c:c + GROUP_WIDTH_A] = piece.astype(BF16)
        col += SEG_GROUP
    for ref, width in ((mla_ref, SEG_MLA), (qc_ref, SEG_QC), (gl_ref, SEG_GL)):
        for c in range(0, width, PROJ_CHUNK):
            w = min(PROJ_CHUNK, width - c)
            ref[:, c:c + w] = chunk(col + c, w).astype(BF16)
        col += width


def _project(x, row_offset, w, batch, seq):
    d = D_MODEL
    t = batch * seq
    tm = PROJ_TM
    tiles_per_seq = seq // tm
    assert row_offset % tm == 0
    row = lambda i: (i, 0)
    x_row = lambda i: (row_offset // tm + i, 0)

    def class_major(dil):
        shape = (batch, dil, seq // dil, SEG_GROUP)
        spec = pl.BlockSpec((1, dil, tm // dil, SEG_GROUP), lambda i: (i // tiles_per_seq, 0, i % tiles_per_seq, 0))
        return jax.ShapeDtypeStruct(shape, BF16), spec

    (cm1_shape, cm1_spec), (cm2_shape, cm2_spec) = class_major(DIL_GROUPS[1][1]), class_major(DIL_GROUPS[2][1])
    flat = (SEG_GROUP, SEG_MLA, SEG_QC, SEG_GL)
    flat_shapes = [jax.ShapeDtypeStruct((t, n), BF16) for n in flat]
    flat_specs = [pl.BlockSpec((tm, n), row) for n in flat]
    return pl.pallas_call(
        _proj_kernel,
        grid=(t // tm,),
        in_specs=[pl.BlockSpec((tm * X_SLABS, 128), x_row), _resident((d, N_PROJ), lambda i: (0, 0))],
        out_specs=[flat_specs[0], cm1_spec, cm2_spec] + flat_specs[1:],
        out_shape=[flat_shapes[0], cm1_shape, cm2_shape] + flat_shapes[1:],
        scratch_shapes=[pltpu.VMEM((LANE_SLABS, tm, 128), F32)],
        compiler_params=_params("parallel"),
        name="in_proj",
    )(x, w)


BAND_TQ = 512
BAND_QB = 512
BAND_KB = BAND_QB + 2 * RADIUS_A


def _band_kernel(q_ref, kp_ref, km_ref, kn_ref, vp_ref, vm_ref, vn_ref, o_ref, lse_ref, k_scr, v_scr,
                 *, seq_len, slopes):
    i = pl.program_id(1)
    r = RADIUS_A
    k_scr[0:r, :] = kp_ref[...]
    k_scr[r:r + BAND_TQ, :] = km_ref[...]
    k_scr[r + BAND_TQ:, :] = kn_ref[...]
    v_scr[0:r, :] = vp_ref[...]
    v_scr[r:r + BAND_TQ, :] = vm_ref[...]
    v_scr[r + BAND_TQ:, :] = vn_ref[...]

    row = lax.broadcasted_iota(jnp.int32, (BAND_QB, BAND_KB), 0)
    col = lax.broadcasted_iota(jnp.int32, (BAND_QB, BAND_KB), 1)
    rel = col - r - row
    dist = jnp.abs(rel).astype(F32)
    in_band = jnp.abs(rel) <= r
    scale = HEAD_DIM_A ** -0.5

    for qb in range(BAND_TQ // BAND_QB):
        key_pos = i * BAND_TQ + qb * BAND_QB - r + col
        valid = in_band & (key_pos >= 0) & (key_pos < seq_len)
        for h in range(HEADS_A):
            lanes = slice(h * HEAD_DIM_A, (h + 1) * HEAD_DIM_A)
            q = q_ref[qb * BAND_QB:(qb + 1) * BAND_QB, lanes]
            k = k_scr[qb * BAND_QB:qb * BAND_QB + BAND_KB, lanes]
            v = v_scr[qb * BAND_QB:qb * BAND_QB + BAND_KB, lanes]
            s = lax.dot_general(q, k, NT_DIMS, preferred_element_type=F32)
            logits = jnp.where(valid, s * scale - slopes[h] * dist, NEG_BIG)
            m = jnp.max(logits, axis=1, keepdims=True)
            p = jnp.exp(logits - m)
            den = jnp.sum(p, axis=1, keepdims=True)
            o = jnp.dot(p.astype(BF16), v, preferred_element_type=F32) / den
            rows = slice(qb * BAND_QB, (qb + 1) * BAND_QB)
            o_ref[rows, lanes] = o.astype(BF16)
            lse_ref[rows, lanes] = jnp.broadcast_to(m + jnp.log(den), (BAND_QB, HEAD_DIM_A))


def _band_attention(q_src, k_src, v_src, n_seq, seq_len, slopes):
    tq, r = BAND_TQ, RADIUS_A
    assert seq_len % tq == 0 and tq % r == 0
    steps = seq_len // tq
    halo_per_tile = tq // r
    halo_blocks = seq_len // r

    def main_map(cb):
        return lambda n, i: (n * steps + i, cb)

    def prev_map(cb):
        return lambda n, i: (n * halo_blocks + jnp.maximum(i * halo_per_tile - 1, 0), cb)

    def next_map(cb):
        return lambda n, i: (n * halo_blocks + jnp.minimum((i + 1) * halo_per_tile, halo_blocks - 1), cb)

    (qa, qcb), (ka, kcb), (va, vcb) = q_src, k_src, v_src
    w = GROUP_WIDTH_A
    rows = n_seq * seq_len
    return pl.pallas_call(
        functools.partial(_band_kernel, seq_len=seq_len, slopes=slopes),
        grid=(n_seq, steps),
        in_specs=[
            pl.BlockSpec((tq, w), main_map(qcb)),
            pl.BlockSpec((r, w), prev_map(kcb)), pl.BlockSpec((tq, w), main_map(kcb)),
            pl.BlockSpec((r, w), next_map(kcb)),
            pl.BlockSpec((r, w), prev_map(vcb)), pl.BlockSpec((tq, w), main_map(vcb)),
            pl.BlockSpec((r, w), next_map(vcb)),
        ],
        out_specs=[pl.BlockSpec((tq, w), main_map(0)), pl.BlockSpec((tq, w), main_map(0))],
        out_shape=[jax.ShapeDtypeStruct((rows, w), BF16), jax.ShapeDtypeStruct((rows, w), F32)],
        scratch_shapes=[pltpu.VMEM((tq + 2 * r, w), BF16), pltpu.VMEM((tq + 2 * r, w), BF16)],
        compiler_params=_params("parallel", "parallel"),
        name="band_attention",
    )(qa, ka, ka, ka, va, va, va)


def _alibi_slopes():
    n = N_GROUPS_A * HEADS_A
    return [2.0 ** (-8.0 * (i + 1) / n) for i in range(n)]


def _dilated_mixer(group_qkv, batch, seq):
    slopes = _alibi_slopes()
    outs = []
    for g, (_, d) in enumerate(DIL_GROUPS):
        group_slopes = tuple(float(s * d) for s in slopes[g * HEADS_A:(g + 1) * HEADS_A])
        rows = group_qkv[g].reshape(batch * seq, SEG_GROUP)
        o, lse = _band_attention((rows, 0), (rows, 1), (rows, 2), batch * d, seq // d, group_slopes)
        if d > 1:
            o = o.reshape(batch, d, seq // d, GROUP_WIDTH_A)
            lse = lse.reshape(batch, d, seq // d, GROUP_WIDTH_A)
        outs.append((o, lse))
    return outs


V_ROWS_B = 80
MLA_TQ = 1024
MLA_TK = 4096
MLA_VCHUNK = 1024
MLA_KS = 256
MLA_QS = 512
MLA_LOOKAHEAD = 2
MLA_GAP_LIMIT = 64.0
MLA_HEADS_PER_STEP = 2


def _mla_prep_kernel(mla_ref, cos_ref, sin_ref, cos_t_ref, sin_t_ref, gq_ref, wqa_ref, wqb_ref, gkv_ref,
                     wk_ref, wv_ref, qt_ref, k_ref, vt_ref):
    m = mla_ref[...]
    cq = m[:, 0:Q_LORA].astype(F32)
    ckv = m[:, Q_LORA:Q_LORA + KV_LORA].astype(F32)
    kr = m[:, Q_LORA + KV_LORA:Q_LORA + KV_LORA + HEAD_PAD_B].astype(F32)
    kr_rot = m[:, Q_LORA + KV_LORA + HEAD_PAD_B:].astype(F32)
    scale = (QK_NOPE + QK_ROPE) ** -0.5 * math.log2(math.e)

    cqn = _rms_norm_rows(cq, gq_ref[...]).astype(BF16)
    qa_t = lax.dot_general(wqa_ref[...], cqn, NT_DIMS, preferred_element_type=F32)
    qb_t = lax.dot_general(wqb_ref[...], cqn, NT_DIMS, preferred_element_type=F32)
    ckvn = _rms_norm_rows(ckv, gkv_ref[...]).astype(BF16)
    kn = jnp.dot(ckvn, wk_ref[...], preferred_element_type=F32)
    v_t = lax.dot_general(wv_ref[...], ckvn, NT_DIMS, preferred_element_type=F32)
    k_rope = kr * cos_ref[...] + kr_rot * sin_ref[...]
    cos_t, sin_t = cos_t_ref[...], sin_t_ref[...]
    row = lax.broadcasted_iota(jnp.int32, (V_ROWS_B, 1), 0)
    ones_row = (row == V_DIM_B).astype(F32)
    for h in range(HEADS_B):
        slot = slice(h * HEAD_PAD_B, (h + 1) * HEAD_PAD_B)
        qt_ref[slot, :] = ((qa_t[slot, :] * cos_t + qb_t[slot, :] * sin_t) * scale).astype(BF16)
        k_ref[:, slot] = (kn[:, slot] + k_rope).astype(BF16)
        vt_ref[0, h, 0] = (v_t[h * V_ROWS_B:(h + 1) * V_ROWS_B, :] + ones_row).astype(BF16)


def _mla_prep(mla, tables, gq, wqa_t, wqb_t, gkv, wk, wv_t, batch, seq):
    t = mla.shape[0]
    tm = MLA_VCHUNK
    chunks = seq // tm
    cos, sin, cos_t, sin_t = tables
    row = lambda i: (i, 0)
    pos = lambda i: (i % chunks, 0)
    pos_t = lambda i: (0, i % chunks)
    fixed = lambda i: (0, 0)
    wide = HEADS_B * HEAD_PAD_B
    return pl.pallas_call(
        _mla_prep_kernel,
        grid=(t // tm,),
        in_specs=[pl.BlockSpec((tm, SEG_MLA), row),
                  pl.BlockSpec((tm, HEAD_PAD_B), pos), pl.BlockSpec((tm, HEAD_PAD_B), pos),
                  pl.BlockSpec((HEAD_PAD_B, tm), pos_t), pl.BlockSpec((HEAD_PAD_B, tm), pos_t),
                  pl.BlockSpec((1, Q_LORA), fixed), pl.BlockSpec((wide, Q_LORA), fixed),
                  pl.BlockSpec((wide, Q_LORA), fixed),
                  pl.BlockSpec((1, KV_LORA), fixed), pl.BlockSpec((KV_LORA, wide), fixed),
                  pl.BlockSpec((HEADS_B * V_ROWS_B, KV_LORA), fixed)],
        out_specs=[pl.BlockSpec((wide, tm), lambda i: (i // chunks, i % chunks)),
                   pl.BlockSpec((tm, wide), row),
                   pl.BlockSpec((1, HEADS_B, 1, V_ROWS_B, tm), lambda i: (i // chunks, 0, i % chunks, 0, 0))],
        out_shape=[jax.ShapeDtypeStruct((batch * wide, seq), BF16),
                   jax.ShapeDtypeStruct((t, wide), BF16),
                   jax.ShapeDtypeStruct((batch, HEADS_B, chunks, V_ROWS_B, tm), BF16)],
        compiler_params=_params("parallel"),
        name="mla_prep",
    )(mla, cos, sin, cos_t, sin_t, gq, wqa_t, wqb_t, gkv, wk, wv_t)


def _mla_attn_kernel(qt_ref, k_ref, vt_ref, o_ref, *, seq):
    n_sub = MLA_TK // MLA_KS
    nq = MLA_TQ // MLA_QS
    units = [(c, hh, j) for c in range(n_sub) for hh in range(MLA_HEADS_PER_STEP) for j in range(nq)]

    def scores(rows, hh, j):
        slot = slice(hh * HEAD_PAD_B, (hh + 1) * HEAD_PAD_B)
        return jnp.dot(k_ref[rows, slot], qt_ref[slot, j * MLA_QS:(j + 1) * MLA_QS],
                       preferred_element_type=F32)

    def sweep(kc, carry, update):
        def unit_scores(u):
            c, hh, j = units[u]
            return scores(pl.ds(pl.multiple_of(kc * MLA_TK + c * MLA_KS, MLA_KS), MLA_KS), hh, j)

        new = list(carry)
        pending = [unit_scores(u) for u in range(min(MLA_LOOKAHEAD, len(units)))]
        worst = None
        for u, (c, hh, j) in enumerate(units):
            if u + MLA_LOOKAHEAD < len(units):
                pending.append(unit_scores(u + MLA_LOOKAHEAD))
            v_chunk, v_off = divmod(c * MLA_KS, MLA_VCHUNK)
            v_blk = vt_ref[0, hh, kc * (MLA_TK // MLA_VCHUNK) + v_chunk, :, v_off:v_off + MLA_KS]
            new[hh * nq + j], gap = update(new[hh * nq + j], pending.pop(0), v_blk)
            worst = gap if worst is None else jnp.maximum(worst, gap)
        return tuple(new), worst

    def exact_update(state, s, v_blk):
        m, acc = state
        m_new = jnp.maximum(m, jnp.max(s, axis=0, keepdims=True))
        p = jnp.exp2(s - m_new).astype(BF16)
        acc = jnp.exp2(m - m_new) * acc + jnp.dot(v_blk, p, preferred_element_type=F32)
        return (m_new, acc), jnp.zeros_like(m)

    def lagged_update(state, s, v_blk):
        m, acc = state
        p = jnp.exp2(s - m).astype(BF16)
        col_max = jnp.max(s, axis=0, keepdims=True)
        gap = col_max - m
        m_new = jnp.maximum(m, col_max)
        acc = (acc + jnp.dot(v_blk, p, preferred_element_type=F32)) * jnp.exp2(m - m_new)
        return (m_new, acc), gap

    def step(kc, carry):
        fast, worst = sweep(kc, carry, lagged_update)
        overflow_risk = jnp.max(worst) > MLA_GAP_LIMIT
        return lax.cond(overflow_risk, lambda: sweep(kc, carry, exact_update)[0], lambda: fast)

    first = pl.ds(0, MLA_KS)
    init = tuple((jnp.max(scores(first, hh, j), axis=0, keepdims=True), jnp.zeros((V_ROWS_B, MLA_QS), F32))
                 for hh in range(MLA_HEADS_PER_STEP) for j in range(nq))
    final = lax.fori_loop(0, seq // MLA_TK, step, init)
    parts = [acc[0:V_DIM_B, :] / acc[V_DIM_B:V_DIM_B + 1, :] for _, acc in final]
    heads = [jnp.concatenate(parts[hh * nq:(hh + 1) * nq], axis=1) for hh in range(MLA_HEADS_PER_STEP)]
    o_ref[...] = jnp.concatenate(heads, axis=0).T.astype(BF16)


def _mla_attention(q_t, k, v_t, batch, seq):
    t = batch * seq
    tq = MLA_TQ
    steps = seq // tq
    chunks = seq // MLA_VCHUNK
    pairs = HEADS_B // MLA_HEADS_PER_STEP
    pair = MLA_HEADS_PER_STEP * HEAD_PAD_B
    return pl.pallas_call(
        functools.partial(_mla_attn_kernel, seq=seq),
        grid=(batch, pairs, steps),
        in_specs=[pl.BlockSpec((pair, tq), lambda b, hp, i: (b * pairs + hp, i)),
                  _resident((seq, pair), lambda b, hp, i: (b, hp)),
                  _resident((1, MLA_HEADS_PER_STEP, chunks, V_ROWS_B, MLA_VCHUNK), lambda b, hp, i: (b, hp, 0, 0, 0))],
        out_specs=pl.BlockSpec((tq, MLA_HEADS_PER_STEP * V_DIM_B), lambda b, hp, i: (b * steps + i, hp)),
        out_shape=jax.ShapeDtypeStruct((t, HEADS_B * V_DIM_B), BF16),
        compiler_params=_params("parallel", "parallel", "arbitrary"),
        name="mla_attention",
    )(q_t, k, v_t)


def _rope_tables(seq):
    inv_freq = 1.0 / (ROPE_THETA ** (jnp.arange(0, QK_ROPE, 2, dtype=F32) / QK_ROPE))
    ang = jnp.arange(seq, dtype=F32)[:, None] * inv_freq[None, :]
    cos, sin = jnp.cos(ang), jnp.sin(ang)
    pad = HEAD_PAD_B - QK_NOPE - QK_ROPE
    cos_s = jnp.concatenate([jnp.ones((seq, QK_NOPE), F32), cos, cos, jnp.zeros((seq, pad), F32)], axis=1)
    sin_s = jnp.concatenate([jnp.zeros((seq, QK_NOPE), F32), sin, sin, jnp.zeros((seq, pad), F32)], axis=1)
    return cos_s, sin_s, cos_s.T, sin_s.T


def _mem_kv_kernel(mem_ref, w_ref, o_ref):
    o_ref[...] = jnp.dot(mem_ref[...], w_ref[...], preferred_element_type=F32).astype(BF16)


def _mem_kv(mem_b, w):
    rows, d = mem_b.shape
    n = w.shape[1]
    tm = 256
    return pl.pallas_call(
        _mem_kv_kernel,
        grid=(rows // tm,),
        in_specs=[pl.BlockSpec((tm, d), lambda i: (i, 0)), pl.BlockSpec((d, n), lambda i: (0, 0))],
        out_specs=pl.BlockSpec((tm, n), lambda i: (i, 0)),
        out_shape=jax.ShapeDtypeStruct((rows, n), BF16),
        compiler_params=_params("parallel"),
        name="mem_kv",
    )(mem_b, w)


def _mem_attn_kernel(q_ref, kv_ref, o_ref):
    scale = HEAD_DIM_C ** -0.5
    for h in range(HEADS_C):
        lanes = slice(h * HEAD_DIM_C, (h + 1) * HEAD_DIM_C)
        k = kv_ref[:, lanes]
        v = kv_ref[:, OUT_C + h * HEAD_DIM_C:OUT_C + (h + 1) * HEAD_DIM_C]
        s = lax.dot_general(q_ref[:, lanes], k, NT_DIMS, preferred_element_type=F32) * scale
        m = jnp.max(s, axis=1, keepdims=True)
        p = jnp.exp(s - m)
        den = jnp.sum(p, axis=1, keepdims=True)
        o_ref[:, lanes] = (jnp.dot(p.astype(BF16), v, preferred_element_type=F32) / den).astype(BF16)


def _mem_attention(qc, kv, batch, seq, n_mem):
    t = batch * seq
    ts = 1024
    steps = seq // ts
    return pl.pallas_call(
        _mem_attn_kernel,
        grid=(batch, steps),
        in_specs=[pl.BlockSpec((ts, OUT_C), lambda b, i: (b * steps + i, 0)),
                  pl.BlockSpec((n_mem, 2 * OUT_C), lambda b, i: (b, 0))],
        out_specs=pl.BlockSpec((ts, OUT_C), lambda b, i: (b * steps + i, 0)),
        out_shape=jax.ShapeDtypeStruct((t, OUT_C), BF16),
        compiler_params=_params("parallel", "parallel"),
        name="mem_attention",
    )(qc, kv)


def _token_order(cm_ref, slab_ref):
    d, n = cm_ref.shape[1], cm_ref.shape[2]
    for r in range(d):
        blk = cm_ref[0, r].astype(F32)
        for s in range(LANE_SLABS):
            slab_ref[s, pl.ds(r, n, stride=d), :] = blk[:, s * 128:(s + 1) * 128]
    return jnp.concatenate([slab_ref[s] for s in range(LANE_SLABS)], axis=1)


def _merge_kernel(oa0_ref, oa1_ref, oa2_ref, l0_ref, l1_ref, l2_ref, ob_ref, oc_ref, gl_ref, x_ref,
                  wb_ref, wo_ref, g_ref, b_ref, xf_ref, slab_ref):
    l0 = l0_ref[...]
    l1, l2 = _token_order(l1_ref, slab_ref), _token_order(l2_ref, slab_ref)
    m = jnp.maximum(jnp.maximum(l0, l1), l2)
    e0, e1, e2 = jnp.exp(l0 - m), jnp.exp(l1 - m), jnp.exp(l2 - m)
    oa = e0 * oa0_ref[...].astype(F32) + e1 * _token_order(oa1_ref, slab_ref) + e2 * _token_order(oa2_ref, slab_ref)
    oa = (oa / (e0 + e1 + e2)).astype(BF16)
    z = None
    for i, o in enumerate((oa, ob_ref[...], oc_ref[...])):
        gate = jax.nn.sigmoid(gl_ref[:, i * D_MODEL:(i + 1) * D_MODEL].astype(F32))
        term = gate * jnp.dot(o, wb_ref[i], preferred_element_type=F32)
        z = term if z is None else z + term
    y = jnp.dot(z.astype(BF16), wo_ref[...], preferred_element_type=F32)
    x = _load_token_tiles(x_ref, PROJ_TM, X_SLABS)
    _store_token_tiles(xf_ref, _layer_norm_rows(DEEPNORM_ALPHA * x + y, g_ref[...], b_ref[...]), X_SLABS)


def _merge(oa, ob, oc, gl, x, row_offset, wb, wo, g, b, batch, seq):
    d = D_MODEL
    t = batch * seq
    tm = PROJ_TM
    tiles_per_seq = seq // tm
    assert row_offset % tm == 0
    row = lambda i: (i, 0)
    x_row = lambda i: (row_offset // tm + i, 0)
    fixed = lambda i: (0, 0)
    half = pl.BlockSpec((tm, BRANCH_WIDTH), row)

    def class_major(dil):
        return pl.BlockSpec((1, dil, tm // dil, GROUP_WIDTH_A),
                            lambda i: (i // tiles_per_seq, 0, i % tiles_per_seq, 0))

    cm1, cm2 = class_major(DIL_GROUPS[1][1]), class_major(DIL_GROUPS[2][1])
    (oa0, l0), (oa1, l1), (oa2, l2) = oa
    return pl.pallas_call(
        _merge_kernel,
        grid=(t // tm,),
        in_specs=[half, cm1, cm2, half, cm1, cm2, half, half,
                  pl.BlockSpec((tm, SEG_GL), row), pl.BlockSpec((tm * X_SLABS, 128), x_row),
                  _resident((N_BRANCH, BRANCH_WIDTH, d), lambda i: (0, 0, 0)),
                  _resident((d, d), fixed),
                  pl.BlockSpec((1, d), fixed), pl.BlockSpec((1, d), fixed)],
        out_specs=pl.BlockSpec((tm * X_SLABS, 128), x_row),
        out_shape=jax.ShapeDtypeStruct(x.shape, F32),
        input_output_aliases={9: 0},
        scratch_shapes=[pltpu.VMEM((LANE_SLABS, tm, 128), F32)],
        compiler_params=_params("arbitrary"),
        name="merge_ln1",
    )(oa0, oa1, oa2, l0, l1, l2, ob, oc, gl, x, wb, wo, g, b)


GATE_LANES = 128


def _first_index_of_max(vals, idx, axis, sentinel):
    mx = jnp.max(vals, axis=axis, keepdims=True)
    return jnp.min(jnp.where(vals == mx, idx, sentinel), axis=axis, keepdims=True)


def _router_kernel(x_ref, w_ref, bias_ref, xg_ref, code_ref):
    tm = code_ref.shape[1]
    x = _load_token_tiles(x_ref, tm, X_SLABS)
    logits = lax.dot_general(w_ref[...], x.astype(BF16), NT_DIMS, preferred_element_type=F32)
    scores = jax.nn.sigmoid(logits)
    choice = scores + bias_ref[...]
    neg = -jnp.inf

    c3 = choice.reshape(N_EXPERT_GROUPS, EXPERTS_PER_GROUP, tm)
    e_idx = lax.broadcasted_iota(jnp.int32, c3.shape, 1)
    first = jnp.max(c3, axis=1, keepdims=True)
    first_at = jnp.min(jnp.where(c3 == first, e_idx, EXPERTS_PER_GROUP), axis=1, keepdims=True)
    second = jnp.max(jnp.where(e_idx == first_at, neg, c3), axis=1, keepdims=True)
    group_score = (first + second).reshape(N_EXPERT_GROUPS, tm)

    g_idx = lax.broadcasted_iota(jnp.int32, group_score.shape, 0)
    group_sel = jnp.zeros(group_score.shape, jnp.bool_)
    for _ in range(TOPK_GROUPS):
        at = _first_index_of_max(group_score, g_idx, 0, N_EXPERT_GROUPS)
        hit = g_idx == at
        group_sel = group_sel | hit
        group_score = jnp.where(hit, neg, group_score)

    allowed = jnp.broadcast_to(group_sel.reshape(N_EXPERT_GROUPS, 1, tm), c3.shape).reshape(N_EXPERTS, tm)
    cand = jnp.where(allowed, choice, neg)
    x_idx = lax.broadcasted_iota(jnp.int32, cand.shape, 0)
    chosen = jnp.zeros(cand.shape, jnp.bool_)
    for _ in range(TOP_K):
        at = _first_index_of_max(cand, x_idx, 0, N_EXPERTS)
        hit = x_idx == at
        chosen = chosen | hit
        cand = jnp.where(hit, neg, cand)

    w_sel = jnp.where(chosen, scores, 0.0)
    gates = w_sel / jnp.sum(w_sel, axis=0, keepdims=True) * ROUTED_SCALE
    padded = jnp.concatenate([gates, jnp.zeros((GATE_LANES - N_EXPERTS, tm), F32)], axis=0)
    _store_token_tiles(xg_ref, x, XG_SLABS)
    xg_ref[pl.ds(X_SLABS, tm, stride=XG_SLABS), :] = padded.T
    bit = jnp.left_shift(1, g_idx)
    code_ref[...] = jnp.sum(jnp.where(group_sel, bit, 0), axis=0, keepdims=True)


def _router(x, w_t, bias):
    t, d = x.shape[0] // X_SLABS, D_MODEL
    tm = 1024
    return pl.pallas_call(
        _router_kernel,
        grid=(t // tm,),
        in_specs=[pl.BlockSpec((tm * X_SLABS, 128), lambda i: (i, 0)), pl.BlockSpec((N_EXPERTS, d), lambda i: (0, 0)),
                  pl.BlockSpec((N_EXPERTS, 1), lambda i: (0, 0))],
        out_specs=[pl.BlockSpec((tm * XG_SLABS, 128), lambda i: (i, 0)), pl.BlockSpec((1, tm), lambda i: (0, i))],
        out_shape=[jax.ShapeDtypeStruct((t * XG_SLABS, 128), F32), jax.ShapeDtypeStruct((1, t), jnp.int32)],
        compiler_params=_params("parallel"),
        name="router",
    )(x, w_t, bias)


MOE_TM = 1024
MOE_SUB = 128
MOE_NSUB = MOE_TM // MOE_SUB
TOKEN_BITS = 16


def _dispatch_plan(code):
    t = code.shape[1]
    tiles = t // MOE_TM
    assert t <= 1 << TOKEN_BITS
    key = jnp.sort(code[0] * (1 << TOKEN_BITS) + jnp.arange(t, dtype=jnp.int32))
    perm = key & ((1 << TOKEN_BITS) - 1)
    bits = ((key >> TOKEN_BITS)[:, None] >> jnp.arange(N_EXPERT_GROUPS, dtype=jnp.int32)[None, :]) & 1
    sub = jnp.max(bits.reshape(tiles, MOE_NSUB, MOE_SUB, N_EXPERT_GROUPS), axis=2)
    tile_active = jnp.max(sub, axis=1)
    n_active = jnp.sum(tile_active, axis=1).astype(jnp.int32)
    step = jnp.arange(N_EXPERT_GROUPS, dtype=jnp.int32)[None, :]
    odd = (jnp.arange(tiles, dtype=jnp.int32) % 2)[:, None]
    visit_rank = jnp.where(odd == 1, N_EXPERT_GROUPS - 1 - step, step)
    order = jnp.argsort((1 - tile_active) * N_EXPERT_GROUPS + visit_rank, axis=1).astype(jnp.int32)
    last = jnp.take_along_axis(order, jnp.maximum(n_active - 1, 0)[:, None], axis=1)
    groups = jnp.where(step < n_active[:, None], order, last)
    flags = jnp.transpose(sub, (0, 2, 1)).astype(jnp.int32)
    return perm.reshape(tiles, 1, MOE_TM), groups.reshape(-1), n_active, flags.reshape(-1)


def _token_dma(src_ref, dst_ref, src_token, dst_token, rows, sem):
    return pltpu.make_async_copy(src_ref.at[pl.ds(src_token * rows, rows)],
                                 dst_ref.at[pl.ds(dst_token * rows, rows)], sem)


DMA_THREADS = 2


def _gather_tokens_kernel(perm_ref, src_ref, out_ref, sem, *, rows):
    def start(q, carry):
        for u in range(DMA_THREADS):
            r = q * DMA_THREADS + u
            _token_dma(src_ref, out_ref, perm_ref[0, 0, r], r, rows, sem).start(priority=u)
        return carry

    lax.fori_loop(0, MOE_TM // DMA_THREADS, start, 0)
    pltpu.make_async_copy(src_ref.at[pl.ds(0, MOE_TM * rows)], out_ref, sem).wait()


def _gather_tokens(src, perm, rows):
    n = src.shape[0]
    return pl.pallas_call(
        functools.partial(_gather_tokens_kernel, rows=rows),
        grid=(n // (MOE_TM * rows),),
        in_specs=[pl.BlockSpec((1, 1, MOE_TM), lambda i: (i, 0, 0), memory_space=pltpu.SMEM),
                  pl.BlockSpec(memory_space=pl.ANY)],
        out_specs=pl.BlockSpec((MOE_TM * rows, 128), lambda i: (i, 0)),
        out_shape=jax.ShapeDtypeStruct(src.shape, src.dtype),
        scratch_shapes=[pltpu.SemaphoreType.DMA(())],
        compiler_params=_params("arbitrary"),
        name="moe_gather",
    )(perm, src)


def _scatter_tokens_kernel(perm_ref, src_ref, out_ref, sem, *, rows):
    def start(q, carry):
        for u in range(DMA_THREADS):
            r = q * DMA_THREADS + u
            _token_dma(src_ref, out_ref, r, perm_ref[0, 0, r], rows, sem).start(priority=u)
        return carry

    lax.fori_loop(0, MOE_TM // DMA_THREADS, start, 0)
    pltpu.make_async_copy(src_ref, out_ref.at[pl.ds(0, MOE_TM * rows)], sem).wait()


def _scatter_tokens(src, perm, rows):
    n = src.shape[0]
    return pl.pallas_call(
        functools.partial(_scatter_tokens_kernel, rows=rows),
        grid=(n // (MOE_TM * rows),),
        in_specs=[pl.BlockSpec((1, 1, MOE_TM), lambda i: (i, 0, 0), memory_space=pltpu.SMEM),
                  pl.BlockSpec((MOE_TM * rows, 128), lambda i: (i, 0))],
        out_specs=pl.BlockSpec(memory_space=pl.ANY),
        out_shape=jax.ShapeDtypeStruct(src.shape, src.dtype),
        scratch_shapes=[pltpu.SemaphoreType.DMA(())],
        compiler_params=_params("arbitrary"),
        name="moe_scatter",
    )(perm, src)


def _swiglu(xb, wg, wu):
    return jax.nn.silu(jnp.dot(xb, wg, preferred_element_type=F32)) * jnp.dot(xb, wu, preferred_element_type=F32)


def _moe_kernel(groups_ref, nact_ref, flags_ref, xg_ref, wg_ref, wu_ref, wd_ref, sg_ref, su_ref, sd_ref,
                g_ref, b_ref, o_ref, acc_ref, xb_ref, gates_ref):
    i = pl.program_id(0)
    j = pl.program_id(1)

    @pl.when(j == 0)
    def _():
        xb = _load_token_tiles(xg_ref, MOE_TM, XG_SLABS).astype(BF16)
        xb_ref[...] = xb
        gates_ref[...] = xg_ref[pl.ds(X_SLABS, MOE_TM, stride=XG_SLABS), :]
        h = _swiglu(xb, sg_ref[...], su_ref[...])
        acc_ref[...] = jnp.dot(h.astype(BF16), sd_ref[...], preferred_element_type=F32)

    @pl.when(j < nact_ref[i])
    def _():
        group = groups_ref[i * N_EXPERT_GROUPS + j]
        wd_all = wd_ref[...].reshape(EXPERTS_PER_GROUP * D_EXPERT, D_MODEL)
        for s in range(MOE_NSUB):
            @pl.when(flags_ref[(i * N_EXPERT_GROUPS + group) * MOE_NSUB + s] != 0)
            def _():
                rows = slice(s * MOE_SUB, (s + 1) * MOE_SUB)
                xb = xb_ref[rows, :]
                gates = gates_ref[rows, :]
                lane = lax.broadcasted_iota(jnp.int32, gates.shape, 1)
                hs = []
                for e in range(EXPERTS_PER_GROUP):
                    gate = jnp.sum(jnp.where(lane == group * EXPERTS_PER_GROUP + e, gates, 0.0),
                                   axis=1, keepdims=True)
                    hs.append((_swiglu(xb, wg_ref[e], wu_ref[e]) * gate).astype(BF16))
                acc_ref[rows, :] += jnp.dot(jnp.concatenate(hs, axis=1), wd_all, preferred_element_type=F32)

    @pl.when(j == pl.num_programs(1) - 1)
    def _():
        x = _load_token_tiles(xg_ref, MOE_TM, XG_SLABS)
        _store_token_tiles(o_ref, _layer_norm_rows(DEEPNORM_ALPHA * x + acc_ref[...], g_ref[...], b_ref[...]),
                           X_SLABS)


def _moe(xg_sorted, plan, wg, wu, wd, sg, su, sd, g, b):
    t = xg_sorted.shape[0] // XG_SLABS
    d, f, tm, ng = D_MODEL, D_EXPERT, MOE_TM, N_EXPERT_GROUPS
    _, groups, n_active, flags = plan
    row = lambda i, j, *_: (i, 0)
    fixed = lambda i, j, *_: (0, 0)
    expert_block = lambda i, j, groups_ref, *_: (groups_ref[i * ng + j], 0, 0)
    grid_spec = pltpu.PrefetchScalarGridSpec(
        num_scalar_prefetch=3,
        grid=(t // tm, ng),
        in_specs=[pl.BlockSpec((tm * XG_SLABS, 128), row),
                  pl.BlockSpec((EXPERTS_PER_GROUP, d, f), expert_block),
                  pl.BlockSpec((EXPERTS_PER_GROUP, d, f), expert_block),
                  pl.BlockSpec((EXPERTS_PER_GROUP, f, d), expert_block),
                  pl.BlockSpec((d, f), fixed), pl.BlockSpec((d, f), fixed), pl.BlockSpec((f, d), fixed),
                  pl.BlockSpec((1, d), fixed), pl.BlockSpec((1, d), fixed)],
        out_specs=pl.BlockSpec((tm * X_SLABS, 128), row),
        scratch_shapes=[pltpu.VMEM((tm, d), F32), pltpu.VMEM((tm, d), BF16), pltpu.VMEM((tm, GATE_LANES), F32)],
    )
    return pl.pallas_call(
        _moe_kernel,
        grid_spec=grid_spec,
        out_shape=jax.ShapeDtypeStruct((t * X_SLABS, 128), F32),
        compiler_params=_params("arbitrary", "arbitrary"),
        name="moe_ln2",
    )(groups, n_active, flags, xg_sorted, wg, wu, wd, sg, su, sd, g, b)


def _moe_layer(x, w_router_t, router_bias, wg, wu, wd, sg, su, sd, g, b):
    xg, code = _router(x, w_router_t, router_bias.reshape(N_EXPERTS, 1))
    plan = _dispatch_plan(code)
    perm = plan[0]
    y_sorted = _moe(_gather_tokens(xg, perm, XG_SLABS), plan, wg, wu, wd, sg, su, sd, g, b)
    return _scatter_tokens(y_sorted, perm, X_SLABS)


def _rotate_half_columns(w):
    half = QK_ROPE // 2
    return jnp.concatenate([-w[..., half:], w[..., :half]], axis=-1)


def _prep_in_proj(w_in):
    layers, d, _ = w_in.shape
    cuts = np.cumsum((WIDTH_A, WIDTH_A, WIDTH_A, Q_LORA, KV_LORA, QK_ROPE, OUT_C))
    qa, ka, va, cq, ckv, kr, qc, gl = jnp.split(w_in, [int(c) for c in cuts], axis=-1)
    lead = jnp.zeros((layers, d, QK_NOPE), w_in.dtype)
    tail = jnp.zeros((layers, d, HEAD_PAD_B - QK_NOPE - QK_ROPE), w_in.dtype)
    kr_slot = jnp.concatenate([lead, kr, tail], axis=-1)
    kr_rot_slot = jnp.concatenate([lead, _rotate_half_columns(kr), tail], axis=-1)
    per_group = [m[..., g * GROUP_WIDTH_A:(g + 1) * GROUP_WIDTH_A] for g in range(N_GROUPS_A) for m in (qa, ka, va)]
    return jnp.concatenate(per_group + [cq, ckv, kr_slot, kr_rot_slot, qc, gl], axis=-1).astype(BF16)


def _prep_mla_weights(w_q_up, w_kv_up):
    layers = w_q_up.shape[0]
    wq = w_q_up.reshape(layers, Q_LORA, HEADS_B, QK_NOPE + QK_ROPE)
    nope, rope = wq[..., :QK_NOPE], wq[..., QK_NOPE:]
    pad = HEAD_PAD_B - QK_NOPE - QK_ROPE
    zq = lambda n: jnp.zeros((layers, Q_LORA, HEADS_B, n), w_q_up.dtype)
    wqa = jnp.concatenate([nope, rope, zq(pad)], axis=-1)
    wqb = jnp.concatenate([zq(QK_NOPE), _rotate_half_columns(rope), zq(pad)], axis=-1)
    wkv = w_kv_up.reshape(layers, KV_LORA, HEADS_B, QK_NOPE + V_DIM_B)
    zk = lambda n: jnp.zeros((layers, KV_LORA, HEADS_B, n), w_kv_up.dtype)
    wk = jnp.concatenate([wkv[..., :QK_NOPE], zk(HEAD_PAD_B - QK_NOPE)], axis=-1)
    wv = jnp.concatenate([wkv[..., QK_NOPE:], zk(V_ROWS_B - V_DIM_B)], axis=-1)
    flat = lambda w: w.reshape(layers, w.shape[1], -1).astype(BF16)
    flat_t = lambda w: jnp.swapaxes(flat(w), 1, 2)
    return flat_t(wqa), flat_t(wqb), flat(wk), flat_t(wv)


def _encoder(groups, emb_g, emb_b, w, depth):
    d = groups[0][0].shape[-1]
    row2 = lambda v: v.reshape(1, -1)
    sizes = [x.shape[0] * x.shape[1] for x, _ in groups]
    offsets = [sum(sizes[:i]) for i in range(len(sizes))]
    xf = _embed_ln(jnp.concatenate([x.reshape(-1, d) for x, _ in groups], axis=0), row2(emb_g), row2(emb_b))
    mems = [mem.reshape(-1, d).astype(BF16) for _, mem in groups]
    tables = [_rope_tables(x.shape[1]) for x, _ in groups]
    for l in range(depth):
        for (x, mem), off, mem_b, table in zip(groups, offsets, mems, tables):
            batch, seq, _ = x.shape
            qkv0, cm1, cm2, mla, qc, gl = _project(xf, off, w["in_proj"][l], batch, seq)
            oa = _dilated_mixer((qkv0, cm1, cm2), batch, seq)
            q_t, k, v_t = _mla_prep(mla, table, row2(w["q_norm_g"][l]), w["wqa"][l], w["wqb"][l],
                                    row2(w["kv_norm_g"][l]), w["wk"][l], w["wv"][l], batch, seq)
            ob = _mla_attention(q_t, k, v_t, batch, seq)
            oc = _mem_attention(qc, _mem_kv(mem_b, w["mem_kv"][l]), batch, seq, mem.shape[1])
            xf = _merge(oa, ob, oc, gl, xf, off, w["branch"][l], w["out"][l],
                        row2(w["ln1_g"][l]), row2(w["ln1_b"][l]), batch, seq)
        xf = _moe_layer(xf, w["router_t"][l], w["router_bias"][l], w["exp_gate"][l],
                        w["exp_up"][l], w["exp_down"][l], w["sh_gate"][l], w["sh_up"][l], w["sh_down"][l],
                        row2(w["ln2_g"][l]), row2(w["ln2_b"][l]))
    out = xf.reshape(-1, d)
    return [out[off:off + n].reshape(x.shape) for (x, _), off, n in zip(groups, offsets, sizes)]


def kernel(x_prompt, x_sample, mem_prompt, mem_sample, emb_ln_g, emb_ln_b, w_in, q_norm_g, w_q_up, kv_norm_g,
           w_kv_up, w_mem_kv, w_branch, w_out, ln1_g, ln1_b, w_router, router_bias, w_exp_gate, w_exp_up,
           w_exp_down, w_sh_gate, w_sh_up, w_sh_down, ln2_g, ln2_b):
    wqa, wqb, wk, wv = _prep_mla_weights(w_q_up, w_kv_up)
    w = {
        "in_proj": _prep_in_proj(w_in),
        "q_norm_g": q_norm_g, "kv_norm_g": kv_norm_g, "wqa": wqa, "wqb": wqb, "wk": wk, "wv": wv,
        "mem_kv": w_mem_kv.astype(BF16), "branch": w_branch.astype(BF16), "out": w_out.astype(BF16),
        "ln1_g": ln1_g, "ln1_b": ln1_b,
        "router_t": jnp.swapaxes(w_router, 1, 2).astype(BF16), "router_bias": router_bias,
        "exp_gate": w_exp_gate.astype(BF16), "exp_up": w_exp_up.astype(BF16), "exp_down": w_exp_down.astype(BF16),
        "sh_gate": w_sh_gate.astype(BF16), "sh_up": w_sh_up.astype(BF16), "sh_down": w_sh_down.astype(BF16),
        "ln2_g": ln2_g, "ln2_b": ln2_b,
    }
    depth = w_in.shape[0]
    y_prompt, y_sample = _encoder([(x_prompt, mem_prompt), (x_sample, mem_sample)], emb_ln_g, emb_ln_b, w, depth)
    return (y_prompt, y_sample)
```

```python
import functools
import math

import numpy as np
import jax
import jax.numpy as jnp
from jax import lax
from jax.experimental import pallas as pl
from jax.experimental.pallas import tpu as pltpu

F32 = jnp.float32
BF16 = jnp.bfloat16

D_MODEL = 1024
DEPTH = 4
DIL_GROUPS = ((128, 1), (512, 4), (2048, 16))
N_GROUPS_A = 3
HEADS_A = 4
HEAD_DIM_A = 128
GROUP_WIDTH_A = HEADS_A * HEAD_DIM_A
WIDTH_A = N_GROUPS_A * GROUP_WIDTH_A
RADIUS_A = 64
HEADS_B = 8
Q_LORA = 256
KV_LORA = 128
QK_NOPE = 64
QK_ROPE = 32
V_DIM_B = 64
ROPE_THETA = 10000.0
HEAD_PAD_B = 128
HEADS_C = 4
HEAD_DIM_C = 128
OUT_C = HEADS_C * HEAD_DIM_C
N_BRANCH = 3
BRANCH_WIDTH = 512
N_EXPERTS = 64
TOP_K = 8
N_EXPERT_GROUPS = 8
EXPERTS_PER_GROUP = N_EXPERTS // N_EXPERT_GROUPS
TOPK_GROUPS = 4
D_EXPERT = 256
ROUTED_SCALE = 2.5
DEEPNORM_ALPHA = (2 * DEPTH) ** 0.25
LN_EPS = 1e-5
RMS_EPS = 1e-6

SEG_QKV = 3 * WIDTH_A
SEG_GROUP = 3 * GROUP_WIDTH_A
SEG_MLA = Q_LORA + KV_LORA + 2 * HEAD_PAD_B
SEG_QC = OUT_C
SEG_GL = N_BRANCH * D_MODEL
N_PROJ = SEG_QKV + SEG_MLA + SEG_QC + SEG_GL

NEG_BIG = -1e30
VMEM_LIMIT = 56 * 2 ** 20

NT_DIMS = (((1,), (1,)), ((), ()))


def _params(*sem):
    return pltpu.CompilerParams(dimension_semantics=sem, vmem_limit_bytes=VMEM_LIMIT)


def _resident(block_shape, index_map):
    return pl.BlockSpec(block_shape, index_map, pipeline_mode=pl.Buffered(1))


def _layer_norm_rows(h, g, b):
    mu = jnp.mean(h, axis=-1, keepdims=True)
    c = h - mu
    var = jnp.mean(c * c, axis=-1, keepdims=True)
    return c * lax.rsqrt(var + LN_EPS) * g + b


def _rms_norm_rows(h, g):
    return h * lax.rsqrt(jnp.mean(h * h, axis=-1, keepdims=True) + RMS_EPS) * g


X_SLABS = D_MODEL // 128
XG_SLABS = X_SLABS + 1


def _load_token_tiles(ref, n_tokens, rows_per_token):
    return jnp.concatenate([ref[pl.ds(c, n_tokens, stride=rows_per_token), :] for c in range(X_SLABS)], axis=1)


def _store_token_tiles(ref, value, rows_per_token):
    for c in range(X_SLABS):
        ref[pl.ds(c, value.shape[0], stride=rows_per_token), :] = value[:, c * 128:(c + 1) * 128]


def _embed_ln_kernel(x_ref, g_ref, b_ref, xf_ref):
    _store_token_tiles(xf_ref, _layer_norm_rows(x_ref[...], g_ref[...], b_ref[...]), X_SLABS)


def _embed_ln(x, g, b):
    t, d = x.shape
    tm = 512
    row = lambda i: (i, 0)
    fixed = lambda i: (0, 0)
    return pl.pallas_call(
        _embed_ln_kernel,
        grid=(t // tm,),
        in_specs=[pl.BlockSpec((tm, d), row), pl.BlockSpec((1, d), fixed), pl.BlockSpec((1, d), fixed)],
        out_specs=pl.BlockSpec((tm * X_SLABS, 128), row),
        out_shape=jax.ShapeDtypeStruct((t * X_SLABS, 128), F32),
        compiler_params=_params("parallel"),
        name="embed_ln",
    )(x, g, b)


PROJ_CHUNK = 512
PROJ_TM = 512
LANE_SLABS = GROUP_WIDTH_A // 128


def _proj_kernel(x_ref, w_ref, qkv0_ref, cm1_ref, cm2_ref, mla_ref, qc_ref, gl_ref, slab_ref):
    xb = _load_token_tiles(x_ref, PROJ_TM, X_SLABS).astype(BF16)

    def chunk(col, width):
        return jnp.dot(xb, w_ref[:, col:col + width], preferred_element_type=F32)

    col = 0
    for c in range(0, SEG_GROUP, PROJ_CHUNK):
        qkv0_ref[:, c:c + PROJ_CHUNK] = chunk(col + c, PROJ_CHUNK).astype(BF16)
    col += SEG_GROUP
    for ref, (_, d) in ((cm1_ref, DIL_GROUPS[1]), (cm2_ref, DIL_GROUPS[2])):
        n = PROJ_TM // d
        for c in range(0, SEG_GROUP, GROUP_WIDTH_A):
            res = chunk(col + c, GROUP_WIDTH_A)
            for s in range(LANE_SLABS):
                slab_ref[s] = res[:, s * 128:(s + 1) * 128]
            for r in range(d):
                piece = jnp.concatenate([slab_ref[s, pl.ds(r, n, stride=d), :] for s in range(LANE_SLABS)], axis=1)
                ref[0, r, :, c:c + GROUP_WIDTH_A] = piece.astype(BF16)
        col += SEG_GROUP
    for ref, width in ((mla_ref, SEG_MLA), (qc_ref, SEG_QC), (gl_ref, SEG_GL)):
        for c in range(0, width, PROJ_CHUNK):
            w = min(PROJ_CHUNK, width - c)
            ref[:, c:c + w] = chunk(col + c, w).astype(BF16)
        col += width


def _project(x, row_offset, w, batch, seq):
    d = D_MODEL
    t = batch * seq
    tm = PROJ_TM
    tiles_per_seq = seq // tm
    assert row_offset % tm == 0
    row = lambda i: (i, 0)
    x_row = lambda i: (row_offset // tm + i, 0)

    def class_major(dil):
        shape = (batch, dil, seq // dil, SEG_GROUP)
        spec = pl.BlockSpec((1, dil, tm // dil, SEG_GROUP), lambda i: (i // tiles_per_seq, 0, i % tiles_per_seq, 0))
        return jax.ShapeDtypeStruct(shape, BF16), spec

    (cm1_shape, cm1_spec), (cm2_shape, cm2_spec) = class_major(DIL_GROUPS[1][1]), class_major(DIL_GROUPS[2][1])
    flat = (SEG_GROUP, SEG_MLA, SEG_QC, SEG_GL)
    flat_shapes = [jax.ShapeDtypeStruct((t, n), BF16) for n in flat]
    flat_specs = [pl.BlockSpec((tm, n), row) for n in flat]
    return pl.pallas_call(
        _proj_kernel,
        grid=(t // tm,),
        in_specs=[pl.BlockSpec((tm * X_SLABS, 128), x_row), _resident((d, N_PROJ), lambda i: (0, 0))],
        out_specs=[flat_specs[0], cm1_spec, cm2_spec] + flat_specs[1:],
        out_shape=[flat_shapes[0], cm1_shape, cm2_shape] + flat_shapes[1:],
        scratch_shapes=[pltpu.VMEM((LANE_SLABS, tm, 128), F32)],
        compiler_params=_params("parallel"),
        name="in_proj",
    )(x, w)


BAND_TQ = 512
BAND_QB = 128
BAND_KB = BAND_QB + 2 * RADIUS_A


def _band_kernel(q_ref, kp_ref, km_ref, kn_ref, vp_ref, vm_ref, vn_ref, o_ref, lse_ref, k_scr, v_scr,
                 *, seq_len, slopes):
    i = pl.program_id(1)
    r = RADIUS_A
    k_scr[0:r, :] = kp_ref[...]
    k_scr[r:r + BAND_TQ, :] = km_ref[...]
    k_scr[r + BAND_TQ:, :] = kn_ref[...]
    v_scr[0:r, :] = vp_ref[...]
    v_scr[r:r + BAND_TQ, :] = vm_ref[...]
    v_scr[r + BAND_TQ:, :] = vn_ref[...]

    row = lax.broadcasted_iota(jnp.int32, (BAND_QB, BAND_KB), 0)
    col = lax.broadcasted_iota(jnp.int32, (BAND_QB, BAND_KB), 1)
    rel = col - r - row
    dist = jnp.abs(rel).astype(F32)
    in_band = jnp.abs(rel) <= r
    scale = HEAD_DIM_A ** -0.5

    for qb in range(BAND_TQ // BAND_QB):
        key_pos = i * BAND_TQ + qb * BAND_QB - r + col
        valid = in_band & (key_pos >= 0) & (key_pos < seq_len)
        for h in range(HEADS_A):
            lanes = slice(h * HEAD_DIM_A, (h + 1) * HEAD_DIM_A)
            q = q_ref[qb * BAND_QB:(qb + 1) * BAND_QB, lanes]
            k = k_scr[qb * BAND_QB:qb * BAND_QB + BAND_KB, lanes]
            v = v_scr[qb * BAND_QB:qb * BAND_QB + BAND_KB, lanes]
            s = lax.dot_general(q, k, NT_DIMS, preferred_element_type=F32)
            logits = jnp.where(valid, s * scale - slopes[h] * dist, NEG_BIG)
            m = jnp.max(logits, axis=1, keepdims=True)
            p = jnp.exp(logits - m)
            den = jnp.sum(p, axis=1, keepdims=True)
            o = jnp.dot(p.astype(BF16), v, preferred_element_type=F32) / den
            rows = slice(qb * BAND_QB, (qb + 1) * BAND_QB)
            o_ref[rows, lanes] = o.astype(BF16)
            lse_ref[rows, lanes] = jnp.broadcast_to(m + jnp.log(den), (BAND_QB, HEAD_DIM_A))


def _band_attention(q_src, k_src, v_src, n_seq, seq_len, slopes):
    tq, r = BAND_TQ, RADIUS_A
    assert seq_len % tq == 0 and tq % r == 0
    steps = seq_len // tq
    halo_per_tile = tq // r
    halo_blocks = seq_len // r

    def main_map(cb):
        return lambda n, i: (n * steps + i, cb)

    def prev_map(cb):
        return lambda n, i: (n * halo_blocks + jnp.maximum(i * halo_per_tile - 1, 0), cb)

    def next_map(cb):
        return lambda n, i: (n * halo_blocks + jnp.minimum((i + 1) * halo_per_tile, halo_blocks - 1), cb)

    (qa, qcb), (ka, kcb), (va, vcb) = q_src, k_src, v_src
    w = GROUP_WIDTH_A
    rows = n_seq * seq_len
    return pl.pallas_call(
        functools.partial(_band_kernel, seq_len=seq_len, slopes=slopes),
        grid=(n_seq, steps),
        in_specs=[
            pl.BlockSpec((tq, w), main_map(qcb)),
            pl.BlockSpec((r, w), prev_map(kcb)), pl.BlockSpec((tq, w), main_map(kcb)),
            pl.BlockSpec((r, w), next_map(kcb)),
            pl.BlockSpec((r, w), prev_map(vcb)), pl.BlockSpec((tq, w), main_map(vcb)),
            pl.BlockSpec((r, w), next_map(vcb)),
        ],
        out_specs=[pl.BlockSpec((tq, w), main_map(0)), pl.BlockSpec((tq, w), main_map(0))],
        out_shape=[jax.ShapeDtypeStruct((rows, w), BF16), jax.ShapeDtypeStruct((rows, w), F32)],
        scratch_shapes=[pltpu.VMEM((tq + 2 * r, w), BF16), pltpu.VMEM((tq + 2 * r, w), BF16)],
        compiler_params=_params("parallel", "parallel"),
        name="band_attention",
    )(qa, ka, ka, ka, va, va, va)


def _alibi_slopes():
    n = N_GROUPS_A * HEADS_A
    return [2.0 ** (-8.0 * (i + 1) / n) for i in range(n)]


def _dilated_mixer(group_qkv, batch, seq):
    slopes = _alibi_slopes()
    outs = []
    for g, (_, d) in enumerate(DIL_GROUPS):
        group_slopes = tuple(float(s * d) for s in slopes[g * HEADS_A:(g + 1) * HEADS_A])
        rows = group_qkv[g].reshape(batch * seq, SEG_GROUP)
        o, lse = _band_attention((rows, 0), (rows, 1), (rows, 2), batch * d, seq // d, group_slopes)
        if d > 1:
            o = o.reshape(batch, d, seq // d, GROUP_WIDTH_A)
            lse = lse.reshape(batch, d, seq // d, GROUP_WIDTH_A)
        outs.append((o, lse))
    return outs


V_ROWS_B = 80
MLA_TQ = 1024
MLA_TK = 4096
MLA_VCHUNK = 1024
MLA_KS = 256
MLA_QS = 512
MLA_LOOKAHEAD = 2
MLA_GAP_LIMIT = 64.0
MLA_HEADS_PER_STEP = 2


def _mla_prep_kernel(mla_ref, cos_ref, sin_ref, cos_t_ref, sin_t_ref, gq_ref, wqa_ref, wqb_ref, gkv_ref,
                     wk_ref, wv_ref, qt_ref, k_ref, vt_ref):
    m = mla_ref[...]
    cq = m[:, 0:Q_LORA].astype(F32)
    ckv = m[:, Q_LORA:Q_LORA + KV_LORA].astype(F32)
    kr = m[:, Q_LORA + KV_LORA:Q_LORA + KV_LORA + HEAD_PAD_B].astype(F32)
    kr_rot = m[:, Q_LORA + KV_LORA + HEAD_PAD_B:].astype(F32)
    scale = (QK_NOPE + QK_ROPE) ** -0.5 * math.log2(math.e)

    cqn = _rms_norm_rows(cq, gq_ref[...]).astype(BF16)
    qa_t = lax.dot_general(wqa_ref[...], cqn, NT_DIMS, preferred_element_type=F32)
    qb_t = lax.dot_general(wqb_ref[...], cqn, NT_DIMS, preferred_element_type=F32)
    ckvn = _rms_norm_rows(ckv, gkv_ref[...]).astype(BF16)
    kn = jnp.dot(ckvn, wk_ref[...], preferred_element_type=F32)
    v_t = lax.dot_general(wv_ref[...], ckvn, NT_DIMS, preferred_element_type=F32)
    k_rope = kr * cos_ref[...] + kr_rot * sin_ref[...]
    cos_t, sin_t = cos_t_ref[...], sin_t_ref[...]
    row = lax.broadcasted_iota(jnp.int32, (V_ROWS_B, 1), 0)
    ones_row = (row == V_DIM_B).astype(F32)
    for h in range(HEADS_B):
        slot = slice(h * HEAD_PAD_B, (h + 1) * HEAD_PAD_B)
        qt_ref[slot, :] = ((qa_t[slot, :] * cos_t + qb_t[slot, :] * sin_t) * scale).astype(BF16)
        k_ref[:, slot] = (kn[:, slot] + k_rope).astype(BF16)
        vt_ref[0, h, 0] = (v_t[h * V_ROWS_B:(h + 1) * V_ROWS_B, :] + ones_row).astype(BF16)


def _mla_prep(mla, tables, gq, wqa_t, wqb_t, gkv, wk, wv_t, batch, seq):
    t = mla.shape[0]
    tm = MLA_VCHUNK
    chunks = seq // tm
    cos, sin, cos_t, sin_t = tables
    row = lambda i: (i, 0)
    pos = lambda i: (i % chunks, 0)
    pos_t = lambda i: (0, i % chunks)
    fixed = lambda i: (0, 0)
    wide = HEADS_B * HEAD_PAD_B
    return pl.pallas_call(
        _mla_prep_kernel,
        grid=(t // tm,),
        in_specs=[pl.BlockSpec((tm, SEG_MLA), row),
                  pl.BlockSpec((tm, HEAD_PAD_B), pos), pl.BlockSpec((tm, HEAD_PAD_B), pos),
                  pl.BlockSpec((HEAD_PAD_B, tm), pos_t), pl.BlockSpec((HEAD_PAD_B, tm), pos_t),
                  pl.BlockSpec((1, Q_LORA), fixed), pl.BlockSpec((wide, Q_LORA), fixed),
                  pl.BlockSpec((wide, Q_LORA), fixed),
                  pl.BlockSpec((1, KV_LORA), fixed), pl.BlockSpec((KV_LORA, wide), fixed),
                  pl.BlockSpec((HEADS_B * V_ROWS_B, KV_LORA), fixed)],
        out_specs=[pl.BlockSpec((wide, tm), lambda i: (i // chunks, i % chunks)),
                   pl.BlockSpec((tm, wide), row),
                   pl.BlockSpec((1, HEADS_B, 1, V_ROWS_B, tm), lambda i: (i // chunks, 0, i % chunks, 0, 0))],
        out_shape=[jax.ShapeDtypeStruct((batch * wide, seq), BF16),
                   jax.ShapeDtypeStruct((t, wide), BF16),
                   jax.ShapeDtypeStruct((batch, HEADS_B, chunks, V_ROWS_B, tm), BF16)],
        compiler_params=_params("parallel"),
        name="mla_prep",
    )(mla, cos, sin, cos_t, sin_t, gq, wqa_t, wqb_t, gkv, wk, wv_t)


def _mla_attn_kernel(qt_ref, k_ref, vt_ref, o_ref, *, seq):
    n_sub = MLA_TK // MLA_KS
    nq = MLA_TQ // MLA_QS
    units = [(c, hh, j) for c in range(n_sub) for hh in range(MLA_HEADS_PER_STEP) for j in range(nq)]

    def scores(rows, hh, j):
        slot = slice(hh * HEAD_PAD_B, (hh + 1) * HEAD_PAD_B)
        return jnp.dot(k_ref[rows, slot], qt_ref[slot, j * MLA_QS:(j + 1) * MLA_QS],
                       preferred_element_type=F32)

    def sweep(kc, carry, update):
        def unit_scores(u):
            c, hh, j = units[u]
            return scores(pl.ds(pl.multiple_of(kc * MLA_TK + c * MLA_KS, MLA_KS), MLA_KS), hh, j)

        new = list(carry)
        pending = [unit_scores(u) for u in range(min(MLA_LOOKAHEAD, len(units)))]
        worst = None
        for u, (c, hh, j) in enumerate(units):
            if u + MLA_LOOKAHEAD < len(units):
                pending.append(unit_scores(u + MLA_LOOKAHEAD))
            v_chunk, v_off = divmod(c * MLA_KS, MLA_VCHUNK)
            v_blk = vt_ref[0, hh, kc * (MLA_TK // MLA_VCHUNK) + v_chunk, :, v_off:v_off + MLA_KS]
            new[hh * nq + j], gap = update(new[hh * nq + j], pending.pop(0), v_blk)
            worst = gap if worst is None else jnp.maximum(worst, gap)
        return tuple(new), worst

    def exact_update(state, s, v_blk):
        m, acc = state
        m_new = jnp.maximum(m, jnp.max(s, axis=0, keepdims=True))
        p = jnp.exp2(s - m_new).astype(BF16)
        acc = jnp.exp2(m - m_new) * acc + jnp.dot(v_blk, p, preferred_element_type=F32)
        return (m_new, acc), jnp.zeros_like(m)

    def lagged_update(state, s, v_blk):
        m, acc = state
        p = jnp.exp2(s - m).astype(BF16)
        col_max = jnp.max(s, axis=0, keepdims=True)
        gap = col_max - m
        m_new = jnp.maximum(m, col_max)
        acc = (acc + jnp.dot(v_blk, p, preferred_element_type=F32)) * jnp.exp2(m - m_new)
        return (m_new, acc), gap

    def step(kc, carry):
        fast, worst = sweep(kc, carry, lagged_update)
        overflow_risk = jnp.max(worst) > MLA_GAP_LIMIT
        return lax.cond(overflow_risk, lambda: sweep(kc, carry, exact_update)[0], lambda: fast)

    first = pl.ds(0, MLA_KS)
    init = tuple((jnp.max(scores(first, hh, j), axis=0, keepdims=True), jnp.zeros((V_ROWS_B, MLA_QS), F32))
                 for hh in range(MLA_HEADS_PER_STEP) for j in range(nq))
    final = lax.fori_loop(0, seq // MLA_TK, step, init)
    parts = [acc[0:V_DIM_B, :] / acc[V_DIM_B:V_DIM_B + 1, :] for _, acc in final]
    heads = [jnp.concatenate(parts[hh * nq:(hh + 1) * nq], axis=1) for hh in range(MLA_HEADS_PER_STEP)]
    o_ref[...] = jnp.concatenate(heads, axis=0).T.astype(BF16)


def _mla_attention(q_t, k, v_t, batch, seq):
    t = batch * seq
    tq = MLA_TQ
    steps = seq // tq
    chunks = seq // MLA_VCHUNK
    pairs = HEADS_B // MLA_HEADS_PER_STEP
    pair = MLA_HEADS_PER_STEP * HEAD_PAD_B
    return pl.pallas_call(
        functools.partial(_mla_attn_kernel, seq=seq),
        grid=(batch, pairs, steps),
        in_specs=[pl.BlockSpec((pair, tq), lambda b, hp, i: (b * pairs + hp, i)),
                  _resident((seq, pair), lambda b, hp, i: (b, hp)),
                  _resident((1, MLA_HEADS_PER_STEP, chunks, V_ROWS_B, MLA_VCHUNK), lambda b, hp, i: (b, hp, 0, 0, 0))],
        out_specs=pl.BlockSpec((tq, MLA_HEADS_PER_STEP * V_DIM_B), lambda b, hp, i: (b * steps + i, hp)),
        out_shape=jax.ShapeDtypeStruct((t, HEADS_B * V_DIM_B), BF16),
        compiler_params=_params("parallel", "parallel", "arbitrary"),
        name="mla_attention",
    )(q_t, k, v_t)


def _rope_tables(seq):
    inv_freq = 1.0 / (ROPE_THETA ** (jnp.arange(0, QK_ROPE, 2, dtype=F32) / QK_ROPE))
    ang = jnp.arange(seq, dtype=F32)[:, None] * inv_freq[None, :]
    cos, sin = jnp.cos(ang), jnp.sin(ang)
    pad = HEAD_PAD_B - QK_NOPE - QK_ROPE
    cos_s = jnp.concatenate([jnp.ones((seq, QK_NOPE), F32), cos, cos, jnp.zeros((seq, pad), F32)], axis=1)
    sin_s = jnp.concatenate([jnp.zeros((seq, QK_NOPE), F32), sin, sin, jnp.zeros((seq, pad), F32)], axis=1)
    return cos_s, sin_s, cos_s.T, sin_s.T


def _mem_kv_kernel(mem_ref, w_ref, o_ref):
    o_ref[...] = jnp.dot(mem_ref[...], w_ref[...], preferred_element_type=F32).astype(BF16)


def _mem_kv(mem_b, w):
    rows, d = mem_b.shape
    n = w.shape[1]
    tm = 256
    return pl.pallas_call(
        _mem_kv_kernel,
        grid=(rows // tm,),
        in_specs=[pl.BlockSpec((tm, d), lambda i: (i, 0)), pl.BlockSpec((d, n), lambda i: (0, 0))],
        out_specs=pl.BlockSpec((tm, n), lambda i: (i, 0)),
        out_shape=jax.ShapeDtypeStruct((rows, n), BF16),
        compiler_params=_params("parallel"),
        name="mem_kv",
    )(mem_b, w)


def _mem_attn_kernel(q_ref, kv_ref, o_ref):
    scale = HEAD_DIM_C ** -0.5
    for h in range(HEADS_C):
        lanes = slice(h * HEAD_DIM_C, (h + 1) * HEAD_DIM_C)
        k = kv_ref[:, lanes]
        v = kv_ref[:, OUT_C + h * HEAD_DIM_C:OUT_C + (h + 1) * HEAD_DIM_C]
        s = lax.dot_general(q_ref[:, lanes], k, NT_DIMS, preferred_element_type=F32) * scale
        m = jnp.max(s, axis=1, keepdims=True)
        p = jnp.exp(s - m)
        den = jnp.sum(p, axis=1, keepdims=True)
        o_ref[:, lanes] = (jnp.dot(p.astype(BF16), v, preferred_element_type=F32) / den).astype(BF16)


def _mem_attention(qc, kv, batch, seq, n_mem):
    t = batch * seq
    ts = 1024
    steps = seq // ts
    return pl.pallas_call(
        _mem_attn_kernel,
        grid=(batch, steps),
        in_specs=[pl.BlockSpec((ts, OUT_C), lambda b, i: (b * steps + i, 0)),
                  pl.BlockSpec((n_mem, 2 * OUT_C), lambda b, i: (b, 0))],
        out_specs=pl.BlockSpec((ts, OUT_C), lambda b, i: (b * steps + i, 0)),
        out_shape=jax.ShapeDtypeStruct((t, OUT_C), BF16),
        compiler_params=_params("parallel", "parallel"),
        name="mem_attention",
    )(qc, kv)


def _token_order(cm_ref, slab_ref):
    d, n = cm_ref.shape[1], cm_ref.shape[2]
    for r in range(d):
        blk = cm_ref[0, r].astype(F32)
        for s in range(LANE_SLABS):
            slab_ref[s, pl.ds(r, n, stride=d), :] = blk[:, s * 128:(s + 1) * 128]
    return jnp.concatenate([slab_ref[s] for s in range(LANE_SLABS)], axis=1)


def _merge_kernel(oa0_ref, oa1_ref, oa2_ref, l0_ref, l1_ref, l2_ref, ob_ref, oc_ref, gl_ref, x_ref,
                  wb_ref, wo_ref, g_ref, b_ref, xf_ref, slab_ref):
    l0 = l0_ref[...]
    l1, l2 = _token_order(l1_ref, slab_ref), _token_order(l2_ref, slab_ref)
    m = jnp.maximum(jnp.maximum(l0, l1), l2)
    e0, e1, e2 = jnp.exp(l0 - m), jnp.exp(l1 - m), jnp.exp(l2 - m)
    oa = e0 * oa0_ref[...].astype(F32) + e1 * _token_order(oa1_ref, slab_ref) + e2 * _token_order(oa2_ref, slab_ref)
    oa = (oa / (e0 + e1 + e2)).astype(BF16)
    z = None
    for i, o in enumerate((oa, ob_ref[...], oc_ref[...])):
        gate = jax.nn.sigmoid(gl_ref[:, i * D_MODEL:(i + 1) * D_MODEL].astype(F32))
        term = gate * jnp.dot(o, wb_ref[i], preferred_element_type=F32)
        z = term if z is None else z + term
    y = jnp.dot(z.astype(BF16), wo_ref[...], preferred_element_type=F32)
    x = _load_token_tiles(x_ref, PROJ_TM, X_SLABS)
    _store_token_tiles(xf_ref, _layer_norm_rows(DEEPNORM_ALPHA * x + y, g_ref[...], b_ref[...]), X_SLABS)


def _merge(oa, ob, oc, gl, x, row_offset, wb, wo, g, b, batch, seq):
    d = D_MODEL
    t = batch * seq
    tm = PROJ_TM
    tiles_per_seq = seq // tm
    assert row_offset % tm == 0
    row = lambda i: (i, 0)
    x_row = lambda i: (row_offset // tm + i, 0)
    fixed = lambda i: (0, 0)
    half = pl.BlockSpec((tm, BRANCH_WIDTH), row)

    def class_major(dil):
        return pl.BlockSpec((1, dil, tm // dil, GROUP_WIDTH_A),
                            lambda i: (i // tiles_per_seq, 0, i % tiles_per_seq, 0))

    cm1, cm2 = class_major(DIL_GROUPS[1][1]), class_major(DIL_GROUPS[2][1])
    (oa0, l0), (oa1, l1), (oa2, l2) = oa
    return pl.pallas_call(
        _merge_kernel,
        grid=(t // tm,),
        in_specs=[half, cm1, cm2, half, cm1, cm2, half, half,
                  pl.BlockSpec((tm, SEG_GL), row), pl.BlockSpec((tm * X_SLABS, 128), x_row),
                  _resident((N_BRANCH, BRANCH_WIDTH, d), lambda i: (0, 0, 0)),
                  _resident((d, d), fixed),
                  pl.BlockSpec((1, d), fixed), pl.BlockSpec((1, d), fixed)],
        out_specs=pl.BlockSpec((tm * X_SLABS, 128), x_row),
        out_shape=jax.ShapeDtypeStruct(x.shape, F32),
        input_output_aliases={9: 0},
        scratch_shapes=[pltpu.VMEM((LANE_SLABS, tm, 128), F32)],
        compiler_params=_params("arbitrary"),
        name="merge_ln1",
    )(oa0, oa1, oa2, l0, l1, l2, ob, oc, gl, x, wb, wo, g, b)


GATE_LANES = 128


def _first_index_of_max(vals, idx, axis, sentinel):
    mx = jnp.max(vals, axis=axis, keepdims=True)
    return jnp.min(jnp.where(vals == mx, idx, sentinel), axis=axis, keepdims=True)


def _router_kernel(x_ref, w_ref, bias_ref, xg_ref, code_ref):
    tm = code_ref.shape[1]
    x = _load_token_tiles(x_ref, tm, X_SLABS)
    logits = lax.dot_general(w_ref[...], x.astype(BF16), NT_DIMS, preferred_element_type=F32)
    scores = jax.nn.sigmoid(logits)
    choice = scores + bias_ref[...]
    neg = -jnp.inf

    c3 = choice.reshape(N_EXPERT_GROUPS, EXPERTS_PER_GROUP, tm)
    e_idx = lax.broadcasted_iota(jnp.int32, c3.shape, 1)
    first = jnp.max(c3, axis=1, keepdims=True)
    first_at = jnp.min(jnp.where(c3 == first, e_idx, EXPERTS_PER_GROUP), axis=1, keepdims=True)
    second = jnp.max(jnp.where(e_idx == first_at, neg, c3), axis=1, keepdims=True)
    group_score = (first + second).reshape(N_EXPERT_GROUPS, tm)

    g_idx = lax.broadcasted_iota(jnp.int32, group_score.shape, 0)
    group_sel = jnp.zeros(group_score.shape, jnp.bool_)
    for _ in range(TOPK_GROUPS):
        at = _first_index_of_max(group_score, g_idx, 0, N_EXPERT_GROUPS)
        hit = g_idx == at
        group_sel = group_sel | hit
        group_score = jnp.where(hit, neg, group_score)

    allowed = jnp.broadcast_to(group_sel.reshape(N_EXPERT_GROUPS, 1, tm), c3.shape).reshape(N_EXPERTS, tm)
    cand = jnp.where(allowed, choice, neg)
    x_idx = lax.broadcasted_iota(jnp.int32, cand.shape, 0)
    chosen = jnp.zeros(cand.shape, jnp.bool_)
    for _ in range(TOP_K):
        at = _first_index_of_max(cand, x_idx, 0, N_EXPERTS)
        hit = x_idx == at
        chosen = chosen | hit
        cand = jnp.where(hit, neg, cand)

    w_sel = jnp.where(chosen, scores, 0.0)
    gates = w_sel / jnp.sum(w_sel, axis=0, keepdims=True) * ROUTED_SCALE
    padded = jnp.concatenate([gates, jnp.zeros((GATE_LANES - N_EXPERTS, tm), F32)], axis=0)
    _store_token_tiles(xg_ref, x, XG_SLABS)
    xg_ref[pl.ds(X_SLABS, tm, stride=XG_SLABS), :] = padded.T
    bit = jnp.left_shift(1, g_idx)
    code_ref[...] = jnp.sum(jnp.where(group_sel, bit, 0), axis=0, keepdims=True)


def _router(x, w_t, bias):
    t, d = x.shape[0] // X_SLABS, D_MODEL
    tm = 1024
    return pl.pallas_call(
        _router_kernel,
        grid=(t // tm,),
        in_specs=[pl.BlockSpec((tm * X_SLABS, 128), lambda i: (i, 0)), pl.BlockSpec((N_EXPERTS, d), lambda i: (0, 0)),
                  pl.BlockSpec((N_EXPERTS, 1), lambda i: (0, 0))],
        out_specs=[pl.BlockSpec((tm * XG_SLABS, 128), lambda i: (i, 0)), pl.BlockSpec((1, tm), lambda i: (0, i))],
        out_shape=[jax.ShapeDtypeStruct((t * XG_SLABS, 128), F32), jax.ShapeDtypeStruct((1, t), jnp.int32)],
        compiler_params=_params("parallel"),
        name="router",
    )(x, w_t, bias)


MOE_TM = 1024
MOE_SUB = 128
MOE_NSUB = MOE_TM // MOE_SUB
TOKEN_BITS = 16


def _dispatch_plan(code):
    t = code.shape[1]
    tiles = t // MOE_TM
    assert t <= 1 << TOKEN_BITS
    key = jnp.sort(code[0] * (1 << TOKEN_BITS) + jnp.arange(t, dtype=jnp.int32))
    perm = key & ((1 << TOKEN_BITS) - 1)
    bits = ((key >> TOKEN_BITS)[:, None] >> jnp.arange(N_EXPERT_GROUPS, dtype=jnp.int32)[None, :]) & 1
    sub = jnp.max(bits.reshape(tiles, MOE_NSUB, MOE_SUB, N_EXPERT_GROUPS), axis=2)
    tile_active = jnp.max(sub, axis=1)
    n_active = jnp.sum(tile_active, axis=1).astype(jnp.int32)
    step = jnp.arange(N_EXPERT_GROUPS, dtype=jnp.int32)[None, :]
    odd = (jnp.arange(tiles, dtype=jnp.int32) % 2)[:, None]
    visit_rank = jnp.where(odd == 1, N_EXPERT_GROUPS - 1 - step, step)
    order = jnp.argsort((1 - tile_active) * N_EXPERT_GROUPS + visit_rank, axis=1).astype(jnp.int32)
    last = jnp.take_along_axis(order, jnp.maximum(n_active - 1, 0)[:, None], axis=1)
    groups = jnp.where(step < n_active[:, None], order, last)
    flags = jnp.transpose(sub, (0, 2, 1)).astype(jnp.int32)
    return perm.reshape(tiles, 1, MOE_TM), groups.reshape(-1), n_active, flags.reshape(-1)


def _token_dma(src_ref, dst_ref, src_token, dst_token, rows, sem):
    return pltpu.make_async_copy(src_ref.at[pl.ds(src_token * rows, rows)],
                                 dst_ref.at[pl.ds(dst_token * rows, rows)], sem)


DMA_THREADS = 2


def _gather_tokens_kernel(perm_ref, src_ref, out_ref, sem, *, rows):
    def start(q, carry):
        for u in range(DMA_THREADS):
            r = q * DMA_THREADS + u
            _token_dma(src_ref, out_ref, perm_ref[0, 0, r], r, rows, sem).start(priority=u)
        return carry

    lax.fori_loop(0, MOE_TM // DMA_THREADS, start, 0)
    pltpu.make_async_copy(src_ref.at[pl.ds(0, MOE_TM * rows)], out_ref, sem).wait()


def _gather_tokens(src, perm, rows):
    n = src.shape[0]
    return pl.pallas_call(
        functools.partial(_gather_tokens_kernel, rows=rows),
        grid=(n // (MOE_TM * rows),),
        in_specs=[pl.BlockSpec((1, 1, MOE_TM), lambda i: (i, 0, 0), memory_space=pltpu.SMEM),
                  pl.BlockSpec(memory_space=pl.ANY)],
        out_specs=pl.BlockSpec((MOE_TM * rows, 128), lambda i: (i, 0)),
        out_shape=jax.ShapeDtypeStruct(src.shape, src.dtype),
        scratch_shapes=[pltpu.SemaphoreType.DMA(())],
        compiler_params=_params("arbitrary"),
        name="moe_gather",
    )(perm, src)


def _scatter_tokens_kernel(perm_ref, src_ref, out_ref, sem, *, rows):
    def start(q, carry):
        for u in range(DMA_THREADS):
            r = q * DMA_THREADS + u
            _token_dma(src_ref, out_ref, r, perm_ref[0, 0, r], rows, sem).start(priority=u)
        return carry

    lax.fori_loop(0, MOE_TM // DMA_THREADS, start, 0)
    pltpu.make_async_copy(src_ref, out_ref.at[pl.ds(0, MOE_TM * rows)], sem).wait()


def _scatter_tokens(src, perm, rows):
    n = src.shape[0]
    return pl.pallas_call(
        functools.partial(_scatter_tokens_kernel, rows=rows),
        grid=(n // (MOE_TM * rows),),
        in_specs=[pl.BlockSpec((1, 1, MOE_TM), lambda i: (i, 0, 0), memory_space=pltpu.SMEM),
                  pl.BlockSpec((MOE_TM * rows, 128), lambda i: (i, 0))],
        out_specs=pl.BlockSpec(memory_space=pl.ANY),
        out_shape=jax.ShapeDtypeStruct(src.shape, src.dtype),
        scratch_shapes=[pltpu.SemaphoreType.DMA(())],
        compiler_params=_params("arbitrary"),
        name="moe_scatter",
    )(perm, src)


def _swiglu(xb, wg, wu):
    return jax.nn.silu(jnp.dot(xb, wg, preferred_element_type=F32)) * jnp.dot(xb, wu, preferred_element_type=F32)


def _moe_kernel(groups_ref, nact_ref, flags_ref, xg_ref, wg_ref, wu_ref, wd_ref, sg_ref, su_ref, sd_ref,
                g_ref, b_ref, o_ref, acc_ref, xb_ref, gates_ref):
    i = pl.program_id(0)
    j = pl.program_id(1)

    @pl.when(j == 0)
    def _():
        xb = _load_token_tiles(xg_ref, MOE_TM, XG_SLABS).astype(BF16)
        xb_ref[...] = xb
        gates_ref[...] = xg_ref[pl.ds(X_SLABS, MOE_TM, stride=XG_SLABS), :]
        h = _swiglu(xb, sg_ref[...], su_ref[...])
        acc_ref[...] = jnp.dot(h.astype(BF16), sd_ref[...], preferred_element_type=F32)

    @pl.when(j < nact_ref[i])
    def _():
        group = groups_ref[i * N_EXPERT_GROUPS + j]
        wd_all = wd_ref[...].reshape(EXPERTS_PER_GROUP * D_EXPERT, D_MODEL)
        for s in range(MOE_NSUB):
            @pl.when(flags_ref[(i * N_EXPERT_GROUPS + group) * MOE_NSUB + s] != 0)
            def _():
                rows = slice(s * MOE_SUB, (s + 1) * MOE_SUB)
                xb = xb_ref[rows, :]
                gates = gates_ref[rows, :]
                lane = lax.broadcasted_iota(jnp.int32, gates.shape, 1)
                hs = []
                for e in range(EXPERTS_PER_GROUP):
                    gate = jnp.sum(jnp.where(lane == group * EXPERTS_PER_GROUP + e, gates, 0.0),
                                   axis=1, keepdims=True)
                    hs.append((_swiglu(xb, wg_ref[e], wu_ref[e]) * gate).astype(BF16))
                acc_ref[rows, :] += jnp.dot(jnp.concatenate(hs, axis=1), wd_all, preferred_element_type=F32)

    @pl.when(j == pl.num_programs(1) - 1)
    def _():
        x = _load_token_tiles(xg_ref, MOE_TM, XG_SLABS)
        _store_token_tiles(o_ref, _layer_norm_rows(DEEPNORM_ALPHA * x + acc_ref[...], g_ref[...], b_ref[...]),
                           X_SLABS)


def _moe(xg_sorted, plan, wg, wu, wd, sg, su, sd, g, b):
    t = xg_sorted.shape[0] // XG_SLABS
    d, f, tm, ng = D_MODEL, D_EXPERT, MOE_TM, N_EXPERT_GROUPS
    _, groups, n_active, flags = plan
    row = lambda i, j, *_: (i, 0)
    fixed = lambda i, j, *_: (0, 0)
    expert_block = lambda i, j, groups_ref, *_: (groups_ref[i * ng + j], 0, 0)
    grid_spec = pltpu.PrefetchScalarGridSpec(
        num_scalar_prefetch=3,
        grid=(t // tm, ng),
        in_specs=[pl.BlockSpec((tm * XG_SLABS, 128), row),
                  pl.BlockSpec((EXPERTS_PER_GROUP, d, f), expert_block),
                  pl.BlockSpec((EXPERTS_PER_GROUP, d, f), expert_block),
                  pl.BlockSpec((EXPERTS_PER_GROUP, f, d), expert_block),
                  pl.BlockSpec((d, f), fixed), pl.BlockSpec((d, f), fixed), pl.BlockSpec((f, d), fixed),
                  pl.BlockSpec((1, d), fixed), pl.BlockSpec((1, d), fixed)],
        out_specs=pl.BlockSpec((tm * X_SLABS, 128), row),
        scratch_shapes=[pltpu.VMEM((tm, d), F32), pltpu.VMEM((tm, d), BF16), pltpu.VMEM((tm, GATE_LANES), F32)],
    )
    return pl.pallas_call(
        _moe_kernel,
        grid_spec=grid_spec,
        out_shape=jax.ShapeDtypeStruct((t * X_SLABS, 128), F32),
        compiler_params=_params("arbitrary", "arbitrary"),
        name="moe_ln2",
    )(groups, n_active, flags, xg_sorted, wg, wu, wd, sg, su, sd, g, b)


def _moe_layer(x, w_router_t, router_bias, wg, wu, wd, sg, su, sd, g, b):
    xg, code = _router(x, w_router_t, router_bias.reshape(N_EXPERTS, 1))
    plan = _dispatch_plan(code)
    perm = plan[0]
    y_sorted = _moe(_gather_tokens(xg, perm, XG_SLABS), plan, wg, wu, wd, sg, su, sd, g, b)
    return _scatter_tokens(y_sorted, perm, X_SLABS)


def _rotate_half_columns(w):
    half = QK_ROPE // 2
    return jnp.concatenate([-w[..., half:], w[..., :half]], axis=-1)


def _prep_in_proj(w_in):
    layers, d, _ = w_in.shape
    cuts = np.cumsum((WIDTH_A, WIDTH_A, WIDTH_A, Q_LORA, KV_LORA, QK_ROPE, OUT_C))
    qa, ka, va, cq, ckv, kr, qc, gl = jnp.split(w_in, [int(c) for c in cuts], axis=-1)
    lead = jnp.zeros((layers, d, QK_NOPE), w_in.dtype)
    tail = jnp.zeros((layers, d, HEAD_PAD_B - QK_NOPE - QK_ROPE), w_in.dtype)
    kr_slot = jnp.concatenate([lead, kr, tail], axis=-1)
    kr_rot_slot = jnp.concatenate([lead, _rotate_half_columns(kr), tail], axis=-1)
    per_group = [m[..., g * GROUP_WIDTH_A:(g + 1) * GROUP_WIDTH_A] for g in range(N_GROUPS_A) for m in (qa, ka, va)]
    return jnp.concatenate(per_group + [cq, ckv, kr_slot, kr_rot_slot, qc, gl], axis=-1).astype(BF16)


def _prep_mla_weights(w_q_up, w_kv_up):
    layers = w_q_up.shape[0]
    wq = w_q_up.reshape(layers, Q_LORA, HEADS_B, QK_NOPE + QK_ROPE)
    nope, rope = wq[..., :QK_NOPE], wq[..., QK_NOPE:]
    pad = HEAD_PAD_B - QK_NOPE - QK_ROPE
    zq = lambda n: jnp.zeros((layers, Q_LORA, HEADS_B, n), w_q_up.dtype)
    wqa = jnp.concatenate([nope, rope, zq(pad)], axis=-1)
    wqb = jnp.concatenate([zq(QK_NOPE), _rotate_half_columns(rope), zq(pad)], axis=-1)
    wkv = w_kv_up.reshape(layers, KV_LORA, HEADS_B, QK_NOPE + V_DIM_B)
    zk = lambda n: jnp.zeros((layers, KV_LORA, HEADS_B, n), w_kv_up.dtype)
    wk = jnp.concatenate([wkv[..., :QK_NOPE], zk(HEAD_PAD_B - QK_NOPE)], axis=-1)
    wv = jnp.concatenate([wkv[..., QK_NOPE:], zk(V_ROWS_B - V_DIM_B)], axis=-1)
    flat = lambda w: w.reshape(layers, w.shape[1], -1).astype(BF16)
    flat_t = lambda w: jnp.swapaxes(flat(w), 1, 2)
    return flat_t(wqa), flat_t(wqb), flat(wk), flat_t(wv)


def _encoder(groups, emb_g, emb_b, w, depth):
    d = groups[0][0].shape[-1]
    row2 = lambda v: v.reshape(1, -1)
    sizes = [x.shape[0] * x.shape[1] for x, _ in groups]
    offsets = [sum(sizes[:i]) for i in range(len(sizes))]
    xf = _embed_ln(jnp.concatenate([x.reshape(-1, d) for x, _ in groups], axis=0), row2(emb_g), row2(emb_b))
    mems = [mem.reshape(-1, d).astype(BF16) for _, mem in groups]
    tables = [_rope_tables(x.shape[1]) for x, _ in groups]
    for l in range(depth):
        for (x, mem), off, mem_b, table in zip(groups, offsets, mems, tables):
            batch, seq, _ = x.shape
            qkv0, cm1, cm2, mla, qc, gl = _project(xf, off, w["in_proj"][l], batch, seq)
            oa = _dilated_mixer((qkv0, cm1, cm2), batch, seq)
            q_t, k, v_t = _mla_prep(mla, table, row2(w["q_norm_g"][l]), w["wqa"][l], w["wqb"][l],
                                    row2(w["kv_norm_g"][l]), w["wk"][l], w["wv"][l], batch, seq)
            ob = _mla_attention(q_t, k, v_t, batch, seq)
            oc = _mem_attention(qc, _mem_kv(mem_b, w["mem_kv"][l]), batch, seq, mem.shape[1])
            xf = _merge(oa, ob, oc, gl, xf, off, w["branch"][l], w["out"][l],
                        row2(w["ln1_g"][l]), row2(w["ln1_b"][l]), batch, seq)
        xf = _moe_layer(xf, w["router_t"][l], w["router_bias"][l], w["exp_gate"][l],
                        w["exp_up"][l], w["exp_down"][l], w["sh_gate"][l], w["sh_up"][l], w["sh_down"][l],
                        row2(w["ln2_g"][l]), row2(w["ln2_b"][l]))
    out = xf.reshape(-1, d)
    return [out[off:off + n].reshape(x.shape) for (x, _), off, n in zip(groups, offsets, sizes)]


def kernel(x_prompt, x_sample, mem_prompt, mem_sample, emb_ln_g, emb_ln_b, w_in, q_norm_g, w_q_up, kv_norm_g,
           w_kv_up, w_mem_kv, w_branch, w_out, ln1_g, ln1_b, w_router, router_bias, w_exp_gate, w_exp_up,
           w_exp_down, w_sh_gate, w_sh_up, w_sh_down, ln2_g, ln2_b):
    wqa, wqb, wk, wv = _prep_mla_weights(w_q_up, w_kv_up)
    w = {
        "in_proj": _prep_in_proj(w_in),
        "q_norm_g": q_norm_g, "kv_norm_g": kv_norm_g, "wqa": wqa, "wqb": wqb, "wk": wk, "wv": wv,
        "mem_kv": w_mem_kv.astype(BF16), "branch": w_branch.astype(BF16), "out": w_out.astype(BF16),
        "ln1_g": ln1_g, "ln1_b": ln1_b,
        "router_t": jnp.swapaxes(w_router, 1, 2).astype(BF16), "router_bias": router_bias,
        "exp_gate": w_exp_gate.astype(BF16), "exp_up": w_exp_up.astype(BF16), "exp_down": w_exp_down.astype(BF16),
        "sh_gate": w_sh_gate.astype(BF16), "sh_up": w_sh_up.astype(BF16), "sh_down": w_sh_down.astype(BF16),
        "ln2_g": ln2_g, "ln2_b": ln2_b,
    }
    depth = w_in.shape[0]
    y_prompt, y_sample = _encoder([(x_prompt, mem_prompt), (x_sample, mem_sample)], emb_ln_g, emb_ln_b, w, depth)
    return (y_prompt, y_sample)
```

```python
import functools
import math

import numpy as np
import jax
import jax.numpy as jnp
from jax import lax
from jax.experimental import pallas as pl
from jax.experimental.pallas import tpu as pltpu

F32 = jnp.float32
BF16 = jnp.bfloat16

D_MODEL = 1024
DEPTH = 4
DIL_GROUPS = ((128, 1), (512, 4), (2048, 16))
N_GROUPS_A = 3
HEADS_A = 4
HEAD_DIM_A = 128
GROUP_WIDTH_A = HEADS_A * HEAD_DIM_A
WIDTH_A = N_GROUPS_A * GROUP_WIDTH_A
RADIUS_A = 64
HEADS_B = 8
Q_LORA = 256
KV_LORA = 128
QK_NOPE = 64
QK_ROPE = 32
V_DIM_B = 64
ROPE_THETA = 10000.0
HEAD_PAD_B = 128
HEADS_C = 4
HEAD_DIM_C = 128
OUT_C = HEADS_C * HEAD_DIM_C
N_BRANCH = 3
BRANCH_WIDTH = 512
N_EXPERTS = 64
TOP_K = 8
N_EXPERT_GROUPS = 8
EXPERTS_PER_GROUP = N_EXPERTS // N_EXPERT_GROUPS
TOPK_GROUPS = 4
D_EXPERT = 256
ROUTED_SCALE = 2.5
DEEPNORM_ALPHA = (2 * DEPTH) ** 0.25
LN_EPS = 1e-5
RMS_EPS = 1e-6

SEG_QKV = 3 * WIDTH_A
SEG_GROUP = 3 * GROUP_WIDTH_A
SEG_MLA = Q_LORA + KV_LORA + 2 * HEAD_PAD_B
SEG_QC = OUT_C
SEG_GL = N_BRANCH * D_MODEL
N_PROJ = SEG_QKV + SEG_MLA + SEG_QC + SEG_GL

NEG_BIG = -1e30
VMEM_LIMIT = 56 * 2 ** 20

NT_DIMS = (((1,), (1,)), ((), ()))


def _params(*sem):
    return pltpu.CompilerParams(dimension_semantics=sem, vmem_limit_bytes=VMEM_LIMIT)


def _resident(block_shape, index_map):
    return pl.BlockSpec(block_shape, index_map, pipeline_mode=pl.Buffered(1))


def _layer_norm_rows(h, g, b):
    mu = jnp.mean(h, axis=-1, keepdims=True)
    c = h - mu
    var = jnp.mean(c * c, axis=-1, keepdims=True)
    return c * lax.rsqrt(var + LN_EPS) * g + b


def _rms_norm_rows(h, g):
    return h * lax.rsqrt(jnp.mean(h * h, axis=-1, keepdims=True) + RMS_EPS) * g


X_SLABS = D_MODEL // 128
XG_SLABS = X_SLABS + 1


def _load_token_tiles(ref, n_tokens, rows_per_token):
    return jnp.concatenate([ref[pl.ds(c, n_tokens, stride=rows_per_token), :] for c in range(X_SLABS)], axis=1)


def _store_token_tiles(ref, value, rows_per_token):
    for c in range(X_SLABS):
        ref[pl.ds(c, value.shape[0], stride=rows_per_token), :] = value[:, c * 128:(c + 1) * 128]


EMBED_TM = 512


def _embed_ln_kernel(*refs, tile_starts):
    x_refs, (g_ref, b_ref, xf_ref) = refs[:len(tile_starts)], refs[len(tile_starts):]
    i = pl.program_id(0)
    for k, x_ref in enumerate(x_refs):
        lo = tile_starts[k]
        hi = tile_starts[k + 1] if k + 1 < len(tile_starts) else pl.num_programs(0)

        @pl.when((i >= lo) & (i < hi))
        def _():
            _store_token_tiles(xf_ref, _layer_norm_rows(x_ref[...], g_ref[...], b_ref[...]), X_SLABS)


def _embed_ln(xs, g, b):
    d = xs[0].shape[1]
    tm = EMBED_TM
    tiles = [x.shape[0] // tm for x in xs]
    starts = tuple(sum(tiles[:k]) for k in range(len(tiles)))
    total = sum(tiles)
    fixed = lambda i: (0, 0)

    def group_rows(start, n):
        return lambda i: (jnp.clip(i - start, 0, n - 1), 0)

    return pl.pallas_call(
        functools.partial(_embed_ln_kernel, tile_starts=starts),
        grid=(total,),
        in_specs=[pl.BlockSpec((tm, d), group_rows(s, n)) for s, n in zip(starts, tiles)]
                 + [pl.BlockSpec((1, d), fixed), pl.BlockSpec((1, d), fixed)],
        out_specs=pl.BlockSpec((tm * X_SLABS, 128), lambda i: (i, 0)),
        out_shape=jax.ShapeDtypeStruct((total * tm * X_SLABS, 128), F32),
        compiler_params=_params("arbitrary"),
        name="embed_ln",
    )(*xs, g, b)


def _untile_kernel(x_ref, o_ref):
    o_ref[...] = _load_token_tiles(x_ref, EMBED_TM, X_SLABS)


def _untile(x, row_offset, n_tokens):
    tm = EMBED_TM
    assert row_offset % tm == 0
    return pl.pallas_call(
        _untile_kernel,
        grid=(n_tokens // tm,),
        in_specs=[pl.BlockSpec((tm * X_SLABS, 128), lambda i: (row_offset // tm + i, 0))],
        out_specs=pl.BlockSpec((tm, D_MODEL), lambda i: (i, 0)),
        out_shape=jax.ShapeDtypeStruct((n_tokens, D_MODEL), F32),
        compiler_params=_params("parallel"),
        name="untile",
    )(x)


PROJ_CHUNK = 512
PROJ_TM = 512
LANE_SLABS = GROUP_WIDTH_A // 128


def _proj_kernel(x_ref, w_ref, qkv0_ref, cm1_ref, cm2_ref, mla_ref, qc_ref, gl_ref, slab_ref):
    xb = _load_token_tiles(x_ref, PROJ_TM, X_SLABS).astype(BF16)

    def chunk(col, width):
        return jnp.dot(xb, w_ref[:, col:col + width], preferred_element_type=F32)

    col = 0
    for c in range(0, SEG_GROUP, PROJ_CHUNK):
        qkv0_ref[:, c:c + PROJ_CHUNK] = chunk(col + c, PROJ_CHUNK).astype(BF16)
    col += SEG_GROUP
    for ref, (_, d) in ((cm1_ref, DIL_GROUPS[1]), (cm2_ref, DIL_GROUPS[2])):
        n = PROJ_TM // d
        for c in range(0, SEG_GROUP, GROUP_WIDTH_A):
            res = chunk(col + c, GROUP_WIDTH_A)
            for s in range(LANE_SLABS):
                slab_ref[s] = res[:, s * 128:(s + 1) * 128]
            for r in range(d):
                piece = jnp.concatenate([slab_ref[s, pl.ds(r, n, stride=d), :] for s in range(LANE_SLABS)], axis=1)
                ref[0, r, :, c:c + GROUP_WIDTH_A] = piece.astype(BF16)
        col += SEG_GROUP
    for ref, width in ((mla_ref, SEG_MLA), (qc_ref, SEG_QC), (gl_ref, SEG_GL)):
        for c in range(0, width, PROJ_CHUNK):
            w = min(PROJ_CHUNK, width - c)
            ref[:, c:c + w] = chunk(col + c, w).astype(BF16)
        col += width


def _project(x, row_offset, w, batch, seq):
    d = D_MODEL
    t = batch * seq
    tm = PROJ_TM
    tiles_per_seq = seq // tm
    assert row_offset % tm == 0
    row = lambda i: (i, 0)
    x_row = lambda i: (row_offset // tm + i, 0)

    def class_major(dil):
        shape = (batch, dil, seq // dil, SEG_GROUP)
        spec = pl.BlockSpec((1, dil, tm // dil, SEG_GROUP), lambda i: (i // tiles_per_seq, 0, i % tiles_per_seq, 0))
        return jax.ShapeDtypeStruct(shape, BF16), spec

    (cm1_shape, cm1_spec), (cm2_shape, cm2_spec) = class_major(DIL_GROUPS[1][1]), class_major(DIL_GROUPS[2][1])
    flat = (SEG_GROUP, SEG_MLA, SEG_QC, SEG_GL)
    flat_shapes = [jax.ShapeDtypeStruct((t, n), BF16) for n in flat]
    flat_specs = [pl.BlockSpec((tm, n), row) for n in flat]
    return pl.pallas_call(
        _proj_kernel,
        grid=(t // tm,),
        in_specs=[pl.BlockSpec((tm * X_SLABS, 128), x_row), _resident((d, N_PROJ), lambda i: (0, 0))],
        out_specs=[flat_specs[0], cm1_spec, cm2_spec] + flat_specs[1:],
        out_shape=[flat_shapes[0], cm1_shape, cm2_shape] + flat_shapes[1:],
        scratch_shapes=[pltpu.VMEM((LANE_SLABS, tm, 128), F32)],
        compiler_params=_params("parallel"),
        name="in_proj",
    )(x, w)


BAND_TQ = 512
BAND_QB = 128
BAND_KB = BAND_QB + 2 * RADIUS_A


def _band_kernel(q_ref, kp_ref, km_ref, kn_ref, vp_ref, vm_ref, vn_ref, o_ref, lse_ref, k_scr, v_scr,
                 *, seq_len, slopes):
    i = pl.program_id(1)
    r = RADIUS_A
    k_scr[0:r, :] = kp_ref[...]
    k_scr[r:r + BAND_TQ, :] = km_ref[...]
    k_scr[r + BAND_TQ:, :] = kn_ref[...]
    v_scr[0:r, :] = vp_ref[...]
    v_scr[r:r + BAND_TQ, :] = vm_ref[...]
    v_scr[r + BAND_TQ:, :] = vn_ref[...]

    row = lax.broadcasted_iota(jnp.int32, (BAND_QB, BAND_KB), 0)
    col = lax.broadcasted_iota(jnp.int32, (BAND_QB, BAND_KB), 1)
    rel = col - r - row
    dist = jnp.abs(rel).astype(F32)
    in_band = jnp.abs(rel) <= r
    scale = HEAD_DIM_A ** -0.5

    for qb in range(BAND_TQ // BAND_QB):
        key_pos = i * BAND_TQ + qb * BAND_QB - r + col
        valid = in_band & (key_pos >= 0) & (key_pos < seq_len)
        for h in range(HEADS_A):
            lanes = slice(h * HEAD_DIM_A, (h + 1) * HEAD_DIM_A)
            q = q_ref[qb * BAND_QB:(qb + 1) * BAND_QB, lanes]
            k = k_scr[qb * BAND_QB:qb * BAND_QB + BAND_KB, lanes]
            v = v_scr[qb * BAND_QB:qb * BAND_QB + BAND_KB, lanes]
            s = lax.dot_general(q, k, NT_DIMS, preferred_element_type=F32)
            logits = jnp.where(valid, s * scale - slopes[h] * dist, NEG_BIG)
            m = jnp.max(logits, axis=1, keepdims=True)
            p = jnp.exp(logits - m)
            den = jnp.sum(p, axis=1, keepdims=True)
            o = jnp.dot(p.astype(BF16), v, preferred_element_type=F32) / den
            rows = slice(qb * BAND_QB, (qb + 1) * BAND_QB)
            o_ref[rows, lanes] = o.astype(BF16)
            lse_ref[rows, lanes] = jnp.broadcast_to(m + jnp.log(den), (BAND_QB, HEAD_DIM_A))


def _band_attention(q_src, k_src, v_src, n_seq, seq_len, slopes):
    tq, r = BAND_TQ, RADIUS_A
    assert seq_len % tq == 0 and tq % r == 0
    steps = seq_len // tq
    halo_per_tile = tq // r
    halo_blocks = seq_len // r

    def main_map(cb):
        return lambda n, i: (n * steps + i, cb)

    def prev_map(cb):
        return lambda n, i: (n * halo_blocks + jnp.maximum(i * halo_per_tile - 1, 0), cb)

    def next_map(cb):
        return lambda n, i: (n * halo_blocks + jnp.minimum((i + 1) * halo_per_tile, halo_blocks - 1), cb)

    (qa, qcb), (ka, kcb), (va, vcb) = q_src, k_src, v_src
    w = GROUP_WIDTH_A
    rows = n_seq * seq_len
    return pl.pallas_call(
        functools.partial(_band_kernel, seq_len=seq_len, slopes=slopes),
        grid=(n_seq, steps),
        in_specs=[
            pl.BlockSpec((tq, w), main_map(qcb)),
            pl.BlockSpec((r, w), prev_map(kcb)), pl.BlockSpec((tq, w), main_map(kcb)),
            pl.BlockSpec((r, w), next_map(kcb)),
            pl.BlockSpec((r, w), prev_map(vcb)), pl.BlockSpec((tq, w), main_map(vcb)),
            pl.BlockSpec((r, w), next_map(vcb)),
        ],
        out_specs=[pl.BlockSpec((tq, w), main_map(0)), pl.BlockSpec((tq, w), main_map(0))],
        out_shape=[jax.ShapeDtypeStruct((rows, w), BF16), jax.ShapeDtypeStruct((rows, w), F32)],
        scratch_shapes=[pltpu.VMEM((tq + 2 * r, w), BF16), pltpu.VMEM((tq + 2 * r, w), BF16)],
        compiler_params=_params("parallel", "parallel"),
        name="band_attention",
    )(qa, ka, ka, ka, va, va, va)


def _alibi_slopes():
    n = N_GROUPS_A * HEADS_A
    return [2.0 ** (-8.0 * (i + 1) / n) for i in range(n)]


def _dilated_mixer(group_qkv, batch, seq):
    slopes = _alibi_slopes()
    outs = []
    for g, (_, d) in enumerate(DIL_GROUPS):
        group_slopes = tuple(float(s * d) for s in slopes[g * HEADS_A:(g + 1) * HEADS_A])
        rows = group_qkv[g].reshape(batch * seq, SEG_GROUP)
        o, lse = _band_attention((rows, 0), (rows, 1), (rows, 2), batch * d, seq // d, group_slopes)
        if d > 1:
            o = o.reshape(batch, d, seq // d, GROUP_WIDTH_A)
            lse = lse.reshape(batch, d, seq // d, GROUP_WIDTH_A)
        outs.append((o, lse))
    return outs


V_ROWS_B = 80
MLA_TQ = 1024
MLA_TK = 4096
MLA_VCHUNK = 1024
MLA_KS = 256
MLA_QS = 512
MLA_LOOKAHEAD = 2
MLA_GAP_LIMIT = 64.0
MLA_HEADS_PER_STEP = 2


def _mla_prep_kernel(mla_ref, cos_ref, sin_ref, cos_t_ref, sin_t_ref, gq_ref, wqa_ref, wqb_ref, gkv_ref,
                     wk_ref, wv_ref, qt_ref, k_ref, vt_ref):
    m = mla_ref[...]
    cq = m[:, 0:Q_LORA].astype(F32)
    ckv = m[:, Q_LORA:Q_LORA + KV_LORA].astype(F32)
    kr = m[:, Q_LORA + KV_LORA:Q_LORA + KV_LORA + HEAD_PAD_B].astype(F32)
    kr_rot = m[:, Q_LORA + KV_LORA + HEAD_PAD_B:].astype(F32)
    scale = (QK_NOPE + QK_ROPE) ** -0.5 * math.log2(math.e)

    cqn = _rms_norm_rows(cq, gq_ref[...]).astype(BF16)
    qa_t = lax.dot_general(wqa_ref[...], cqn, NT_DIMS, preferred_element_type=F32)
    qb_t = lax.dot_general(wqb_ref[...], cqn, NT_DIMS, preferred_element_type=F32)
    ckvn = _rms_norm_rows(ckv, gkv_ref[...]).astype(BF16)
    kn = jnp.dot(ckvn, wk_ref[...], preferred_element_type=F32)
    v_t = lax.dot_general(wv_ref[...], ckvn, NT_DIMS, preferred_element_type=F32)
    k_rope = kr * cos_ref[...] + kr_rot * sin_ref[...]
    cos_t, sin_t = cos_t_ref[...], sin_t_ref[...]
    row = lax.broadcasted_iota(jnp.int32, (V_ROWS_B, 1), 0)
    ones_row = (row == V_DIM_B).astype(F32)
    for h in range(HEADS_B):
        slot = slice(h * HEAD_PAD_B, (h + 1) * HEAD_PAD_B)
        qt_ref[slot, :] = ((qa_t[slot, :] * cos_t + qb_t[slot, :] * sin_t) * scale).astype(BF16)
        k_ref[:, slot] = (kn[:, slot] + k_rope).astype(BF16)
        vt_ref[0, h, 0] = (v_t[h * V_ROWS_B:(h + 1) * V_ROWS_B, :] + ones_row).astype(BF16)


def _mla_prep(mla, tables, gq, wqa_t, wqb_t, gkv, wk, wv_t, batch, seq):
    t = mla.shape[0]
    tm = MLA_VCHUNK
    chunks = seq // tm
    cos, sin, cos_t, sin_t = tables
    row = lambda i: (i, 0)
    pos = lambda i: (i % chunks, 0)
    pos_t = lambda i: (0, i % chunks)
    fixed = lambda i: (0, 0)
    wide = HEADS_B * HEAD_PAD_B
    return pl.pallas_call(
        _mla_prep_kernel,
        grid=(t // tm,),
        in_specs=[pl.BlockSpec((tm, SEG_MLA), row),
                  pl.BlockSpec((tm, HEAD_PAD_B), pos), pl.BlockSpec((tm, HEAD_PAD_B), pos),
                  pl.BlockSpec((HEAD_PAD_B, tm), pos_t), pl.BlockSpec((HEAD_PAD_B, tm), pos_t),
                  pl.BlockSpec((1, Q_LORA), fixed), pl.BlockSpec((wide, Q_LORA), fixed),
                  pl.BlockSpec((wide, Q_LORA), fixed),
                  pl.BlockSpec((1, KV_LORA), fixed), pl.BlockSpec((KV_LORA, wide), fixed),
                  pl.BlockSpec((HEADS_B * V_ROWS_B, KV_LORA), fixed)],
        out_specs=[pl.BlockSpec((wide, tm), lambda i: (i // chunks, i % chunks)),
                   pl.BlockSpec((tm, wide), row),
                   pl.BlockSpec((1, HEADS_B, 1, V_ROWS_B, tm), lambda i: (i // chunks, 0, i % chunks, 0, 0))],
        out_shape=[jax.ShapeDtypeStruct((batch * wide, seq), BF16),
                   jax.ShapeDtypeStruct((t, wide), BF16),
                   jax.ShapeDtypeStruct((batch, HEADS_B, chunks, V_ROWS_B, tm), BF16)],
        compiler_params=_params("parallel"),
        name="mla_prep",
    )(mla, cos, sin, cos_t, sin_t, gq, wqa_t, wqb_t, gkv, wk, wv_t)


def _mla_attn_kernel(qt_ref, k_ref, vt_ref, o_ref, *, seq):
    n_sub = MLA_TK // MLA_KS
    nq = MLA_TQ // MLA_QS
    units = [(c, hh, j) for c in range(n_sub) for hh in range(MLA_HEADS_PER_STEP) for j in range(nq)]

    def scores(rows, hh, j):
        slot = slice(hh * HEAD_PAD_B, (hh + 1) * HEAD_PAD_B)
        return jnp.dot(k_ref[rows, slot], qt_ref[slot, j * MLA_QS:(j + 1) * MLA_QS],
                       preferred_element_type=F32)

    def sweep(kc, carry, update):
        def unit_scores(u):
            c, hh, j = units[u]
            return scores(pl.ds(pl.multiple_of(kc * MLA_TK + c * MLA_KS, MLA_KS), MLA_KS), hh, j)

        new = list(carry)
        pending = [unit_scores(u) for u in range(min(MLA_LOOKAHEAD, len(units)))]
        worst = None
        for u, (c, hh, j) in enumerate(units):
            if u + MLA_LOOKAHEAD < len(units):
                pending.append(unit_scores(u + MLA_LOOKAHEAD))
            v_chunk, v_off = divmod(c * MLA_KS, MLA_VCHUNK)
            v_blk = vt_ref[0, hh, kc * (MLA_TK // MLA_VCHUNK) + v_chunk, :, v_off:v_off + MLA_KS]
            new[hh * nq + j], gap = update(new[hh * nq + j], pending.pop(0), v_blk)
            worst = gap if worst is None else jnp.maximum(worst, gap)
        return tuple(new), worst

    def exact_update(state, s, v_blk):
        m, acc = state
        m_new = jnp.maximum(m, jnp.max(s, axis=0, keepdims=True))
        p = jnp.exp2(s - m_new).astype(BF16)
        acc = jnp.exp2(m - m_new) * acc + jnp.dot(v_blk, p, preferred_element_type=F32)
        return (m_new, acc), jnp.zeros_like(m)

    def lagged_update(state, s, v_blk):
        m, acc = state
        p = jnp.exp2(s - m).astype(BF16)
        col_max = jnp.max(s, axis=0, keepdims=True)
        gap = col_max - m
        m_new = jnp.maximum(m, col_max)
        acc = (acc + jnp.dot(v_blk, p, preferred_element_type=F32)) * jnp.exp2(m - m_new)
        return (m_new, acc), gap

    def step(kc, carry):
        fast, worst = sweep(kc, carry, lagged_update)
        overflow_risk = jnp.max(worst) > MLA_GAP_LIMIT
        return lax.cond(overflow_risk, lambda: sweep(kc, carry, exact_update)[0], lambda: fast)

    first = pl.ds(0, MLA_KS)
    init = tuple((jnp.max(scores(first, hh, j), axis=0, keepdims=True), jnp.zeros((V_ROWS_B, MLA_QS), F32))
                 for hh in range(MLA_HEADS_PER_STEP) for j in range(nq))
    final = lax.fori_loop(0, seq // MLA_TK, step, init)
    parts = [acc[0:V_DIM_B, :] / acc[V_DIM_B:V_DIM_B + 1, :] for _, acc in final]
    heads = [jnp.concatenate(parts[hh * nq:(hh + 1) * nq], axis=1) for hh in range(MLA_HEADS_PER_STEP)]
    o_ref[...] = jnp.concatenate(heads, axis=0).T.astype(BF16)


def _mla_attention(q_t, k, v_t, batch, seq):
    t = batch * seq
    tq = MLA_TQ
    steps = seq // tq
    chunks = seq // MLA_VCHUNK
    pairs = HEADS_B // MLA_HEADS_PER_STEP
    pair = MLA_HEADS_PER_STEP * HEAD_PAD_B
    return pl.pallas_call(
        functools.partial(_mla_attn_kernel, seq=seq),
        grid=(batch, pairs, steps),
        in_specs=[pl.BlockSpec((pair, tq), lambda b, hp, i: (b * pairs + hp, i)),
                  _resident((seq, pair), lambda b, hp, i: (b, hp)),
                  _resident((1, MLA_HEADS_PER_STEP, chunks, V_ROWS_B, MLA_VCHUNK), lambda b, hp, i: (b, hp, 0, 0, 0))],
        out_specs=pl.BlockSpec((tq, MLA_HEADS_PER_STEP * V_DIM_B), lambda b, hp, i: (b * steps + i, hp)),
        out_shape=jax.ShapeDtypeStruct((t, HEADS_B * V_DIM_B), BF16),
        compiler_params=_params("parallel", "parallel", "arbitrary"),
        name="mla_attention",
    )(q_t, k, v_t)


def _rope_tables(seq):
    inv_freq = 1.0 / (ROPE_THETA ** (jnp.arange(0, QK_ROPE, 2, dtype=F32) / QK_ROPE))
    ang = jnp.arange(seq, dtype=F32)[:, None] * inv_freq[None, :]
    cos, sin = jnp.cos(ang), jnp.sin(ang)
    pad = HEAD_PAD_B - QK_NOPE - QK_ROPE
    cos_s = jnp.concatenate([jnp.ones((seq, QK_NOPE), F32), cos, cos, jnp.zeros((seq, pad), F32)], axis=1)
    sin_s = jnp.concatenate([jnp.zeros((seq, QK_NOPE), F32), sin, sin, jnp.zeros((seq, pad), F32)], axis=1)
    return cos_s, sin_s, cos_s.T, sin_s.T


def _mem_kv_kernel(mem_ref, w_ref, o_ref):
    o_ref[...] = jnp.dot(mem_ref[...], w_ref[...], preferred_element_type=F32).astype(BF16)


def _mem_kv(mem_b, w):
    rows, d = mem_b.shape
    n = w.shape[1]
    tm = 256
    return pl.pallas_call(
        _mem_kv_kernel,
        grid=(rows // tm,),
        in_specs=[pl.BlockSpec((tm, d), lambda i: (i, 0)), pl.BlockSpec((d, n), lambda i: (0, 0))],
        out_specs=pl.BlockSpec((tm, n), lambda i: (i, 0)),
        out_shape=jax.ShapeDtypeStruct((rows, n), BF16),
        compiler_params=_params("parallel"),
        name="mem_kv",
    )(mem_b, w)


def _mem_attn_kernel(q_ref, kv_ref, o_ref):
    scale = HEAD_DIM_C ** -0.5
    for h in range(HEADS_C):
        lanes = slice(h * HEAD_DIM_C, (h + 1) * HEAD_DIM_C)
        k = kv_ref[:, lanes]
        v = kv_ref[:, OUT_C + h * HEAD_DIM_C:OUT_C + (h + 1) * HEAD_DIM_C]
        s = lax.dot_general(q_ref[:, lanes], k, NT_DIMS, preferred_element_type=F32) * scale
        m = jnp.max(s, axis=1, keepdims=True)
        p = jnp.exp(s - m)
        den = jnp.sum(p, axis=1, keepdims=True)
        o_ref[:, lanes] = (jnp.dot(p.astype(BF16), v, preferred_element_type=F32) / den).astype(BF16)


def _mem_attention(qc, kv, batch, seq, n_mem):
    t = batch * seq
    ts = 1024
    steps = seq // ts
    return pl.pallas_call(
        _mem_attn_kernel,
        grid=(batch, steps),
        in_specs=[pl.BlockSpec((ts, OUT_C), lambda b, i: (b * steps + i, 0)),
                  pl.BlockSpec((n_mem, 2 * OUT_C), lambda b, i: (b, 0))],
        out_specs=pl.BlockSpec((ts, OUT_C), lambda b, i: (b * steps + i, 0)),
        out_shape=jax.ShapeDtypeStruct((t, OUT_C), BF16),
        compiler_params=_params("parallel", "parallel"),
        name="mem_attention",
    )(qc, kv)


def _token_order(cm_ref, slab_ref):
    d, n = cm_ref.shape[1], cm_ref.shape[2]
    for r in range(d):
        blk = cm_ref[0, r].astype(F32)
        for s in range(LANE_SLABS):
            slab_ref[s, pl.ds(r, n, stride=d), :] = blk[:, s * 128:(s + 1) * 128]
    return jnp.concatenate([slab_ref[s] for s in range(LANE_SLABS)], axis=1)


def _merge_kernel(oa0_ref, oa1_ref, oa2_ref, l0_ref, l1_ref, l2_ref, ob_ref, oc_ref, gl_ref, x_ref,
                  wb_ref, wo_ref, g_ref, b_ref, xf_ref, slab_ref):
    l0 = l0_ref[...]
    l1, l2 = _token_order(l1_ref, slab_ref), _token_order(l2_ref, slab_ref)
    m = jnp.maximum(jnp.maximum(l0, l1), l2)
    e0, e1, e2 = jnp.exp(l0 - m), jnp.exp(l1 - m), jnp.exp(l2 - m)
    oa = e0 * oa0_ref[...].astype(F32) + e1 * _token_order(oa1_ref, slab_ref) + e2 * _token_order(oa2_ref, slab_ref)
    oa = (oa / (e0 + e1 + e2)).astype(BF16)
    z = None
    for i, o in enumerate((oa, ob_ref[...], oc_ref[...])):
        gate = jax.nn.sigmoid(gl_ref[:, i * D_MODEL:(i + 1) * D_MODEL].astype(F32))
        term = gate * jnp.dot(o, wb_ref[i], preferred_element_type=F32)
        z = term if z is None else z + term
    y = jnp.dot(z.astype(BF16), wo_ref[...], preferred_element_type=F32)
    x = _load_token_tiles(x_ref, PROJ_TM, X_SLABS)
    _store_token_tiles(xf_ref, _layer_norm_rows(DEEPNORM_ALPHA * x + y, g_ref[...], b_ref[...]), X_SLABS)


def _merge(oa, ob, oc, gl, x, row_offset, wb, wo, g, b, batch, seq):
    d = D_MODEL
    t = batch * seq
    tm = PROJ_TM
    tiles_per_seq = seq // tm
    assert row_offset % tm == 0
    row = lambda i: (i, 0)
    x_row = lambda i: (row_offset // tm + i, 0)
    fixed = lambda i: (0, 0)
    half = pl.BlockSpec((tm, BRANCH_WIDTH), row)

    def class_major(dil):
        return pl.BlockSpec((1, dil, tm // dil, GROUP_WIDTH_A),
                            lambda i: (i // tiles_per_seq, 0, i % tiles_per_seq, 0))

    cm1, cm2 = class_major(DIL_GROUPS[1][1]), class_major(DIL_GROUPS[2][1])
    (oa0, l0), (oa1, l1), (oa2, l2) = oa
    return pl.pallas_call(
        _merge_kernel,
        grid=(t // tm,),
        in_specs=[half, cm1, cm2, half, cm1, cm2, half, half,
                  pl.BlockSpec((tm, SEG_GL), row), pl.BlockSpec((tm * X_SLABS, 128), x_row),
                  _resident((N_BRANCH, BRANCH_WIDTH, d), lambda i: (0, 0, 0)),
                  _resident((d, d), fixed),
                  pl.BlockSpec((1, d), fixed), pl.BlockSpec((1, d), fixed)],
        out_specs=pl.BlockSpec((tm * X_SLABS, 128), x_row),
        out_shape=jax.ShapeDtypeStruct(x.shape, F32),
        input_output_aliases={9: 0},
        scratch_shapes=[pltpu.VMEM((LANE_SLABS, tm, 128), F32)],
        compiler_params=_params("arbitrary"),
        name="merge_ln1",
    )(oa0, oa1, oa2, l0, l1, l2, ob, oc, gl, x, wb, wo, g, b)


GATE_LANES = 128


def _first_index_of_max(vals, idx, axis, sentinel):
    mx = jnp.max(vals, axis=axis, keepdims=True)
    return jnp.min(jnp.where(vals == mx, idx, sentinel), axis=axis, keepdims=True)


def _router_kernel(x_ref, w_ref, bias_ref, xg_ref, code_ref):
    tm = code_ref.shape[1]
    x = _load_token_tiles(x_ref, tm, X_SLABS)
    logits = lax.dot_general(w_ref[...], x.astype(BF16), NT_DIMS, preferred_element_type=F32)
    scores = jax.nn.sigmoid(logits)
    choice = scores + bias_ref[...]
    neg = -jnp.inf

    c3 = choice.reshape(N_EXPERT_GROUPS, EXPERTS_PER_GROUP, tm)
    e_idx = lax.broadcasted_iota(jnp.int32, c3.shape, 1)
    first = jnp.max(c3, axis=1, keepdims=True)
    first_at = jnp.min(jnp.where(c3 == first, e_idx, EXPERTS_PER_GROUP), axis=1, keepdims=True)
    second = jnp.max(jnp.where(e_idx == first_at, neg, c3), axis=1, keepdims=True)
    group_score = (first + second).reshape(N_EXPERT_GROUPS, tm)

    g_idx = lax.broadcasted_iota(jnp.int32, group_score.shape, 0)
    group_sel = jnp.zeros(group_score.shape, jnp.bool_)
    for _ in range(TOPK_GROUPS):
        at = _first_index_of_max(group_score, g_idx, 0, N_EXPERT_GROUPS)
        hit = g_idx == at
        group_sel = group_sel | hit
        group_score = jnp.where(hit, neg, group_score)

    allowed = jnp.broadcast_to(group_sel.reshape(N_EXPERT_GROUPS, 1, tm), c3.shape).reshape(N_EXPERTS, tm)
    cand = jnp.where(allowed, choice, neg)
    x_idx = lax.broadcasted_iota(jnp.int32, cand.shape, 0)
    chosen = jnp.zeros(cand.shape, jnp.bool_)
    for _ in range(TOP_K):
        at = _first_index_of_max(cand, x_idx, 0, N_EXPERTS)
        hit = x_idx == at
        chosen = chosen | hit
        cand = jnp.where(hit, neg, cand)

    w_sel = jnp.where(chosen, scores, 0.0)
    gates = w_sel / jnp.sum(w_sel, axis=0, keepdims=True) * ROUTED_SCALE
    padded = jnp.concatenate([gates, jnp.zeros((GATE_LANES - N_EXPERTS, tm), F32)], axis=0)
    _store_token_tiles(xg_ref, x, XG_SLABS)
    xg_ref[pl.ds(X_SLABS, tm, stride=XG_SLABS), :] = padded.T
    bit = jnp.left_shift(1, g_idx)
    code_ref[...] = jnp.sum(jnp.where(group_sel, bit, 0), axis=0, keepdims=True)


def _router(x, w_t, bias):
    t, d = x.shape[0] // X_SLABS, D_MODEL
    tm = 1024
    return pl.pallas_call(
        _router_kernel,
        grid=(t // tm,),
        in_specs=[pl.BlockSpec((tm * X_SLABS, 128), lambda i: (i, 0)), pl.BlockSpec((N_EXPERTS, d), lambda i: (0, 0)),
                  pl.BlockSpec((N_EXPERTS, 1), lambda i: (0, 0))],
        out_specs=[pl.BlockSpec((tm * XG_SLABS, 128), lambda i: (i, 0)), pl.BlockSpec((1, tm), lambda i: (0, i))],
        out_shape=[jax.ShapeDtypeStruct((t * XG_SLABS, 128), F32), jax.ShapeDtypeStruct((1, t), jnp.int32)],
        compiler_params=_params("parallel"),
        name="router",
    )(x, w_t, bias)


MOE_TM = 1024
MOE_SUB = 128
MOE_NSUB = MOE_TM // MOE_SUB
TOKEN_BITS = 16


def _dispatch_plan(code):
    t = code.shape[1]
    tiles = t // MOE_TM
    assert t <= 1 << TOKEN_BITS
    key = jnp.sort(code[0] * (1 << TOKEN_BITS) + jnp.arange(t, dtype=jnp.int32))
    perm = key & ((1 << TOKEN_BITS) - 1)
    bits = ((key >> TOKEN_BITS)[:, None] >> jnp.arange(N_EXPERT_GROUPS, dtype=jnp.int32)[None, :]) & 1
    sub = jnp.max(bits.reshape(tiles, MOE_NSUB, MOE_SUB, N_EXPERT_GROUPS), axis=2)
    tile_active = jnp.max(sub, axis=1)
    n_active = jnp.sum(tile_active, axis=1).astype(jnp.int32)
    step = jnp.arange(N_EXPERT_GROUPS, dtype=jnp.int32)[None, :]
    odd = (jnp.arange(tiles, dtype=jnp.int32) % 2)[:, None]
    visit_rank = jnp.where(odd == 1, N_EXPERT_GROUPS - 1 - step, step)
    order = jnp.argsort((1 - tile_active) * N_EXPERT_GROUPS + visit_rank, axis=1).astype(jnp.int32)
    last = jnp.take_along_axis(order, jnp.maximum(n_active - 1, 0)[:, None], axis=1)
    groups = jnp.where(step < n_active[:, None], order, last)
    flags = jnp.transpose(sub, (0, 2, 1)).astype(jnp.int32)
    return perm.reshape(tiles, 1, MOE_TM), groups.reshape(-1), n_active, flags.reshape(-1)


def _token_dma(src_ref, dst_ref, src_token, dst_token, rows, sem):
    return pltpu.make_async_copy(src_ref.at[pl.ds(src_token * rows, rows)],
                                 dst_ref.at[pl.ds(dst_token * rows, rows)], sem)


DMA_THREADS = 2


def _gather_tokens_kernel(perm_ref, src_ref, out_ref, sem, *, rows):
    def start(q, carry):
        for u in range(DMA_THREADS):
            r = q * DMA_THREADS + u
            _token_dma(src_ref, out_ref, perm_ref[0, 0, r], r, rows, sem).start(priority=u)
        return carry

    lax.fori_loop(0, MOE_TM // DMA_THREADS, start, 0)
    pltpu.make_async_copy(src_ref.at[pl.ds(0, MOE_TM * rows)], out_ref, sem).wait()


def _gather_tokens(src, perm, rows):
    n = src.shape[0]
    return pl.pallas_call(
        functools.partial(_gather_tokens_kernel, rows=rows),
        grid=(n // (MOE_TM * rows),),
        in_specs=[pl.BlockSpec((1, 1, MOE_TM), lambda i: (i, 0, 0), memory_space=pltpu.SMEM),
                  pl.BlockSpec(memory_space=pl.ANY)],
        out_specs=pl.BlockSpec((MOE_TM * rows, 128), lambda i: (i, 0)),
        out_shape=jax.ShapeDtypeStruct(src.shape, src.dtype),
        scratch_shapes=[pltpu.SemaphoreType.DMA(())],
        compiler_params=_params("arbitrary"),
        name="moe_gather",
    )(perm, src)


def _scatter_tokens_kernel(perm_ref, src_ref, out_ref, sem, *, rows):
    def start(q, carry):
        for u in range(DMA_THREADS):
            r = q * DMA_THREADS + u
            _token_dma(src_ref, out_ref, r, perm_ref[0, 0, r], rows, sem).start(priority=u)
        return carry

    lax.fori_loop(0, MOE_TM // DMA_THREADS, start, 0)
    pltpu.make_async_copy(src_ref, out_ref.at[pl.ds(0, MOE_TM * rows)], sem).wait()


def _scatter_tokens(src, perm, rows):
    n = src.shape[0]
    return pl.pallas_call(
        functools.partial(_scatter_tokens_kernel, rows=rows),
        grid=(n // (MOE_TM * rows),),
        in_specs=[pl.BlockSpec((1, 1, MOE_TM), lambda i: (i, 0, 0), memory_space=pltpu.SMEM),
                  pl.BlockSpec((MOE_TM * rows, 128), lambda i: (i, 0))],
        out_specs=pl.BlockSpec(memory_space=pl.ANY),
        out_shape=jax.ShapeDtypeStruct(src.shape, src.dtype),
        scratch_shapes=[pltpu.SemaphoreType.DMA(())],
        compiler_params=_params("arbitrary"),
        name="moe_scatter",
    )(perm, src)


def _swiglu(xb, wg, wu):
    return jax.nn.silu(jnp.dot(xb, wg, preferred_element_type=F32)) * jnp.dot(xb, wu, preferred_element_type=F32)


def _moe_kernel(groups_ref, nact_ref, flags_ref, xg_ref, wg_ref, wu_ref, wd_ref, sg_ref, su_ref, sd_ref,
                g_ref, b_ref, o_ref, acc_ref, xb_ref, gates_ref):
    i = pl.program_id(0)
    j = pl.program_id(1)

    @pl.when(j == 0)
    def _():
        xb = _load_token_tiles(xg_ref, MOE_TM, XG_SLABS).astype(BF16)
        xb_ref[...] = xb
        gates_ref[...] = xg_ref[pl.ds(X_SLABS, MOE_TM, stride=XG_SLABS), :]
        h = _swiglu(xb, sg_ref[...], su_ref[...])
        acc_ref[...] = jnp.dot(h.astype(BF16), sd_ref[...], preferred_element_type=F32)

    @pl.when(j < nact_ref[i])
    def _():
        group = groups_ref[i * N_EXPERT_GROUPS + j]
        wd_all = wd_ref[...].reshape(EXPERTS_PER_GROUP * D_EXPERT, D_MODEL)
        for s in range(MOE_NSUB):
            @pl.when(flags_ref[(i * N_EXPERT_GROUPS + group) * MOE_NSUB + s] != 0)
            def _():
                rows = slice(s * MOE_SUB, (s + 1) * MOE_SUB)
                xb = xb_ref[rows, :]
                gates = gates_ref[rows, :]
                lane = lax.broadcasted_iota(jnp.int32, gates.shape, 1)
                hs = []
                for e in range(EXPERTS_PER_GROUP):
                    gate = jnp.sum(jnp.where(lane == group * EXPERTS_PER_GROUP + e, gates, 0.0),
                                   axis=1, keepdims=True)
                    hs.append((_swiglu(xb, wg_ref[e], wu_ref[e]) * gate).astype(BF16))
                acc_ref[rows, :] += jnp.dot(jnp.concatenate(hs, axis=1), wd_all, preferred_element_type=F32)

    @pl.when(j == pl.num_programs(1) - 1)
    def _():
        x = _load_token_tiles(xg_ref, MOE_TM, XG_SLABS)
        _store_token_tiles(o_ref, _layer_norm_rows(DEEPNORM_ALPHA * x + acc_ref[...], g_ref[...], b_ref[...]),
                           X_SLABS)


def _moe(xg_sorted, plan, wg, wu, wd, sg, su, sd, g, b):
    t = xg_sorted.shape[0] // XG_SLABS
    d, f, tm, ng = D_MODEL, D_EXPERT, MOE_TM, N_EXPERT_GROUPS
    _, groups, n_active, flags = plan
    row = lambda i, j, *_: (i, 0)
    fixed = lambda i, j, *_: (0, 0)
    expert_block = lambda i, j, groups_ref, *_: (groups_ref[i * ng + j], 0, 0)
    grid_spec = pltpu.PrefetchScalarGridSpec(
        num_scalar_prefetch=3,
        grid=(t // tm, ng),
        in_specs=[pl.BlockSpec((tm * XG_SLABS, 128), row),
                  pl.BlockSpec((EXPERTS_PER_GROUP, d, f), expert_block),
                  pl.BlockSpec((EXPERTS_PER_GROUP, d, f), expert_block),
                  pl.BlockSpec((EXPERTS_PER_GROUP, f, d), expert_block),
                  pl.BlockSpec((d, f), fixed), pl.BlockSpec((d, f), fixed), pl.BlockSpec((f, d), fixed),
                  pl.BlockSpec((1, d), fixed), pl.BlockSpec((1, d), fixed)],
        out_specs=pl.BlockSpec((tm * X_SLABS, 128), row),
        scratch_shapes=[pltpu.VMEM((tm, d), F32), pltpu.VMEM((tm, d), BF16), pltpu.VMEM((tm, GATE_LANES), F32)],
    )
    return pl.pallas_call(
        _moe_kernel,
        grid_spec=grid_spec,
        out_shape=jax.ShapeDtypeStruct((t * X_SLABS, 128), F32),
        compiler_params=_params("arbitrary", "arbitrary"),
        name="moe_ln2",
    )(groups, n_active, flags, xg_sorted, wg, wu, wd, sg, su, sd, g, b)


def _moe_layer(x, w_router_t, router_bias, wg, wu, wd, sg, su, sd, g, b):
    xg, code = _router(x, w_router_t, router_bias.reshape(N_EXPERTS, 1))
    plan = _dispatch_plan(code)
    perm = plan[0]
    y_sorted = _moe(_gather_tokens(xg, perm, XG_SLABS), plan, wg, wu, wd, sg, su, sd, g, b)
    return _scatter_tokens(y_sorted, perm, X_SLABS)


def _rotate_half_columns(w):
    half = QK_ROPE // 2
    return jnp.concatenate([-w[..., half:], w[..., :half]], axis=-1)


def _prep_in_proj(w_in):
    layers, d, _ = w_in.shape
    cuts = np.cumsum((WIDTH_A, WIDTH_A, WIDTH_A, Q_LORA, KV_LORA, QK_ROPE, OUT_C))
    qa, ka, va, cq, ckv, kr, qc, gl = jnp.split(w_in, [int(c) for c in cuts], axis=-1)
    lead = jnp.zeros((layers, d, QK_NOPE), w_in.dtype)
    tail = jnp.zeros((layers, d, HEAD_PAD_B - QK_NOPE - QK_ROPE), w_in.dtype)
    kr_slot = jnp.concatenate([lead, kr, tail], axis=-1)
    kr_rot_slot = jnp.concatenate([lead, _rotate_half_columns(kr), tail], axis=-1)
    per_group = [m[..., g * GROUP_WIDTH_A:(g + 1) * GROUP_WIDTH_A] for g in range(N_GROUPS_A) for m in (qa, ka, va)]
    return jnp.concatenate(per_group + [cq, ckv, kr_slot, kr_rot_slot, qc, gl], axis=-1).astype(BF16)


def _prep_mla_weights(w_q_up, w_kv_up):
    layers = w_q_up.shape[0]
    wq = w_q_up.reshape(layers, Q_LORA, HEADS_B, QK_NOPE + QK_ROPE)
    nope, rope = wq[..., :QK_NOPE], wq[..., QK_NOPE:]
    pad = HEAD_PAD_B - QK_NOPE - QK_ROPE
    zq = lambda n: jnp.zeros((layers, Q_LORA, HEADS_B, n), w_q_up.dtype)
    wqa = jnp.concatenate([nope, rope, zq(pad)], axis=-1)
    wqb = jnp.concatenate([zq(QK_NOPE), _rotate_half_columns(rope), zq(pad)], axis=-1)
    wkv = w_kv_up.reshape(layers, KV_LORA, HEADS_B, QK_NOPE + V_DIM_B)
    zk = lambda n: jnp.zeros((layers, KV_LORA, HEADS_B, n), w_kv_up.dtype)
    wk = jnp.concatenate([wkv[..., :QK_NOPE], zk(HEAD_PAD_B - QK_NOPE)], axis=-1)
    wv = jnp.concatenate([wkv[..., QK_NOPE:], zk(V_ROWS_B - V_DIM_B)], axis=-1)
    flat = lambda w: w.reshape(layers, w.shape[1], -1).astype(BF16)
    flat_t = lambda w: jnp.swapaxes(flat(w), 1, 2)
    return flat_t(wqa), flat_t(wqb), flat(wk), flat_t(wv)


def _encoder(groups, emb_g, emb_b, w, depth):
    d = groups[0][0].shape[-1]
    row2 = lambda v: v.reshape(1, -1)
    sizes = [x.shape[0] * x.shape[1] for x, _ in groups]
    offsets = [sum(sizes[:i]) for i in range(len(sizes))]
    xf = _embed_ln([x.reshape(-1, d) for x, _ in groups], row2(emb_g), row2(emb_b))
    mems = [mem.reshape(-1, d).astype(BF16) for _, mem in groups]
    tables = [_rope_tables(x.shape[1]) for x, _ in groups]
    for l in range(depth):
        for (x, mem), off, mem_b, table in zip(groups, offsets, mems, tables):
            batch, seq, _ = x.shape
            qkv0, cm1, cm2, mla, qc, gl = _project(xf, off, w["in_proj"][l], batch, seq)
            oa = _dilated_mixer((qkv0, cm1, cm2), batch, seq)
            q_t, k, v_t = _mla_prep(mla, table, row2(w["q_norm_g"][l]), w["wqa"][l], w["wqb"][l],
                                    row2(w["kv_norm_g"][l]), w["wk"][l], w["wv"][l], batch, seq)
            ob = _mla_attention(q_t, k, v_t, batch, seq)
            oc = _mem_attention(qc, _mem_kv(mem_b, w["mem_kv"][l]), batch, seq, mem.shape[1])
            xf = _merge(oa, ob, oc, gl, xf, off, w["branch"][l], w["out"][l],
                        row2(w["ln1_g"][l]), row2(w["ln1_b"][l]), batch, seq)
        xf = _moe_layer(xf, w["router_t"][l], w["router_bias"][l], w["exp_gate"][l],
                        w["exp_up"][l], w["exp_down"][l], w["sh_gate"][l], w["sh_up"][l], w["sh_down"][l],
                        row2(w["ln2_g"][l]), row2(w["ln2_b"][l]))
    return [_untile(xf, off, n).reshape(x.shape) for (x, _), off, n in zip(groups, offsets, sizes)]


def kernel(x_prompt, x_sample, mem_prompt, mem_sample, emb_ln_g, emb_ln_b, w_in, q_norm_g, w_q_up, kv_norm_g,
           w_kv_up, w_mem_kv, w_branch, w_out, ln1_g, ln1_b, w_router, router_bias, w_exp_gate, w_exp_up,
           w_exp_down, w_sh_gate, w_sh_up, w_sh_down, ln2_g, ln2_b):
    wqa, wqb, wk, wv = _prep_mla_weights(w_q_up, w_kv_up)
    w = {
        "in_proj": _prep_in_proj(w_in),
        "q_norm_g": q_norm_g, "kv_norm_g": kv_norm_g, "wqa": wqa, "wqb": wqb, "wk": wk, "wv": wv,
        "mem_kv": w_mem_kv.astype(BF16), "branch": w_branch.astype(BF16), "out": w_out.astype(BF16),
        "ln1_g": ln1_g, "ln1_b": ln1_b,
        "router_t": jnp.swapaxes(w_router, 1, 2).astype(BF16), "router_bias": router_bias,
        "exp_gate": w_exp_gate.astype(BF16), "exp_up": w_exp_up.astype(BF16), "exp_down": w_exp_down.astype(BF16),
        "sh_gate": w_sh_gate.astype(BF16), "sh_up": w_sh_up.astype(BF16), "sh_down": w_sh_down.astype(BF16),
        "ln2_g": ln2_g, "ln2_b": ln2_b,
    }
    depth = w_in.shape[0]
    y_prompt, y_sample = _encoder([(x_prompt, mem_prompt), (x_sample, mem_sample)], emb_ln_g, emb_ln_b, w, depth)
    return (y_prompt, y_sample)
```
